```python
import jax, jax.numpy as jnp
from jax import lax
import numpy as np

D_MODEL = 1024
BATCH = 2
SEQ = 8192
DEPTH = 2

HGRN_HEADS = 4
HGRN_DIM = 128
HGRN_WIDTH = HGRN_HEADS * HGRN_DIM
CHUNK = 64
ATTN_HEADS = 8
KV_HEADS = 2
HEAD_DIM = 64
ATTN_WIDTH = ATTN_HEADS * HEAD_DIM
KV_WIDTH = KV_HEADS * HEAD_DIM
GROUP = ATTN_HEADS // KV_HEADS
WINDOW = 128
BLOCK = 128
ROPE_THETA = 500000.0
ROPE_DIM = HEAD_DIM // 4
MIX_WIDTH = HGRN_WIDTH + ATTN_WIDTH
IN_SIZES = (HGRN_WIDTH, HGRN_WIDTH, HGRN_WIDTH, HGRN_WIDTH,
            ATTN_WIDTH, KV_WIDTH, KV_WIDTH)
IN_WIDTH = sum(IN_SIZES)
IN_SPLITS = tuple(int(s) for s in np.cumsum(IN_SIZES)[:-1])
D_FF = 2816
N_MOD = 9
EPS = 1e-6

kernel_name = "hymba_hgrn2_swa_sink_macaron_adaln"


def rmsnorm(x, w):
    xf = x.astype(jnp.float32)
    y = xf * lax.rsqrt(jnp.mean(xf * xf, axis=-1, keepdims=True) + EPS)
    return (y * w.astype(jnp.float32)).astype(x.dtype)


def swiglu(h, w_gate, w_up, w_down):
    return (jax.nn.silu(h @ w_gate) * (h @ w_up)) @ w_down


def partial_rope(t, pos):
    inv_freq = ROPE_THETA ** (-jnp.arange(0, ROPE_DIM, 2, dtype=jnp.float32) / ROPE_DIM)
    ang = pos.astype(jnp.float32)[..., None] * inv_freq
    cos = jnp.cos(ang)[:, :, None, :].astype(t.dtype)
    sin = jnp.sin(ang)[:, :, None, :].astype(t.dtype)
    t1 = t[..., : ROPE_DIM // 2]
    t2 = t[..., ROPE_DIM // 2: ROPE_DIM]
    return jnp.concatenate([t1 * cos - t2 * sin, t2 * cos + t1 * sin, t[..., ROPE_DIM:]], axis=-1)


def hgrn2_mixer(q, f_logit, i, g, lb, norm_w):
    B, T, _ = q.shape
    n = T // CHUNK
    dt = q.dtype
    lb = lb.astype(jnp.float32)
    fl = f_logit.astype(jnp.float32)
    log_f = jnp.logaddexp(jnp.log(lb), jnp.log1p(-lb) + jax.nn.log_sigmoid(fl))
    k = (1.0 - lb) * jax.nn.sigmoid(-fl)

    def to_chunks(t):
        return t.astype(jnp.float32).reshape(B, n, CHUNK, HGRN_HEADS, HGRN_DIM).transpose(1, 0, 3, 2, 4)

    qc = to_chunks(q) * (HGRN_DIM ** -0.5)
    kc, vc, gc = to_chunks(k), to_chunks(i), to_chunks(log_f)
    tril = jnp.tril(jnp.ones((CHUNK, CHUNK), dtype=bool))[:, :, None]

    def step(S, inp):
        qx, kx, vx, gx = inp
        b = jnp.cumsum(gx, axis=-2)
        o_inter = jnp.einsum('bhtd,bhde->bhte', qx * jnp.exp(b), S)
        diff = b[:, :, :, None, :] - b[:, :, None, :, :]
        decay = jnp.exp(jnp.where(tril, diff, -jnp.inf))
        att = jnp.einsum('bhtd,bhtsd,bhsd->bhts', qx, decay, kx)
        o_intra = jnp.einsum('bhts,bhse->bhte', att, vx)
        b_last = b[:, :, -1:, :]
        S_new = jnp.exp(b_last[:, :, 0, :])[..., None] * S + jnp.einsum(
            'bhsd,bhse->bhde', kx * jnp.exp(b_last - b), vx)
        return S_new, o_inter + o_intra

    S0 = jnp.zeros((B, HGRN_HEADS, HGRN_DIM, HGRN_DIM), jnp.float32)
    _, o = lax.scan(step, S0, (qc, kc, vc, gc))
    o = o.transpose(1, 0, 3, 2, 4).reshape(B, T, HGRN_HEADS, HGRN_DIM)
    o = rmsnorm(o, norm_w) * jax.nn.silu(g.astype(jnp.float32).reshape(B, T, HGRN_HEADS, HGRN_DIM))
    return o.reshape(B, T, HGRN_WIDTH).astype(dt)


def swa_sink_mixer(q, k, v, pos, q_norm_w, k_norm_w, sinks):
    B, T, _ = q.shape
    nb = T // BLOCK
    dt = q.dtype
    q = partial_rope(rmsnorm(q.reshape(B, T, ATTN_HEADS, HEAD_DIM), q_norm_w), pos)
    k = partial_rope(rmsnorm(k.reshape(B, T, KV_HEADS, HEAD_DIM), k_norm_w), pos)
    v = v.reshape(B, T, KV_HEADS, HEAD_DIM)
    qb = q.reshape(B, nb, BLOCK, KV_HEADS, GROUP, HEAD_DIM)

    def band(t):
        tb = t.reshape(B, nb, BLOCK, KV_HEADS, HEAD_DIM)
        prev = jnp.concatenate([jnp.zeros_like(tb[:, :1]), tb[:, :-1]], axis=1)
        return jnp.concatenate([prev, tb], axis=2)

    kb, vb = band(k), band(v)
    s = jnp.einsum('bnqkgd,bnskd->bkgnqs', qb, kb).astype(jnp.float32) * (HEAD_DIM ** -0.5)
    qi = jnp.arange(BLOCK)[:, None]
    kj = jnp.arange(2 * BLOCK)[None, :] - BLOCK
    delta = qi - kj
    blk = jnp.arange(nb)[:, None, None]
    valid = ((delta >= 0) & (delta < WINDOW))[None] & (blk * BLOCK + kj[None] >= 0)
    s = jnp.where(valid, s, -jnp.inf)
    sink = sinks.astype(jnp.float32).reshape(KV_HEADS, GROUP)[None, :, :, None, None, None]
    m = jnp.maximum(jnp.max(s, axis=-1, keepdims=True), sink)
    p = jnp.exp(s - m)
    p = p / (jnp.sum(p, axis=-1, keepdims=True) + jnp.exp(sink - m))
    o = jnp.einsum('bkgnqs,bnskd->bnqkgd', p, vb.astype(jnp.float32))
    return o.reshape(B, T, ATTN_WIDTH).astype(dt)


def setup_inputs(seed: int = 0) -> dict:
    key = jax.random.key(seed)
    ks = jax.random.split(key, 24)
    nrm = lambda k, shape, s: jax.random.normal(k, shape, jnp.float32) * s
    gain = lambda k, shape: 1.0 + nrm(k, shape, 0.05)
    x = nrm(ks[0], (BATCH, SEQ, D_MODEL), 1.0)
    c = nrm(ks[1], (BATCH, D_MODEL), 1.0)
    positions = (jnp.arange(SEQ, dtype=jnp.int32)[None, :]
                 + jax.random.randint(ks[2], (BATCH, 1), 0, 1024, dtype=jnp.int32))
    return {
        "x": x,
        "c": c,
        "positions": positions,
        "ada_w": nrm(ks[3], (DEPTH, D_MODEL, N_MOD * D_MODEL), 0.5 * D_MODEL ** -0.5),
        "ada_b": nrm(ks[4], (DEPTH, N_MOD * D_MODEL), 0.02),
        "norm_ffn1": gain(ks[5], (DEPTH, D_MODEL)),
        "ffn1_w_gate": nrm(ks[6], (DEPTH, D_MODEL, D_FF), D_MODEL ** -0.5),
        "ffn1_w_up": nrm(ks[7], (DEPTH, D_MODEL, D_FF), D_MODEL ** -0.5),
        "ffn1_w_down": nrm(ks[8], (DEPTH, D_FF, D_MODEL), D_FF ** -0.5),
        "norm_mix": gain(ks[9], (DEPTH, D_MODEL)),
        "w_in": nrm(ks[10], (DEPTH, D_MODEL, IN_WIDTH), D_MODEL ** -0.5),
        "lb_logits": nrm(ks[11], (DEPTH, HGRN_WIDTH), 1.0),
        "hgrn_norm": gain(ks[12], (DEPTH, HGRN_DIM)),
        "q_norm": gain(ks[13], (DEPTH, HEAD_DIM)),
        "k_norm": gain(ks[14], (DEPTH, HEAD_DIM)),
        "sinks": nrm(ks[15], (DEPTH, ATTN_HEADS), 1.0),
        "w_out": nrm(ks[16], (DEPTH, MIX_WIDTH, D_MODEL), MIX_WIDTH ** -0.5),
        "norm_ffn2": gain(ks[17], (DEPTH, D_MODEL)),
        "ffn2_w_gate": nrm(ks[18], (DEPTH, D_MODEL, D_FF), D_MODEL ** -0.5),
        "ffn2_w_up": nrm(ks[19], (DEPTH, D_MODEL, D_FF), D_MODEL ** -0.5),
        "ffn2_w_down": nrm(ks[20], (DEPTH, D_FF, D_MODEL), D_FF ** -0.5),
    }


def reference(x, c, positions, ada_w, ada_b, norm_ffn1, ffn1_w_gate, ffn1_w_up, ffn1_w_down,
              norm_mix, w_in, lb_logits, hgrn_norm, q_norm, k_norm, sinks, w_out,
              norm_ffn2, ffn2_w_gate, ffn2_w_up, ffn2_w_down):
    B = x.shape[0]
    lb_cum = jnp.cumsum(jax.nn.softmax(lb_logits.astype(jnp.float32), axis=0), axis=0)
    lower_bounds = lb_cum - lb_cum[0:1]
    c_act = jax.nn.silu(c)
    for l in range(DEPTH):
        mod = (c_act @ ada_w[l] + ada_b[l]).reshape(B, N_MOD, 1, D_MODEL)
        sh1, sc1, g1, shm, scm, gm, sh2, sc2, g2 = [mod[:, j] for j in range(N_MOD)]
        h = rmsnorm(x, norm_ffn1[l]) * (1.0 + sc1) + sh1
        x = x + 0.5 * g1 * swiglu(h, ffn1_w_gate[l], ffn1_w_up[l], ffn1_w_down[l])
        h = rmsnorm(x, norm_mix[l]) * (1.0 + scm) + shm
        proj = h @ w_in[l]
        hq, hf, hi, hg, aq, ak, av = jnp.split(proj, IN_SPLITS, axis=-1)
        o_hgrn = hgrn2_mixer(hq, hf, hi, hg, lower_bounds[l], hgrn_norm[l])
        o_attn = swa_sink_mixer(aq, ak, av, positions, q_norm[l], k_norm[l], sinks[l])
        x = x + gm * (jnp.concatenate([o_hgrn, o_attn], axis=-1) @ w_out[l])
        h = rmsnorm(x, norm_ffn2[l]) * (1.0 + sc2) + sh2
        x = x + 0.5 * g2 * swiglu(h, ffn2_w_gate[l], ffn2_w_up[l], ffn2_w_down[l])
    return x
```

```python
import functools

import numpy as np
import jax
import jax.numpy as jnp
from jax import lax
from jax.experimental import pallas as pl
from jax.experimental.pallas import tpu as pltpu

F32 = jnp.float32
BF16 = jnp.bfloat16

HGRN_HEADS = 4
HGRN_DIM = 128
HGRN_WIDTH = HGRN_HEADS * HGRN_DIM
ATTN_HEADS = 8
KV_HEADS = 2
HEAD_DIM = 64
ATTN_WIDTH = ATTN_HEADS * HEAD_DIM
KV_WIDTH = KV_HEADS * HEAD_DIM
GROUP = ATTN_HEADS // KV_HEADS
WINDOW = 128
ROPE_THETA = 500000.0
ROPE_DIM = HEAD_DIM // 4
N_MOD = 9
EPS = 1e-6

LANES = 128
VMEM_LIMIT_BYTES = 56 * 1024 * 1024

MOD_TN = 1024
ROPE_TM = 1024
PROJ_TM = 512
FFN_TM = 512
FFN_CHUNKS = (1536, 1280)
HGRN_CHUNK = 256
HGRN_DIAG = 16


def _params(sem):
    return pltpu.CompilerParams(dimension_semantics=sem, vmem_limit_bytes=VMEM_LIMIT_BYTES)


def _resident(shape):
    nd = len(shape)
    return pl.BlockSpec(shape, lambda *_: (0,) * nd, pipeline_mode=pl.Buffered(1))


def _rms_mod(x, nw, sc, sh):
    ms = jnp.mean(x * x, axis=-1, keepdims=True)
    y = x * lax.rsqrt(ms + EPS)
    return (y * nw) * (1.0 + sc) + sh


def _silu(x):
    return x * jax.nn.sigmoid(x)


def _mod_kernel(ct_ref, w_ref, b_ref, o_ref):
    ca = _silu(ct_ref[...])
    w = w_ref[...]
    rows = [jnp.sum(w * ca[:, b:b + 1], axis=0, keepdims=True) for b in range(ca.shape[1])]
    o_ref[...] = jnp.concatenate(rows, axis=0) + b_ref[...]


def _modulation(c, ada_w, ada_b):
    depth, d, n = ada_w.shape
    bsz = c.shape[0]
    out = pl.pallas_call(
        _mod_kernel,
        grid=(depth, n // MOD_TN),
        in_specs=[
            pl.BlockSpec((d, bsz), lambda l, j: (0, 0)),
            pl.BlockSpec((None, d, MOD_TN), lambda l, j: (l, 0, j)),
            pl.BlockSpec((None, 1, MOD_TN), lambda l, j: (l, 0, j)),
        ],
        out_specs=pl.BlockSpec((None, bsz, MOD_TN), lambda l, j: (l, 0, j)),
        out_shape=jax.ShapeDtypeStruct((depth, bsz, n), F32),
        compiler_params=_params(("arbitrary", "arbitrary")),
        name="adaln_mod",
    )(c.T, ada_w, ada_b.reshape(depth, 1, n))
    return out.reshape(depth, bsz, N_MOD, d)


def _rope_kernel(pos_ref, invf_ref, cos_ref, sin_ref):
    ang = pos_ref[...].astype(F32) * invf_ref[...]
    c = jnp.cos(ang)
    s = jnp.sin(ang)
    lane = lax.broadcasted_iota(jnp.int32, ang.shape, 1)
    s = jnp.where(lane < ROPE_DIM // 2, -s, s)
    cos_ref[...] = jnp.concatenate([c, c], axis=-1)
    sin_ref[...] = jnp.concatenate([s, s], axis=-1)


def _rope_tables(positions):
    bsz, t = positions.shape
    inv_freq = ROPE_THETA ** (-jnp.arange(0, ROPE_DIM, 2, dtype=F32) / ROPE_DIM)
    invf = jnp.concatenate([inv_freq, inv_freq, jnp.zeros((HEAD_DIM - ROPE_DIM,), F32)]).reshape(1, HEAD_DIM)
    shp = jax.ShapeDtypeStruct((bsz, t, 2 * HEAD_DIM), F32)
    return pl.pallas_call(
        _rope_kernel,
        grid=(bsz, t // ROPE_TM),
        in_specs=[
            pl.BlockSpec((None, ROPE_TM, 1), lambda b, i: (b, i, 0)),
            pl.BlockSpec((1, HEAD_DIM), lambda b, i: (0, 0)),
        ],
        out_specs=[pl.BlockSpec((None, ROPE_TM, 2 * HEAD_DIM), lambda b, i: (b, i, 0))] * 2,
        out_shape=[shp, shp],
        compiler_params=_params(("arbitrary", "arbitrary")),
        name="rope_tables",
    )(positions.reshape(bsz, t, 1), invf)


def _head_norm_rope(t, seg, nw, cos, sin, swap_lo, out_scale):
    sq = t * t
    hi = sq.astype(BF16)
    lo = (sq - hi.astype(F32)).astype(BF16)
    ms = (jnp.dot(hi, seg, preferred_element_type=F32) + jnp.dot(lo, seg, preferred_element_type=F32))
    y = (t * lax.rsqrt(ms + EPS)) * nw
    swapped = jnp.where(swap_lo, pltpu.roll(y, LANES - ROPE_DIM // 2, 1), pltpu.roll(y, ROPE_DIM // 2, 1))
    r = y * cos + swapped * sin
    if out_scale != 1.0:
        r = r * out_scale
    return r


def _proj_kernel(x_ref, mod_ref, nw_ref, w_ref, qn_ref, kn_ref, cos_ref, sin_ref, seg_ref,
                 hq_ref, hf_ref, hi_ref, hg_ref, aq_ref, ak_ref, av_ref):
    m = mod_ref[...]
    h = _rms_mod(x_ref[...], nw_ref[...], m[4:5], m[3:4]).astype(BF16)

    def seg_dot(lo, width):
        return jnp.dot(h, w_ref[:, lo:lo + width], preferred_element_type=F32)

    hq_ref[...] = seg_dot(0, HGRN_WIDTH).astype(BF16)
    hf_ref[...] = seg_dot(HGRN_WIDTH, HGRN_WIDTH)
    hi_ref[...] = seg_dot(2 * HGRN_WIDTH, HGRN_WIDTH).astype(BF16)
    hg_ref[...] = seg_dot(3 * HGRN_WIDTH, HGRN_WIDTH).astype(BF16)
    base = 4 * HGRN_WIDTH
    aq = seg_dot(base, ATTN_WIDTH)
    ak = seg_dot(base + ATTN_WIDTH, KV_WIDTH)
    av_ref[...] = seg_dot(base + ATTN_WIDTH + KV_WIDTH, KV_WIDTH).astype(BF16)

    seg = seg_ref[...]
    cos = cos_ref[...]
    sin = sin_ref[...]
    lane = lax.broadcasted_iota(jnp.int32, cos.shape, 1)
    swap_lo = (lane & (HEAD_DIM - 1)) < ROPE_DIM // 2
    qn = qn_ref[...]
    for j in range(ATTN_WIDTH // LANES):
        sl = slice(j * LANES, (j + 1) * LANES)
        aq_ref[:, sl] = _head_norm_rope(aq[:, sl], seg, qn, cos, sin, swap_lo, HEAD_DIM ** -0.5).astype(BF16)
    kn = kn_ref[...]
    for j in range(KV_WIDTH // LANES):
        sl = slice(j * LANES, (j + 1) * LANES)
        ak_ref[:, sl] = _head_norm_rope(ak[:, sl], seg, kn, cos, sin, swap_lo, 1.0).astype(BF16)


def _project(x, mod_l, nw, w_in, q_norm, k_norm, cos_t, sin_t):
    bsz, t, d = x.shape
    tm = PROJ_TM
    in_width = w_in.shape[1]
    seg_np = np.kron(np.eye(LANES // HEAD_DIM), np.full((HEAD_DIM, HEAD_DIM), 1.0 / HEAD_DIM))
    seg = jnp.asarray(seg_np, BF16)
    qn = jnp.tile(q_norm, LANES // HEAD_DIM).reshape(1, LANES)
    kn = jnp.tile(k_norm, LANES // HEAD_DIM).reshape(1, LANES)
    tok = lambda w: pl.BlockSpec((None, tm, w), lambda b, i: (b, i, 0))
    sds = lambda w, dt: jax.ShapeDtypeStruct((bsz, t, w), dt)
    return pl.pallas_call(
        _proj_kernel,
        grid=(bsz, t // tm),
        in_specs=[
            tok(d),
            pl.BlockSpec((None, N_MOD, d), lambda b, i: (b, 0, 0)),
            _resident((1, d)),
            _resident((d, in_width)),
            _resident((1, LANES)),
            _resident((1, LANES)),
            tok(LANES),
            tok(LANES),
            _resident((LANES, LANES)),
        ],
        out_specs=[tok(HGRN_WIDTH)] * 4 + [tok(ATTN_WIDTH), tok(KV_WIDTH), tok(KV_WIDTH)],
        out_shape=[sds(HGRN_WIDTH, BF16), sds(HGRN_WIDTH, F32), sds(HGRN_WIDTH, BF16), sds(HGRN_WIDTH, BF16),
                   sds(ATTN_WIDTH, BF16), sds(KV_WIDTH, BF16), sds(KV_WIDTH, BF16)],
        compiler_params=_params(("arbitrary", "arbitrary")),
        name="in_proj",
    )(x, mod_l, nw.reshape(1, d), w_in, qn, kn, cos_t, sin_t, seg)


def _hgrn_levels():
    lv = []
    m = HGRN_DIAG
    while m < HGRN_CHUNK:
        lv.append(m)
        m *= 2
    return tuple(lv)


def _hgrn_code_table():
    c = HGRN_CHUNK
    t = np.arange(c)[:, None]
    s = np.arange(c)[None, :]
    x = t ^ s
    code = np.full((c, c), -2, np.int32)
    for idx, m in enumerate(_hgrn_levels()):
        code[(x >= m) & (x < 2 * m) & (t > s)] = idx
    code[(t // HGRN_DIAG == s // HGRN_DIAG) & (t >= s)] = -1
    return code


def _split3(g):
    hi = g.astype(BF16)
    r1 = g - hi.astype(F32)
    mid = r1.astype(BF16)
    lo = (r1 - mid.astype(F32)).astype(BF16)
    return hi, mid, lo


def _hgrn_kernel(lbl_ref, nw_ref, q_ref, f_ref, i_ref, g_ref, tri_ref, code_ref, exp_ref, o_ref, st_ref, *, layer):
    c = HGRN_CHUNK
    dg = HGRN_DIAG
    nt_dims = (((1,), (1,)), ((), ()))
    tn_dims = (((0,), (0,)), ((), ()))

    @pl.when(pl.program_id(1) == 0)
    def _():
        st_ref[...] = jnp.zeros(st_ref.shape, F32)

    lbl = lbl_ref[...]
    e = jnp.exp(lbl - jnp.max(lbl, axis=0, keepdims=True))
    sm = e / jnp.sum(e, axis=0, keepdims=True)
    cum0 = sm[0:1]
    cum = cum0
    for j in range(1, layer + 1):
        cum = cum + sm[j:j + 1]
    lb = cum - cum0
    log_lb = jnp.log(lb)
    l1m = jnp.log1p(-lb)

    fl = f_ref[...]
    sp = jnp.log1p(jnp.exp(-jnp.abs(fl)))
    cc = l1m + (jnp.minimum(fl, 0.0) - sp)
    logk = l1m + (-jnp.maximum(fl, 0.0) - sp)
    dlt = log_lb - cc
    g = jnp.maximum(log_lb, cc) + jnp.log1p(jnp.exp(-jnp.abs(dlt)))

    tri = tri_ref[...]
    b_all = sum(jnp.dot(tri, p, preferred_element_type=F32) for p in _split3(g))

    code = code_ref[...]
    expand = exp_ref[...]
    lane = lax.broadcasted_iota(jnp.int32, (c, HGRN_DIM), 1)
    nw = nw_ref[...]

    for hd in range(HGRN_HEADS):
        sl = slice(hd * HGRN_DIM, (hd + 1) * HGRN_DIM)
        q = q_ref[:, sl].astype(F32) * (HGRN_DIM ** -0.5)
        v = i_ref[:, sl]
        b = b_all[:, sl]
        lk = logk[:, sl]
        b_last = b[c - 1:c, :]

        st = st_ref[hd]
        qq0 = (q * jnp.exp(b)).astype(BF16)
        o = lax.dot_general(qq0, st.astype(BF16), nt_dims, preferred_element_type=F32)
        kk_end = jnp.exp(lk + (b_last - b)).astype(BF16)
        st_ref[hd] = st * jnp.exp(b_last) + lax.dot_general(v, kk_end, tn_dims, preferred_element_type=F32)

        att = jnp.zeros((c, c), F32)
        for idx, m in enumerate(_hgrn_levels()):
            b3 = b.reshape(c // (2 * m), 2 * m, HGRN_DIM)
            d = (b3 - b3[:, m - 1:m, :]).reshape(c, HGRN_DIM)
            qm = (q * jnp.exp(jnp.minimum(d, 0.0))).astype(BF16)
            km = jnp.exp(jnp.minimum(lk - d, 0.0)).astype(BF16)
            a = lax.dot_general(qm, km, nt_dims, preferred_element_type=F32)
            att = jnp.where(code == idx, a, att)

        bk3 = (b - lk).reshape(c // dg, dg, HGRN_DIM)
        compact = jnp.zeros((c, HGRN_DIM), F32)
        for s_loc in range(dg):
            sel = jnp.broadcast_to(bk3[:, s_loc:s_loc + 1, :], bk3.shape).reshape(c, HGRN_DIM)
            z = q * jnp.exp(jnp.minimum(b - sel, 0.0))
            col = jnp.sum(z, axis=-1, keepdims=True)
            compact = jnp.where(lane == s_loc, col, compact)
        a_diag = jnp.dot(compact.astype(BF16), expand, preferred_element_type=F32)
        att = jnp.where(code == -1, a_diag, att)

        o = o + jnp.dot(att.astype(BF16), v, preferred_element_type=F32)
        ms = jnp.mean(o * o, axis=-1, keepdims=True)
        y = (o * lax.rsqrt(ms + EPS)) * nw
        o_ref[:, sl] = (y * _silu(g_ref[:, sl].astype(F32))).astype(BF16)


def _hgrn(lb_logits, norm_w, hq, hf, hi, hg, layer):
    bsz, t, w = hq.shape
    c = HGRN_CHUNK
    depth = lb_logits.shape[0]
    tri = jnp.asarray(np.tril(np.ones((c, c))), BF16)
    code = jnp.asarray(_hgrn_code_table())
    expand_np = (np.arange(HGRN_DIM)[:, None] == (np.arange(c)[None, :] % HGRN_DIAG)).astype(np.float32)
    expand = jnp.asarray(expand_np, BF16)
    tok = pl.BlockSpec((None, c, w), lambda b, i: (b, i, 0))
    return pl.pallas_call(
        functools.partial(_hgrn_kernel, layer=layer),
        grid=(bsz, t // c),
        in_specs=[
            _resident((depth, w)),
            _resident((1, HGRN_DIM)),
            tok, tok, tok, tok,
            _resident((c, c)),
            _resident((c, c)),
            _resident((HGRN_DIM, c)),
        ],
        out_specs=tok,
        out_shape=jax.ShapeDtypeStruct((bsz, t, w), BF16),
        scratch_shapes=[pltpu.VMEM((HGRN_HEADS, HGRN_DIM, HGRN_DIM), F32)],
        compiler_params=_params(("arbitrary", "arbitrary")),
        name="hgrn_scan",
    )(lb_logits, norm_w.reshape(1, HGRN_DIM), hq, hf, hi, hg, tri, code, expand)


def _attn_kernel(sink_ref, q_ref, kc_ref, kp_ref, vc_ref, vp_ref, o_ref):
    blk = WINDOW
    n = pl.program_id(1)
    nt_dims = (((1,), (1,)), ((), ()))
    k2 = jnp.concatenate([kp_ref[...], kc_ref[...]], axis=0).astype(F32)
    v2 = jnp.concatenate([vp_ref[...], vc_ref[...]], axis=0).astype(F32)
    k2s = pltpu.roll(k2, HEAD_DIM, 1)
    v2s = pltpu.roll(v2, HEAD_DIM, 1)
    lo = lax.broadcasted_iota(jnp.int32, k2.shape, 1) < HEAD_DIM

    qi = lax.broadcasted_iota(jnp.int32, (blk, 2 * blk), 0)
    kj = lax.broadcasted_iota(jnp.int32, (blk, 2 * blk), 1) - blk
    delta = qi - kj
    ok = jnp.where(delta >= 0, jnp.where(delta < WINDOW, jnp.where(n * blk + kj >= 0, 1, 0), 0), 0)
    bias = jnp.where(ok == 1, 0.0, -jnp.inf).astype(F32)
    bias2 = jnp.concatenate([bias, bias], axis=0)
    top = lax.broadcasted_iota(jnp.int32, (2 * blk, 1), 0) < blk
    lane_lo = lax.broadcasted_iota(jnp.int32, (2 * blk, LANES), 1) < HEAD_DIM

    for kh in range(KV_HEADS):
        own_k, other_k = (k2, k2s) if kh == 0 else (k2s, k2)
        own_v, other_v = (v2, v2s) if kh == 0 else (v2s, v2)
        k_lo = jnp.where(lo, own_k, 0.0).astype(BF16)
        k_hi = jnp.where(lo, 0.0, other_k).astype(BF16)
        v_lo = jnp.where(lo, own_v, 0.0).astype(BF16)
        v_hi = jnp.where(lo, 0.0, other_v).astype(BF16)
        base = kh * GROUP * HEAD_DIM
        qs = jnp.concatenate([q_ref[:, base:base + LANES], q_ref[:, base + LANES:base + 2 * LANES]], axis=0)
        outs = []
        dens = []
        for par, (kk, vv) in enumerate(((k_lo, v_lo), (k_hi, v_hi))):
            s = lax.dot_general(qs, kk, nt_dims, preferred_element_type=F32) + bias2
            sink = jnp.where(top, sink_ref[kh * GROUP + par], sink_ref[kh * GROUP + 2 + par])
            mx = jnp.maximum(jnp.max(s, axis=-1, keepdims=True), sink)
            p = jnp.exp(s - mx)
            dens.append(jnp.sum(p, axis=-1, keepdims=True) + jnp.exp(sink - mx))
            outs.append(jnp.dot(p.astype(BF16), vv, preferred_element_type=F32))
        o = jnp.where(lane_lo, outs[0] / dens[0], outs[1] / dens[1])
        o_ref[:, base:base + LANES] = o[:blk].astype(BF16)
        o_ref[:, base + LANES:base + 2 * LANES] = o[blk:].astype(BF16)


def _attention(sinks_l, aq, ak, av):
    bsz, t, _ = aq.shape
    blk = WINDOW
    cur = lambda w: pl.BlockSpec((None, blk, w), lambda b, i: (b, i, 0))
    prev = lambda w: pl.BlockSpec((None, blk, w), lambda b, i: (b, jnp.maximum(i - 1, 0), 0))
    return pl.pallas_call(
        _attn_kernel,
        grid=(bsz, t // blk),
        in_specs=[
            pl.BlockSpec(memory_space=pltpu.SMEM),
            cur(ATTN_WIDTH), cur(KV_WIDTH), prev(KV_WIDTH), cur(KV_WIDTH), prev(KV_WIDTH),
        ],
        out_specs=cur(ATTN_WIDTH),
        out_shape=jax.ShapeDtypeStruct((bsz, t, ATTN_WIDTH), BF16),
        compiler_params=_params(("arbitrary", "arbitrary")),
        name="swa_attn",
    )(sinks_l, aq, ak, ak, av, av)


def _ffn_kernel(*refs, rows, with_mix):
    if with_mix:
        (x_ref, mod_ref, nw_ref, wg_ref, wu_ref, wd_ref, oh_ref, oa_ref, wo_ref, o_ref, a_scr) = refs
    else:
        (x_ref, mod_ref, nw_ref, wg_ref, wu_ref, wd_ref, o_ref, a_scr) = refs
    m = mod_ref[...]
    x = x_ref[...]
    if with_mix:
        hw = oh_ref.shape[-1]
        mix = (jnp.dot(oh_ref[...], wo_ref[:hw, :], preferred_element_type=F32)
               + jnp.dot(oa_ref[...], wo_ref[hw:, :], preferred_element_type=F32))
        x = x + m[5:6] * mix
    sh, sc, gate = (m[r:r + 1] for r in rows)
    h = _rms_mod(x, nw_ref[...], sc, sh).astype(BF16)
    lo = 0
    for width in FFN_CHUNKS:
        gt = jnp.dot(h, wg_ref[:, lo:lo + width], preferred_element_type=F32)
        ut = jnp.dot(h, wu_ref[:, lo:lo + width], preferred_element_type=F32)
        a_scr[:, lo:lo + width] = (_silu(gt) * ut).astype(BF16)
        lo += width
    y = jnp.dot(a_scr[...], wd_ref[...], preferred_element_type=F32)
    o_ref[...] = x + (0.5 * gate) * y


def _ffn(x, mod_l, nw, wg, wu, wd, rows, mix=None):
    bsz, t, d = x.shape
    tm = FFN_TM
    dff = wg.shape[1]
    assert sum(FFN_CHUNKS) == dff
    tok = lambda w: pl.BlockSpec((None, tm, w), lambda b, i: (b, i, 0))
    in_specs = [
        tok(d),
        pl.BlockSpec((None, N_MOD, d), lambda b, i: (b, 0, 0)),
        _resident((1, d)),
        _resident((d, dff)),
        _resident((d, dff)),
        _resident((dff, d)),
    ]
    args = [x, mod_l, nw.reshape(1, d), wg, wu, wd]
    if mix is not None:
        o_h, o_a, w_out = mix
        in_specs += [tok(o_h.shape[-1]), tok(o_a.shape[-1]), _resident(w_out.shape)]
        args += [o_h, o_a, w_out]
    return pl.pallas_call(
        functools.partial(_ffn_kernel, rows=rows, with_mix=mix is not None),
        grid=(bsz, t // tm),
        in_specs=in_specs,
        out_specs=tok(d),
        out_shape=jax.ShapeDtypeStruct((bsz, t, d), F32),
        scratch_shapes=[pltpu.VMEM((tm, dff), BF16)],
        compiler_params=_params(("arbitrary", "arbitrary")),
        name="ffn_mix" if mix is not None else "ffn",
    )(*args)


def kernel(x, c, positions, ada_w, ada_b, norm_ffn1, ffn1_w_gate, ffn1_w_up, ffn1_w_down, norm_mix, w_in,
           lb_logits, hgrn_norm, q_norm, k_norm, sinks, w_out, norm_ffn2, ffn2_w_gate, ffn2_w_up, ffn2_w_down):
    depth = ada_w.shape[0]
    mod = _modulation(c, ada_w, ada_b)
    cos_t, sin_t = _rope_tables(positions)
    bf = lambda w: w.astype(BF16)
    for l in range(depth):
        x = _ffn(x, mod[l], norm_ffn1[l], bf(ffn1_w_gate[l]), bf(ffn1_w_up[l]), bf(ffn1_w_down[l]), rows=(0, 1, 2))
        hq, hf, hi, hg, aq, ak, av = _project(x, mod[l], norm_mix[l], bf(w_in[l]), q_norm[l], k_norm[l], cos_t, sin_t)
        o_h = _hgrn(lb_logits, hgrn_norm[l], hq, hf, hi, hg, l)
        o_a = _attention(sinks[l], aq, ak, av)
        x = _ffn(x, mod[l], norm_ffn2[l], bf(ffn2_w_gate[l]), bf(ffn2_w_up[l]), bf(ffn2_w_down[l]), rows=(6, 7, 8),
                 mix=(o_h, o_a, bf(w_out[l])))
    return x
```

```python
import functools

import numpy as np
import jax
import jax.numpy as jnp
from jax import lax
from jax.experimental import pallas as pl
from jax.experimental.pallas import tpu as pltpu

F32 = jnp.float32
BF16 = jnp.bfloat16

HGRN_HEADS = 4
HGRN_DIM = 128
HGRN_WIDTH = HGRN_HEADS * HGRN_DIM
ATTN_HEADS = 8
KV_HEADS = 2
HEAD_DIM = 64
ATTN_WIDTH = ATTN_HEADS * HEAD_DIM
KV_WIDTH = KV_HEADS * HEAD_DIM
GROUP = ATTN_HEADS // KV_HEADS
WINDOW = 128
ROPE_THETA = 500000.0
ROPE_DIM = HEAD_DIM // 4
N_MOD = 9
EPS = 1e-6

LANES = 128
VMEM_LIMIT_BYTES = 56 * 1024 * 1024

MOD_TN = 1024
ROPE_TM = 1024
PROJ_TM = 512
FFN_TM = 512
FFN_CHUNKS = (1536, 1280)
HGRN_CHUNK = 256
HGRN_DIAG = 16
HGRN_HALF = 128
LOG2E = 1.4426950408889634


def _params(sem):
    return pltpu.CompilerParams(dimension_semantics=sem, vmem_limit_bytes=VMEM_LIMIT_BYTES)


def _resident(shape):
    nd = len(shape)
    return pl.BlockSpec(shape, lambda *_: (0,) * nd, pipeline_mode=pl.Buffered(1))


def _layer_resident(arr, l):
    nd = arr.ndim - 1
    return pl.BlockSpec((None,) + arr.shape[1:], lambda *_: (l,) + (0,) * nd, pipeline_mode=pl.Buffered(1))


def _mod_spec(mod, l):
    return pl.BlockSpec((None, None) + mod.shape[2:], lambda b, i: (l, b, 0, 0))


def _rms_mod(x, nw, sc, sh):
    ms = jnp.mean(x * x, axis=-1, keepdims=True)
    y = x * lax.rsqrt(ms + EPS)
    return (y * nw) * (1.0 + sc) + sh


def _silu(x):
    return x * jax.nn.sigmoid(x)


def _mod_kernel(ct_ref, w_ref, b_ref, o_ref):
    ca = _silu(ct_ref[...])
    w = w_ref[...]
    rows = [jnp.sum(w * ca[:, b:b + 1], axis=0, keepdims=True) for b in range(ca.shape[1])]
    o_ref[...] = jnp.concatenate(rows, axis=0) + b_ref[...]


def _modulation(c, ada_w, ada_b):
    depth, d, n = ada_w.shape
    bsz = c.shape[0]
    out = pl.pallas_call(
        _mod_kernel,
        grid=(depth, n // MOD_TN),
        in_specs=[
            pl.BlockSpec((d, bsz), lambda l, j: (0, 0)),
            pl.BlockSpec((None, d, MOD_TN), lambda l, j: (l, 0, j)),
            pl.BlockSpec((None, 1, MOD_TN), lambda l, j: (l, 0, j)),
        ],
        out_specs=pl.BlockSpec((None, bsz, MOD_TN), lambda l, j: (l, 0, j)),
        out_shape=jax.ShapeDtypeStruct((depth, bsz, n), F32),
        compiler_params=_params(("arbitrary", "arbitrary")),
        name="adaln_mod",
    )(c.T, ada_w, ada_b.reshape(depth, 1, n))
    return out.reshape(depth, bsz, N_MOD, d)


def _rope_kernel(pos_ref, invf_ref, cos_ref, sin_ref):
    ang = pos_ref[...].astype(F32) * invf_ref[...]
    c = jnp.cos(ang)
    s = jnp.sin(ang)
    lane = lax.broadcasted_iota(jnp.int32, ang.shape, 1)
    s = jnp.where(lane < ROPE_DIM // 2, -s, s)
    cos_ref[...] = jnp.concatenate([c, c], axis=-1)
    sin_ref[...] = jnp.concatenate([s, s], axis=-1)


def _rope_tables(positions):
    bsz, t = positions.shape
    inv_freq = ROPE_THETA ** (-jnp.arange(0, ROPE_DIM, 2, dtype=F32) / ROPE_DIM)
    invf = jnp.concatenate([inv_freq, inv_freq, jnp.zeros((HEAD_DIM - ROPE_DIM,), F32)]).reshape(1, HEAD_DIM)
    shp = jax.ShapeDtypeStruct((bsz, t, 2 * HEAD_DIM), F32)
    return pl.pallas_call(
        _rope_kernel,
        grid=(bsz, t // ROPE_TM),
        in_specs=[
            pl.BlockSpec((None, ROPE_TM, 1), lambda b, i: (b, i, 0)),
            pl.BlockSpec((1, HEAD_DIM), lambda b, i: (0, 0)),
        ],
        out_specs=[pl.BlockSpec((None, ROPE_TM, 2 * HEAD_DIM), lambda b, i: (b, i, 0))] * 2,
        out_shape=[shp, shp],
        compiler_params=_params(("arbitrary", "arbitrary")),
        name="rope_tables",
    )(positions.reshape(bsz, t, 1), invf)


def _head_norm_rope(t, seg, nw, cos, sin, swap_lo, out_scale):
    sq = t * t
    hi = sq.astype(BF16)
    lo = (sq - hi.astype(F32)).astype(BF16)
    ms = (jnp.dot(hi, seg, preferred_element_type=F32) + jnp.dot(lo, seg, preferred_element_type=F32))
    y = (t * lax.rsqrt(ms + EPS)) * nw
    swapped = jnp.where(swap_lo, pltpu.roll(y, LANES - ROPE_DIM // 2, 1), pltpu.roll(y, ROPE_DIM // 2, 1))
    r = y * cos + swapped * sin
    if out_scale != 1.0:
        r = r * out_scale
    return r


def _lower_bound_logs(lbl, layer):
    e = jnp.exp(lbl - jnp.max(lbl, axis=0, keepdims=True))
    sm = e / jnp.sum(e, axis=0, keepdims=True)
    cum0 = sm[0:1]
    cum = cum0
    for j in range(1, layer + 1):
        cum = cum + sm[j:j + 1]
    lb = cum - cum0
    return jnp.log(lb), jnp.log1p(-lb)


def _proj_kernel(x_ref, mod_ref, nw_ref, w_ref, qn_ref, kn_ref, cos_ref, sin_ref, seg_ref, lbl_ref,
                 hq_ref, g2_ref, lk_ref, hi_ref, hg_ref, aq_ref, ak_ref, av_ref, *, layer):
    m = mod_ref[...]
    h = _rms_mod(x_ref[...], nw_ref[...], m[4:5], m[3:4]).astype(BF16)

    def seg_dot(lo, width):
        return jnp.dot(h, w_ref[:, lo:lo + width], preferred_element_type=F32)

    hq_ref[...] = (seg_dot(0, HGRN_WIDTH) * (HGRN_DIM ** -0.5)).astype(BF16)

    log_lb, l1m = _lower_bound_logs(lbl_ref[...], layer)
    fl = seg_dot(HGRN_WIDTH, HGRN_WIDTH)
    sp = jnp.log(1.0 + jnp.exp(-jnp.abs(fl)))
    cc = l1m + (jnp.minimum(fl, 0.0) - sp)
    lk_ref[...] = (l1m + (-jnp.maximum(fl, 0.0) - sp)) * LOG2E
    dlt = log_lb - cc
    g2_ref[...] = (jnp.maximum(log_lb, cc) + jnp.log(1.0 + jnp.exp(-jnp.abs(dlt)))) * LOG2E

    hi_ref[...] = seg_dot(2 * HGRN_WIDTH, HGRN_WIDTH).astype(BF16)
    hg_ref[...] = _silu(seg_dot(3 * HGRN_WIDTH, HGRN_WIDTH)).astype(BF16)
    base = 4 * HGRN_WIDTH
    aq = seg_dot(base, ATTN_WIDTH)
    ak = seg_dot(base + ATTN_WIDTH, KV_WIDTH)
    av_ref[...] = seg_dot(base + ATTN_WIDTH + KV_WIDTH, KV_WIDTH).astype(BF16)

    seg = seg_ref[...]
    cos = cos_ref[...]
    sin = sin_ref[...]
    lane = lax.broadcasted_iota(jnp.int32, cos.shape, 1)
    swap_lo = (lane & (HEAD_DIM - 1)) < ROPE_DIM // 2
    qn = qn_ref[...]
    for j in range(ATTN_WIDTH // LANES):
        sl = slice(j * LANES, (j + 1) * LANES)
        aq_ref[:, sl] = _head_norm_rope(aq[:, sl], seg, qn, cos, sin, swap_lo, HEAD_DIM ** -0.5).astype(BF16)
    kn = kn_ref[...]
    for j in range(KV_WIDTH // LANES):
        sl = slice(j * LANES, (j + 1) * LANES)
        ak_ref[:, sl] = _head_norm_rope(ak[:, sl], seg, kn, cos, sin, swap_lo, 1.0).astype(BF16)


def _project(x, mod, nw, w_in, q_norm, k_norm, cos_t, sin_t, lb_logits, layer):
    bsz, t, d = x.shape
    tm = PROJ_TM
    seg_np = np.kron(np.eye(LANES // HEAD_DIM), np.full((HEAD_DIM, HEAD_DIM), 1.0 / HEAD_DIM))
    seg = jnp.asarray(seg_np, BF16)
    qn = jnp.tile(q_norm, LANES // HEAD_DIM).reshape(1, LANES)
    kn = jnp.tile(k_norm, LANES // HEAD_DIM).reshape(1, LANES)
    tok = lambda w: pl.BlockSpec((None, tm, w), lambda b, i: (b, i, 0))
    sds = lambda w, dt: jax.ShapeDtypeStruct((bsz, t, w), dt)
    return pl.pallas_call(
        functools.partial(_proj_kernel, layer=layer),
        grid=(bsz, t // tm),
        in_specs=[
            tok(d),
            _mod_spec(mod, layer),
            _resident((1, d)),
            _layer_resident(w_in, layer),
            _resident((1, LANES)),
            _resident((1, LANES)),
            tok(LANES),
            tok(LANES),
            _resident((LANES, LANES)),
            _resident(lb_logits.shape),
        ],
        out_specs=[tok(HGRN_WIDTH)] * 5 + [tok(ATTN_WIDTH), tok(KV_WIDTH), tok(KV_WIDTH)],
        out_shape=[sds(HGRN_WIDTH, BF16), sds(HGRN_WIDTH, F32), sds(HGRN_WIDTH, F32), sds(HGRN_WIDTH, BF16),
                   sds(HGRN_WIDTH, BF16), sds(ATTN_WIDTH, BF16), sds(KV_WIDTH, BF16), sds(KV_WIDTH, BF16)],
        compiler_params=_params(("arbitrary", "arbitrary")),
        name="in_proj",
    )(x, mod, nw.reshape(1, d), w_in, qn, kn, cos_t, sin_t, seg, lb_logits)


def _hgrn_inner_levels():
    lv = []
    m = HGRN_DIAG
    while m < HGRN_HALF:
        lv.append(m)
        m *= 2
    return tuple(lv)


def _hgrn_code_table():
    n = HGRN_HALF
    t = np.arange(n)[:, None]
    s = np.arange(n)[None, :]
    x = t ^ s
    code = np.full((n, n), -2, np.int32)
    for idx, m in enumerate(_hgrn_inner_levels()):
        code[(x >= m) & (x < 2 * m) & (t > s)] = idx
    code[(t // HGRN_DIAG == s // HGRN_DIAG) & (t >= s)] = -1
    return np.tile(code, (HGRN_CHUNK // HGRN_HALF, 1))


def _split3(g):
    hi = g.astype(BF16)
    r1 = g - hi.astype(F32)
    mid = r1.astype(BF16)
    lo = (r1 - mid.astype(F32)).astype(BF16)
    return hi, mid, lo


def _hgrn_kernel(nw_ref, q_ref, g2_ref, lk_ref, i_ref, gate_ref, tri_ref, code_ref, exp_ref, o_ref, st_ref):
    c = HGRN_CHUNK
    dg = HGRN_DIAG
    hb = HGRN_HALF
    dh = HGRN_DIM
    nt_dims = (((1,), (1,)), ((), ()))
    tn_dims = (((0,), (0,)), ((), ()))

    @pl.when(pl.program_id(1) == 0)
    def _():
        st_ref[...] = jnp.zeros(st_ref.shape, F32)

    tri = tri_ref[...]
    b_all = sum(jnp.dot(tri, p, preferred_element_type=F32) for p in _split3(g2_ref[...]))

    code = code_ref[...]
    expand = exp_ref[...]
    nw = nw_ref[...]
    nb = c // dg
    half = dg // 2
    lane3 = lax.broadcasted_iota(jnp.int32, (1, half, dh), 2)
    row3 = lax.broadcasted_iota(jnp.int32, (1, half, dh), 1)
    top_code = jnp.where(lane3 <= row3, lane3, -1)
    bot_code = jnp.where(lane3 <= row3 + half, lane3, -1)

    for hd in range(HGRN_HEADS):
        sl = slice(hd * dh, (hd + 1) * dh)
        q = q_ref[:, sl].astype(F32)
        v = i_ref[:, sl]
        b = b_all[:, sl]
        lk = lk_ref[:, sl]
        b_last = b[c - 1:c, :]

        st = st_ref[hd]
        qq0 = (q * jnp.exp2(b)).astype(BF16)
        o = lax.dot_general(qq0, st.astype(BF16), nt_dims, preferred_element_type=F32)
        kk_end = jnp.exp2(lk + (b_last - b)).astype(BF16)
        st_ref[hd] = st * jnp.exp2(b_last) + lax.dot_general(v, kk_end, tn_dims, preferred_element_type=F32)

        att_d = None
        for idx, m in reversed(list(enumerate(_hgrn_inner_levels()))):
            shp = (c // (2 * m), 2 * m, dh)
            b3 = b.reshape(shp)
            mid = b3[:, m - 1:m, :]
            zero = jnp.zeros((shp[0], m, dh), F32)
            q_hi = q.reshape(shp)[:, m:, :] * jnp.exp2(b3[:, m:, :] - mid)
            k_lo = jnp.exp2(lk.reshape(shp)[:, :m, :] + (mid - b3[:, :m, :]))
            qm = jnp.concatenate([zero, q_hi], axis=1).reshape(c, dh).astype(BF16)
            km = jnp.concatenate([k_lo, zero], axis=1).reshape(c, dh).astype(BF16)
            a = lax.dot_general(qm, km, nt_dims, preferred_element_type=F32)
            a_d = jnp.concatenate([a[:hb, :hb], a[hb:, hb:]], axis=0)
            att_d = a_d if att_d is None else jnp.where(code == idx, a_d, att_d)

        mid = b[hb - 1:hb, :]
        q_top = (q[hb:] * jnp.exp2(b[hb:] - mid)).astype(BF16)
        k_top = jnp.exp2(lk[:hb] + (mid - b[:hb])).astype(BF16)
        a_top = lax.dot_general(q_top, k_top, nt_dims, preferred_element_type=F32)

        q3 = q.reshape(nb, dg, dh)
        b3 = b.reshape(nb, dg, dh)
        bk3 = (b - lk).reshape(nb, dg, dh)
        q_bot, b_bot = q3[:, half:, :], b3[:, half:, :]
        c_top = jnp.zeros((nb, half, dh), F32)
        c_bot = jnp.zeros((nb, half, dh), F32)
        for s_loc in range(dg):
            sel = bk3[:, s_loc:s_loc + 1, :]
            if s_loc < half:
                z = q3 * jnp.exp2(jnp.minimum(b3 - sel, 0.0))
                col = jnp.sum(z, axis=-1, keepdims=True)
                c_top = jnp.where(top_code == s_loc, col[:, :half, :], c_top)
                c_bot = jnp.where(bot_code == s_loc, col[:, half:, :], c_bot)
            else:
                z = q_bot * jnp.exp2(jnp.minimum(b_bot - sel, 0.0))
                col = jnp.sum(z, axis=-1, keepdims=True)
                c_bot = jnp.where(bot_code == s_loc, col, c_bot)
        compact = jnp.concatenate([c_top, c_bot], axis=1).reshape(c, dh)
        a_diag = jnp.dot(compact.astype(BF16), expand, preferred_element_type=F32)
        att_d = jnp.where(code == -1, a_diag, att_d)

        att = jnp.concatenate([
            jnp.concatenate([att_d[:hb], jnp.zeros((hb, hb), F32)], axis=1),
            jnp.concatenate([a_top, att_d[hb:]], axis=1)], axis=0)
        o = o + jnp.dot(att.astype(BF16), v, preferred_element_type=F32)
        ms = jnp.mean(o * o, axis=-1, keepdims=True)
        y = (o * lax.rsqrt(ms + EPS)) * nw
        o_ref[:, sl] = (y * gate_ref[:, sl].astype(F32)).astype(BF16)


def _hgrn(norm_w, hq, g2, lk, hi, gate):
    bsz, t, w = hq.shape
    c = HGRN_CHUNK
    assert c == 2 * HGRN_HALF and HGRN_HALF == LANES
    tri = jnp.asarray(np.tril(np.ones((c, c))), BF16)
    code = jnp.asarray(_hgrn_code_table())
    expand_np = (np.arange(HGRN_DIM)[:, None] == (np.arange(HGRN_HALF)[None, :] % HGRN_DIAG)).astype(np.float32)
    expand = jnp.asarray(expand_np, BF16)
    tok = pl.BlockSpec((None, c, w), lambda b, i: (b, i, 0))
    return pl.pallas_call(
        _hgrn_kernel,
        grid=(bsz, t // c),
        in_specs=[
            _resident((1, HGRN_DIM)),
            tok, tok, tok, tok, tok,
            _resident((c, c)),
            _resident((c, HGRN_HALF)),
            _resident((HGRN_DIM, HGRN_HALF)),
        ],
        out_specs=tok,
        out_shape=jax.ShapeDtypeStruct((bsz, t, w), BF16),
        scratch_shapes=[pltpu.VMEM((HGRN_HEADS, HGRN_DIM, HGRN_DIM), F32)],
        compiler_params=_params(("arbitrary", "arbitrary")),
        name="hgrn_scan",
    )(norm_w.reshape(1, HGRN_DIM), hq, g2, lk, hi, gate, tri, code, expand)


def _attn_kernel(sink_ref, q_ref, kc_ref, kp_ref, vc_ref, vp_ref, o_ref):
    blk = WINDOW
    n = pl.program_id(1)
    nt_dims = (((1,), (1,)), ((), ()))
    k2 = jnp.concatenate([kp_ref[...], kc_ref[...]], axis=0).astype(F32)
    v2 = jnp.concatenate([vp_ref[...], vc_ref[...]], axis=0).astype(F32)
    k2s = pltpu.roll(k2, HEAD_DIM, 1)
    v2s = pltpu.roll(v2, HEAD_DIM, 1)
    lo = lax.broadcasted_iota(jnp.int32, k2.shape, 1) < HEAD_DIM

    qi = lax.broadcasted_iota(jnp.int32, (blk, 2 * blk), 0)
    kj = lax.broadcasted_iota(jnp.int32, (blk, 2 * blk), 1) - blk
    delta = qi - kj
    ok = jnp.where(delta >= 0, jnp.where(delta < WINDOW, jnp.where(n * blk + kj >= 0, 1, 0), 0), 0)
    bias = jnp.where(ok == 1, 0.0, -jnp.inf).astype(F32)
    bias2 = jnp.concatenate([bias, bias], axis=0)
    top = lax.broadcasted_iota(jnp.int32, (2 * blk, 1), 0) < blk
    lane_lo = lax.broadcasted_iota(jnp.int32, (2 * blk, LANES), 1) < HEAD_DIM

    for kh in range(KV_HEADS):
        own_k, other_k = (k2, k2s) if kh == 0 else (k2s, k2)
        own_v, other_v = (v2, v2s) if kh == 0 else (v2s, v2)
        k_lo = jnp.where(lo, own_k, 0.0).astype(BF16)
        k_hi = jnp.where(lo, 0.0, other_k).astype(BF16)
        v_lo = jnp.where(lo, own_v, 0.0).astype(BF16)
        v_hi = jnp.where(lo, 0.0, other_v).astype(BF16)
        base = kh * GROUP * HEAD_DIM
        qs = jnp.concatenate([q_ref[:, base:base + LANES], q_ref[:, base + LANES:base + 2 * LANES]], axis=0)
        outs = []
        dens = []
        for par, (kk, vv) in enumerate(((k_lo, v_lo), (k_hi, v_hi))):
            s = lax.dot_general(qs, kk, nt_dims, preferred_element_type=F32) + bias2
            sink = jnp.where(top, sink_ref[kh * GROUP + par], sink_ref[kh * GROUP + 2 + par])
            mx = jnp.maximum(jnp.max(s, axis=-1, keepdims=True), sink)
            p = jnp.exp(s - mx)
            dens.append(jnp.sum(p, axis=-1, keepdims=True) + jnp.exp(sink - mx))
            outs.append(jnp.dot(p.astype(BF16), vv, preferred_element_type=F32))
        o = jnp.where(lane_lo, outs[0] / dens[0], outs[1] / dens[1])
        o_ref[:, base:base + LANES] = o[:blk].astype(BF16)
        o_ref[:, base + LANES:base + 2 * LANES] = o[blk:].astype(BF16)


def _attention(sinks_l, aq, ak, av):
    bsz, t, _ = aq.shape
    blk = WINDOW
    cur = lambda w: pl.BlockSpec((None, blk, w), lambda b, i: (b, i, 0))
    prev = lambda w: pl.BlockSpec((None, blk, w), lambda b, i: (b, jnp.maximum(i - 1, 0), 0))
    return pl.pallas_call(
        _attn_kernel,
        grid=(bsz, t // blk),
        in_specs=[
            pl.BlockSpec(memory_space=pltpu.SMEM),
            cur(ATTN_WIDTH), cur(KV_WIDTH), prev(KV_WIDTH), cur(KV_WIDTH), prev(KV_WIDTH),
        ],
        out_specs=cur(ATTN_WIDTH),
        out_shape=jax.ShapeDtypeStruct((bsz, t, ATTN_WIDTH), BF16),
        compiler_params=_params(("arbitrary", "arbitrary")),
        name="swa_attn",
    )(sinks_l, aq, ak, ak, av, av)


def _ffn_kernel(*refs, rows, with_mix):
    if with_mix:
        (x_ref, mod_ref, nw_ref, wg_ref, wu_ref, wd_ref, oh_ref, oa_ref, wo_ref, o_ref, a_scr) = refs
    else:
        (x_ref, mod_ref, nw_ref, wg_ref, wu_ref, wd_ref, o_ref, a_scr) = refs
    m = mod_ref[...]
    x = x_ref[...]
    if with_mix:
        hw = oh_ref.shape[-1]
        mix = (jnp.dot(oh_ref[...], wo_ref[:hw, :], preferred_element_type=F32)
               + jnp.dot(oa_ref[...], wo_ref[hw:, :], preferred_element_type=F32))
        x = x + m[5:6] * mix
    sh, sc, gate = (m[r:r + 1] for r in rows)
    h = _rms_mod(x, nw_ref[...], sc, sh).astype(BF16)
    lo = 0
    for width in FFN_CHUNKS:
        gt = jnp.dot(h, wg_ref[:, lo:lo + width], preferred_element_type=F32)
        ut = jnp.dot(h, wu_ref[:, lo:lo + width], preferred_element_type=F32)
        a_scr[:, lo:lo + width] = (_silu(gt) * ut).astype(BF16)
        lo += width
    y = jnp.dot(a_scr[...], wd_ref[...], preferred_element_type=F32)
    o_ref[...] = x + (0.5 * gate) * y


def _ffn(x, mod, layer, nw, wg, wu, wd, rows, mix=None):
    bsz, t, d = x.shape
    tm = FFN_TM
    dff = wg.shape[-1]
    assert sum(FFN_CHUNKS) == dff
    tok = lambda w: pl.BlockSpec((None, tm, w), lambda b, i: (b, i, 0))
    in_specs = [
        tok(d),
        _mod_spec(mod, layer),
        _resident((1, d)),
        _layer_resident(wg, layer),
        _layer_resident(wu, layer),
        _layer_resident(wd, layer),
    ]
    args = [x, mod, nw.reshape(1, d), wg, wu, wd]
    if mix is not None:
        o_h, o_a, w_out = mix
        in_specs += [tok(o_h.shape[-1]), tok(o_a.shape[-1]), _layer_resident(w_out, layer)]
        args += [o_h, o_a, w_out]
    return pl.pallas_call(
        functools.partial(_ffn_kernel, rows=rows, with_mix=mix is not None),
        grid=(bsz, t // tm),
        in_specs=in_specs,
        out_specs=tok(d),
        out_shape=jax.ShapeDtypeStruct((bsz, t, d), F32),
        scratch_shapes=[pltpu.VMEM((tm, dff), BF16)],
        compiler_params=_params(("arbitrary", "arbitrary")),
        name="ffn_mix" if mix is not None else "ffn",
    )(*args)


def kernel(x, c, positions, ada_w, ada_b, norm_ffn1, ffn1_w_gate, ffn1_w_up, ffn1_w_down, norm_mix, w_in,
           lb_logits, hgrn_norm, q_norm, k_norm, sinks, w_out, norm_ffn2, ffn2_w_gate, ffn2_w_up, ffn2_w_down):
    depth = ada_w.shape[0]
    mod = _modulation(c, ada_w, ada_b)
    cos_t, sin_t = _rope_tables(positions)
    bf = lambda w: w.astype(BF16)
    wg1, wu1, wd1 = bf(ffn1_w_gate), bf(ffn1_w_up), bf(ffn1_w_down)
    wg2, wu2, wd2 = bf(ffn2_w_gate), bf(ffn2_w_up), bf(ffn2_w_down)
    w_in_b, w_out_b = bf(w_in), bf(w_out)
    for l in range(depth):
        x = _ffn(x, mod, l, norm_ffn1[l], wg1, wu1, wd1, rows=(0, 1, 2))
        hq, g2, lk, hi, gate, aq, ak, av = _project(x, mod, norm_mix[l], w_in_b, q_norm[l], k_norm[l], cos_t, sin_t,
                                                    lb_logits, l)
        o_h = _hgrn(hgrn_norm[l], hq, g2, lk, hi, gate)
        o_a = _attention(sinks[l], aq, ak, av)
        x = _ffn(x, mod, l, norm_ffn2[l], wg2, wu2, wd2, rows=(6, 7, 8), mix=(o_h, o_a, w_out_b))
    return x
```

```python
import functools

import numpy as np
import jax
import jax.numpy as jnp
from jax import lax
from jax.experimental import pallas as pl
from jax.experimental.pallas import tpu as pltpu

F32 = jnp.float32
BF16 = jnp.bfloat16

HGRN_HEADS = 4
HGRN_DIM = 128
HGRN_WIDTH = HGRN_HEADS * HGRN_DIM
ATTN_HEADS = 8
KV_HEADS = 2
HEAD_DIM = 64
ATTN_WIDTH = ATTN_HEADS * HEAD_DIM
KV_WIDTH = KV_HEADS * HEAD_DIM
GROUP = ATTN_HEADS // KV_HEADS
WINDOW = 128
ROPE_THETA = 500000.0
ROPE_DIM = HEAD_DIM // 4
N_MOD = 9
EPS = 1e-6

LANES = 128
VMEM_LIMIT_BYTES = 56 * 1024 * 1024

MOD_TN = 1024
ROPE_TM = 1024
PROJ_TM = 512
FFN_TM = 512
FFN_CHUNKS = (1536, 1280)
HGRN_CHUNK = 256
HGRN_DIAG = 8
ATTN_QB = 4
HGRN_HALF = 128
LOG2E = 1.4426950408889634


def _params(sem):
    return pltpu.CompilerParams(dimension_semantics=sem, vmem_limit_bytes=VMEM_LIMIT_BYTES)


def _resident(shape):
    nd = len(shape)
    return pl.BlockSpec(shape, lambda *_: (0,) * nd, pipeline_mode=pl.Buffered(1))


def _layer_resident(arr, l):
    nd = arr.ndim - 1
    return pl.BlockSpec((None,) + arr.shape[1:], lambda *_: (l,) + (0,) * nd, pipeline_mode=pl.Buffered(1))


def _mod_spec(mod, l):
    return pl.BlockSpec((None, None) + mod.shape[2:], lambda b, i: (l, b, 0, 0))


def _rms_mod(x, nw, sc, sh):
    ms = jnp.mean(x * x, axis=-1, keepdims=True)
    y = x * lax.rsqrt(ms + EPS)
    return (y * nw) * (1.0 + sc) + sh


def _silu(x):
    return x * jax.nn.sigmoid(x)


def _mod_kernel(ct_ref, w_ref, b_ref, o_ref):
    ca = _silu(ct_ref[...])
    w = w_ref[...]
    rows = [jnp.sum(w * ca[:, b:b + 1], axis=0, keepdims=True) for b in range(ca.shape[1])]
    o_ref[...] = jnp.concatenate(rows, axis=0) + b_ref[...]


def _modulation(c, ada_w, ada_b):
    depth, d, n = ada_w.shape
    bsz = c.shape[0]
    out = pl.pallas_call(
        _mod_kernel,
        grid=(depth, n // MOD_TN),
        in_specs=[
            pl.BlockSpec((d, bsz), lambda l, j: (0, 0)),
            pl.BlockSpec((None, d, MOD_TN), lambda l, j: (l, 0, j)),
            pl.BlockSpec((None, 1, MOD_TN), lambda l, j: (l, 0, j)),
        ],
        out_specs=pl.BlockSpec((None, bsz, MOD_TN), lambda l, j: (l, 0, j)),
        out_shape=jax.ShapeDtypeStruct((depth, bsz, n), F32),
        compiler_params=_params(("arbitrary", "arbitrary")),
        name="adaln_mod",
    )(c.T, ada_w, ada_b.reshape(depth, 1, n))
    return out.reshape(depth, bsz, N_MOD, d)


def _rope_kernel(pos_ref, invf_ref, cos_ref, sin_ref):
    ang = pos_ref[...].astype(F32) * invf_ref[...]
    c = jnp.cos(ang)
    s = jnp.sin(ang)
    lane = lax.broadcasted_iota(jnp.int32, ang.shape, 1)
    s = jnp.where(lane < ROPE_DIM // 2, -s, s)
    cos_ref[...] = jnp.concatenate([c, c], axis=-1)
    sin_ref[...] = jnp.concatenate([s, s], axis=-1)


def _rope_tables(positions):
    bsz, t = positions.shape
    inv_freq = ROPE_THETA ** (-jnp.arange(0, ROPE_DIM, 2, dtype=F32) / ROPE_DIM)
    invf = jnp.concatenate([inv_freq, inv_freq, jnp.zeros((HEAD_DIM - ROPE_DIM,), F32)]).reshape(1, HEAD_DIM)
    shp = jax.ShapeDtypeStruct((bsz, t, 2 * HEAD_DIM), F32)
    return pl.pallas_call(
        _rope_kernel,
        grid=(bsz, t // ROPE_TM),
        in_specs=[
            pl.BlockSpec((None, ROPE_TM, 1), lambda b, i: (b, i, 0)),
            pl.BlockSpec((1, HEAD_DIM), lambda b, i: (0, 0)),
        ],
        out_specs=[pl.BlockSpec((None, ROPE_TM, 2 * HEAD_DIM), lambda b, i: (b, i, 0))] * 2,
        out_shape=[shp, shp],
        compiler_params=_params(("arbitrary", "arbitrary")),
        name="rope_tables",
    )(positions.reshape(bsz, t, 1), invf)


def _head_norm_rope(t, seg, nw, cos, sin, swap_lo, out_scale):
    sq = t * t
    hi = sq.astype(BF16)
    lo = (sq - hi.astype(F32)).astype(BF16)
    ms = (jnp.dot(hi, seg, preferred_element_type=F32) + jnp.dot(lo, seg, preferred_element_type=F32))
    y = (t * lax.rsqrt(ms + EPS)) * nw
    swapped = jnp.where(swap_lo, pltpu.roll(y, LANES - ROPE_DIM // 2, 1), pltpu.roll(y, ROPE_DIM // 2, 1))
    r = y * cos + swapped * sin
    if out_scale != 1.0:
        r = r * out_scale
    return r


def _lower_bound_logs(lbl, layer):
    e = jnp.exp(lbl - jnp.max(lbl, axis=0, keepdims=True))
    sm = e / jnp.sum(e, axis=0, keepdims=True)
    cum0 = sm[0:1]
    cum = cum0
    for j in range(1, layer + 1):
        cum = cum + sm[j:j + 1]
    lb = cum - cum0
    return jnp.log(lb), jnp.log1p(-lb)


def _proj_kernel(x_ref, mod_ref, nw_ref, w_ref, qn_ref, kn_ref, cos_ref, sin_ref, seg_ref, lbl_ref,
                 hq_ref, g2_ref, lk_ref, hi_ref, hg_ref, aq_ref, ak_ref, av_ref, *, layer):
    m = mod_ref[...]
    h = _rms_mod(x_ref[...], nw_ref[...], m[4:5], m[3:4]).astype(BF16)

    def seg_dot(lo, width):
        return jnp.dot(h, w_ref[:, lo:lo + width], preferred_element_type=F32)

    hq_ref[...] = (seg_dot(0, HGRN_WIDTH) * (HGRN_DIM ** -0.5)).astype(BF16)

    log_lb, l1m = _lower_bound_logs(lbl_ref[...], layer)
    fl = seg_dot(HGRN_WIDTH, HGRN_WIDTH)
    sp = jnp.log(1.0 + jnp.exp(-jnp.abs(fl)))
    cc = l1m + (jnp.minimum(fl, 0.0) - sp)
    lk_ref[...] = (l1m + (-jnp.maximum(fl, 0.0) - sp)) * LOG2E
    dlt = log_lb - cc
    g2_ref[...] = (jnp.maximum(log_lb, cc) + jnp.log(1.0 + jnp.exp(-jnp.abs(dlt)))) * LOG2E

    hi_ref[...] = seg_dot(2 * HGRN_WIDTH, HGRN_WIDTH).astype(BF16)
    hg_ref[...] = _silu(seg_dot(3 * HGRN_WIDTH, HGRN_WIDTH)).astype(BF16)
    base = 4 * HGRN_WIDTH
    aq = seg_dot(base, ATTN_WIDTH)
    ak = seg_dot(base + ATTN_WIDTH, KV_WIDTH)
    av_ref[...] = seg_dot(base + ATTN_WIDTH + KV_WIDTH, KV_WIDTH).astype(BF16)

    seg = seg_ref[...]
    cos = cos_ref[...]
    sin = sin_ref[...]
    lane = lax.broadcasted_iota(jnp.int32, cos.shape, 1)
    swap_lo = (lane & (HEAD_DIM - 1)) < ROPE_DIM // 2
    qn = qn_ref[...]
    for j in range(ATTN_WIDTH // LANES):
        sl = slice(j * LANES, (j + 1) * LANES)
        aq_ref[:, sl] = _head_norm_rope(aq[:, sl], seg, qn, cos, sin, swap_lo, LOG2E * HEAD_DIM ** -0.5).astype(BF16)
    kn = kn_ref[...]
    for j in range(KV_WIDTH // LANES):
        sl = slice(j * LANES, (j + 1) * LANES)
        ak_ref[:, sl] = _head_norm_rope(ak[:, sl], seg, kn, cos, sin, swap_lo, 1.0).astype(BF16)


def _project(x, mod, nw, w_in, q_norm, k_norm, cos_t, sin_t, lb_logits, layer):
    bsz, t, d = x.shape
    tm = PROJ_TM
    seg_np = np.kron(np.eye(LANES // HEAD_DIM), np.full((HEAD_DIM, HEAD_DIM), 1.0 / HEAD_DIM))
    seg = jnp.asarray(seg_np, BF16)
    qn = jnp.tile(q_norm, LANES // HEAD_DIM).reshape(1, LANES)
    kn = jnp.tile(k_norm, LANES // HEAD_DIM).reshape(1, LANES)
    tok = lambda w: pl.BlockSpec((None, tm, w), lambda b, i: (b, i, 0))
    sds = lambda w, dt: jax.ShapeDtypeStruct((bsz, t, w), dt)
    return pl.pallas_call(
        functools.partial(_proj_kernel, layer=layer),
        grid=(bsz, t // tm),
        in_specs=[
            tok(d),
            _mod_spec(mod, layer),
            _resident((1, d)),
            _layer_resident(w_in, layer),
            _resident((1, LANES)),
            _resident((1, LANES)),
            tok(LANES),
            tok(LANES),
            _resident((LANES, LANES)),
            _resident(lb_logits.shape),
        ],
        out_specs=[tok(HGRN_WIDTH)] * 5 + [tok(ATTN_WIDTH), tok(KV_WIDTH), tok(KV_WIDTH)],
        out_shape=[sds(HGRN_WIDTH, BF16), sds(HGRN_WIDTH, F32), sds(HGRN_WIDTH, F32), sds(HGRN_WIDTH, BF16),
                   sds(HGRN_WIDTH, BF16), sds(ATTN_WIDTH, BF16), sds(KV_WIDTH, BF16), sds(KV_WIDTH, BF16)],
        compiler_params=_params(("arbitrary", "arbitrary")),
        name="in_proj",
    )(x, mod, nw.reshape(1, d), w_in, qn, kn, cos_t, sin_t, seg, lb_logits)


def _hgrn_inner_levels():
    lv = []
    m = HGRN_DIAG
    while m < HGRN_HALF:
        lv.append(m)
        m *= 2
    return tuple(lv)


def _hgrn_code_table():
    n = HGRN_HALF
    t = np.arange(n)[:, None]
    s = np.arange(n)[None, :]
    x = t ^ s
    code = np.full((n, n), -2, np.int32)
    for idx, m in enumerate(_hgrn_inner_levels()):
        code[(x >= m) & (x < 2 * m) & (t > s)] = idx
    code[(t // HGRN_DIAG == s // HGRN_DIAG) & (t >= s)] = -1
    return np.tile(code, (HGRN_CHUNK // HGRN_HALF, 1))


def _split3(g):
    hi = g.astype(BF16)
    r1 = g - hi.astype(F32)
    mid = r1.astype(BF16)
    lo = (r1 - mid.astype(F32)).astype(BF16)
    return hi, mid, lo


def _hgrn_kernel(nw_ref, q_ref, g2_ref, lk_ref, i_ref, gate_ref, tri_ref, code_ref, exp_ref, o_ref, st_ref):
    c = HGRN_CHUNK
    dg = HGRN_DIAG
    hb = HGRN_HALF
    dh = HGRN_DIM
    nt_dims = (((1,), (1,)), ((), ()))
    tn_dims = (((0,), (0,)), ((), ()))

    @pl.when(pl.program_id(1) == 0)
    def _():
        st_ref[...] = jnp.zeros(st_ref.shape, F32)

    tri = tri_ref[...]
    b_all = sum(jnp.dot(tri, p, preferred_element_type=F32) for p in _split3(g2_ref[...]))

    code = code_ref[...]
    expand = exp_ref[...]
    nw = nw_ref[...]
    nb = c // dg
    lane3 = lax.broadcasted_iota(jnp.int32, (1, dg, dh), 2)
    row3 = lax.broadcasted_iota(jnp.int32, (1, dg, dh), 1)
    diag_code = jnp.where(lane3 <= row3, lane3, -1)

    for hd in range(HGRN_HEADS):
        sl = slice(hd * dh, (hd + 1) * dh)
        q = q_ref[:, sl].astype(F32)
        v = i_ref[:, sl]
        b = b_all[:, sl]
        lk = lk_ref[:, sl]
        b_last = b[c - 1:c, :]

        st = st_ref[hd]
        qq0 = (q * jnp.exp2(b)).astype(BF16)
        o = lax.dot_general(qq0, st.astype(BF16), nt_dims, preferred_element_type=F32)
        kk_end = jnp.exp2(lk + (b_last - b)).astype(BF16)
        st_ref[hd] = st * jnp.exp2(b_last) + lax.dot_general(v, kk_end, tn_dims, preferred_element_type=F32)

        att_d = None
        for idx, m in reversed(list(enumerate(_hgrn_inner_levels()))):
            shp = (c // (2 * m), 2 * m, dh)
            b3 = b.reshape(shp)
            mid = b3[:, m - 1:m, :]
            zero = jnp.zeros((shp[0], m, dh), F32)
            q_hi = q.reshape(shp)[:, m:, :] * jnp.exp2(b3[:, m:, :] - mid)
            k_lo = jnp.exp2(lk.reshape(shp)[:, :m, :] + (mid - b3[:, :m, :]))
            qm = jnp.concatenate([zero, q_hi], axis=1).reshape(c, dh).astype(BF16)
            km = jnp.concatenate([k_lo, zero], axis=1).reshape(c, dh).astype(BF16)
            a = lax.dot_general(qm, km, nt_dims, preferred_element_type=F32)
            a_d = jnp.concatenate([a[:hb, :hb], a[hb:, hb:]], axis=0)
            att_d = a_d if att_d is None else jnp.where(code == idx, a_d, att_d)

        mid = b[hb - 1:hb, :]
        q_top = (q[hb:] * jnp.exp2(b[hb:] - mid)).astype(BF16)
        k_top = jnp.exp2(lk[:hb] + (mid - b[:hb])).astype(BF16)
        a_top = lax.dot_general(q_top, k_top, nt_dims, preferred_element_type=F32)

        q3 = q.reshape(nb, dg, dh)
        b3 = b.reshape(nb, dg, dh)
        bk3 = (b - lk).reshape(nb, dg, dh)
        compact = jnp.zeros((nb, dg, dh), F32)
        for s_loc in range(dg):
            z = q3 * jnp.exp2(b3 - bk3[:, s_loc:s_loc + 1, :])
            col = jnp.sum(z, axis=-1, keepdims=True)
            compact = jnp.where(diag_code == s_loc, col, compact)
        compact = compact.reshape(c, dh)
        a_diag = jnp.dot(compact.astype(BF16), expand, preferred_element_type=F32)
        att_d = jnp.where(code == -1, a_diag, att_d)

        att = jnp.concatenate([
            jnp.concatenate([att_d[:hb], jnp.zeros((hb, hb), F32)], axis=1),
            jnp.concatenate([a_top, att_d[hb:]], axis=1)], axis=0)
        o = o + jnp.dot(att.astype(BF16), v, preferred_element_type=F32)
        ms = jnp.mean(o * o, axis=-1, keepdims=True)
        y = (o * lax.rsqrt(ms + EPS)) * nw
        o_ref[:, sl] = (y * gate_ref[:, sl].astype(F32)).astype(BF16)


def _hgrn(norm_w, hq, g2, lk, hi, gate):
    bsz, t, w = hq.shape
    c = HGRN_CHUNK
    assert c == 2 * HGRN_HALF and HGRN_HALF == LANES
    tri = jnp.asarray(np.tril(np.ones((c, c))), BF16)
    code = jnp.asarray(_hgrn_code_table())
    expand_np = (np.arange(HGRN_DIM)[:, None] == (np.arange(HGRN_HALF)[None, :] % HGRN_DIAG)).astype(np.float32)
    expand = jnp.asarray(expand_np, BF16)
    tok = pl.BlockSpec((None, c, w), lambda b, i: (b, i, 0))
    return pl.pallas_call(
        _hgrn_kernel,
        grid=(bsz, t // c),
        in_specs=[
            _resident((1, HGRN_DIM)),
            tok, tok, tok, tok, tok,
            _resident((c, c)),
            _resident((c, HGRN_HALF)),
            _resident((HGRN_DIM, HGRN_HALF)),
        ],
        out_specs=tok,
        out_shape=jax.ShapeDtypeStruct((bsz, t, w), BF16),
        scratch_shapes=[pltpu.VMEM((HGRN_HEADS, HGRN_DIM, HGRN_DIM), F32)],
        compiler_params=_params(("arbitrary", "arbitrary")),
        name="hgrn_scan",
    )(norm_w.reshape(1, HGRN_DIM), hq, g2, lk, hi, gate, tri, code, expand)


def _attn_kernel(sink_ref, q_ref, kc_ref, kp_ref, vc_ref, vp_ref, o_ref):
    blk = WINDOW
    n = pl.program_id(1)
    nt_dims = (((1,), (1,)), ((), ()))
    k_all = jnp.concatenate([kp_ref[...], kc_ref[...]], axis=0).astype(F32)
    v_all = jnp.concatenate([vp_ref[...], vc_ref[...]], axis=0).astype(F32)
    k_sw = pltpu.roll(k_all, HEAD_DIM, 1)
    v_sw = pltpu.roll(v_all, HEAD_DIM, 1)
    lo = lax.broadcasted_iota(jnp.int32, k_all.shape, 1) < HEAD_DIM

    qi = lax.broadcasted_iota(jnp.int32, (blk, 2 * blk), 0)
    kj = lax.broadcasted_iota(jnp.int32, (blk, 2 * blk), 1) - blk
    delta = qi - kj
    neg = jnp.full((blk, 2 * blk), -jnp.inf, F32)
    bias = jnp.where(delta >= 0, jnp.where(delta < WINDOW, 0.0, neg), neg)
    bias_first = jnp.where(n > 0, bias, jnp.where(kj >= 0, bias, neg))
    top = lax.broadcasted_iota(jnp.int32, (2 * blk, 1), 0) < blk

    for kh in range(KV_HEADS):
        own_k, other_k = (k_all, k_sw) if kh == 0 else (k_sw, k_all)
        own_v, other_v = (v_all, v_sw) if kh == 0 else (v_sw, v_all)
        kv_par = ((jnp.where(lo, own_k, 0.0).astype(BF16), jnp.where(lo, own_v, 0.0).astype(BF16)),
                  (jnp.where(lo, 0.0, other_k).astype(BF16), jnp.where(lo, 0.0, other_v).astype(BF16)))
        base = kh * GROUP * HEAD_DIM
        for j in range(ATTN_QB):
            rows = slice(j * blk, (j + 1) * blk)
            keys = slice(j * blk, (j + 2) * blk)
            bias_j = bias_first if j == 0 else bias
            bias2 = jnp.concatenate([bias_j, bias_j], axis=0)
            qs = jnp.concatenate([q_ref[rows, base:base + LANES], q_ref[rows, base + LANES:base + 2 * LANES]], axis=0)
            o = None
            for par, (kk, vv) in enumerate(kv_par):
                s = lax.dot_general(qs, kk[keys], nt_dims, preferred_element_type=F32) + bias2
                sink = jnp.where(top, sink_ref[kh * GROUP + par], sink_ref[kh * GROUP + 2 + par]) * LOG2E
                mx = jnp.maximum(jnp.max(s, axis=-1, keepdims=True), sink)
                p = jnp.exp2(s - mx)
                den = jnp.sum(p, axis=-1, keepdims=True) + jnp.exp2(sink - mx)
                part = jnp.dot(p.astype(BF16), vv[keys], preferred_element_type=F32) / den
                o = part if o is None else o + part
            o_ref[rows, base:base + LANES] = o[:blk].astype(BF16)
            o_ref[rows, base + LANES:base + 2 * LANES] = o[blk:].astype(BF16)


def _attention(sinks_l, aq, ak, av):
    bsz, t, _ = aq.shape
    blk = WINDOW
    qb = ATTN_QB
    cur = lambda w: pl.BlockSpec((None, qb * blk, w), lambda b, i: (b, i, 0))
    prev = lambda w: pl.BlockSpec((None, blk, w), lambda b, i: (b, jnp.maximum(i * qb - 1, 0), 0))
    return pl.pallas_call(
        _attn_kernel,
        grid=(bsz, t // (qb * blk)),
        in_specs=[
            pl.BlockSpec(memory_space=pltpu.SMEM),
            cur(ATTN_WIDTH), cur(KV_WIDTH), prev(KV_WIDTH), cur(KV_WIDTH), prev(KV_WIDTH),
        ],
        out_specs=cur(ATTN_WIDTH),
        out_shape=jax.ShapeDtypeStruct((bsz, t, ATTN_WIDTH), BF16),
        compiler_params=_params(("arbitrary", "arbitrary")),
        name="swa_attn",
    )(sinks_l, aq, ak, ak, av, av)


def _ffn_kernel(*refs, rows, with_mix):
    if with_mix:
        (x_ref, mod_ref, nw_ref, wg_ref, wu_ref, wd_ref, oh_ref, oa_ref, wo_ref, o_ref, a_scr) = refs
    else:
        (x_ref, mod_ref, nw_ref, wg_ref, wu_ref, wd_ref, o_ref, a_scr) = refs
    m = mod_ref[...]
    x = x_ref[...]
    if with_mix:
        hw = oh_ref.shape[-1]
        mix = (jnp.dot(oh_ref[...], wo_ref[:hw, :], preferred_element_type=F32)
               + jnp.dot(oa_ref[...], wo_ref[hw:, :], preferred_element_type=F32))
        x = x + m[5:6] * mix
    sh, sc, gate = (m[r:r + 1] for r in rows)
    h = _rms_mod(x, nw_ref[...], sc, sh).astype(BF16)
    lo = 0
    for width in FFN_CHUNKS:
        gt = jnp.dot(h, wg_ref[:, lo:lo + width], preferred_element_type=F32)
        ut = jnp.dot(h, wu_ref[:, lo:lo + width], preferred_element_type=F32)
        a_scr[:, lo:lo + width] = (_silu(gt) * ut).astype(BF16)
        lo += width
    y = jnp.dot(a_scr[...], wd_ref[...], preferred_element_type=F32)
    o_ref[...] = x + (0.5 * gate) * y


def _ffn(x, mod, layer, nw, wg, wu, wd, rows, mix=None):
    bsz, t, d = x.shape
    tm = FFN_TM
    dff = wg.shape[-1]
    assert sum(FFN_CHUNKS) == dff
    tok = lambda w: pl.BlockSpec((None, tm, w), lambda b, i: (b, i, 0))
    in_specs = [
        tok(d),
        _mod_spec(mod, layer),
        _resident((1, d)),
        _layer_resident(wg, layer),
        _layer_resident(wu, layer),
        _layer_resident(wd, layer),
    ]
    args = [x, mod, nw.reshape(1, d), wg, wu, wd]
    if mix is not None:
        o_h, o_a, w_out = mix
        in_specs += [tok(o_h.shape[-1]), tok(o_a.shape[-1]), _layer_resident(w_out, layer)]
        args += [o_h, o_a, w_out]
    return pl.pallas_call(
        functools.partial(_ffn_kernel, rows=rows, with_mix=mix is not None),
        grid=(bsz, t // tm),
        in_specs=in_specs,
        out_specs=tok(d),
        out_shape=jax.ShapeDtypeStruct((bsz, t, d), F32),
        scratch_shapes=[pltpu.VMEM((tm, dff), BF16)],
        compiler_params=_params(("arbitrary", "arbitrary")),
        name="ffn_mix" if mix is not None else "ffn",
    )(*args)


def kernel(x, c, positions, ada_w, ada_b, norm_ffn1, ffn1_w_gate, ffn1_w_up, ffn1_w_down, norm_mix, w_in,
           lb_logits, hgrn_norm, q_norm, k_norm, sinks, w_out, norm_ffn2, ffn2_w_gate, ffn2_w_up, ffn2_w_down):
    depth = ada_w.shape[0]
    mod = _modulation(c, ada_w, ada_b)
    cos_t, sin_t = _rope_tables(positions)
    bf = lambda w: w.astype(BF16)
    wg1, wu1, wd1 = bf(ffn1_w_gate), bf(ffn1_w_up), bf(ffn1_w_down)
    wg2, wu2, wd2 = bf(ffn2_w_gate), bf(ffn2_w_up), bf(ffn2_w_down)
    w_in_b, w_out_b = bf(w_in), bf(w_out)
    for l in range(depth):
        x = _ffn(x, mod, l, norm_ffn1[l], wg1, wu1, wd1, rows=(0, 1, 2))
        hq, g2, lk, hi, gate, aq, ak, av = _project(x, mod, norm_mix[l], w_in_b, q_norm[l], k_norm[l], cos_t, sin_t,
                                                    lb_logits, l)
        o_h = _hgrn(hgrn_norm[l], hq, g2, lk, hi, gate)
        o_a = _attention(sinks[l], aq, ak, av)
        x = _ffn(x, mod, l, norm_ffn2[l], wg2, wu2, wd2, rows=(6, 7, 8), mix=(o_h, o_a, w_out_b))
    return x
```

```python
import functools

import numpy as np
import jax
import jax.numpy as jnp
from jax import lax
from jax.experimental import pallas as pl
from jax.experimental.pallas import tpu as pltpu

F32 = jnp.float32
BF16 = jnp.bfloat16

HGRN_HEADS = 4
HGRN_DIM = 128
HGRN_WIDTH = HGRN_HEADS * HGRN_DIM
ATTN_HEADS = 8
KV_HEADS = 2
HEAD_DIM = 64
ATTN_WIDTH = ATTN_HEADS * HEAD_DIM
KV_WIDTH = KV_HEADS * HEAD_DIM
GROUP = ATTN_HEADS // KV_HEADS
WINDOW = 128
ROPE_THETA = 500000.0
ROPE_DIM = HEAD_DIM // 4
N_MOD = 9
EPS = 1e-6

LANES = 128
VMEM_LIMIT_BYTES = 56 * 1024 * 1024

MOD_TN = 2304
ROPE_TM = 1024
PROJ_TM = 512
FFN_TM = 1024
FFN_SUB = 512
FFN_CHUNKS = (1536, 1280)
HGRN_CHUNK = 256
HGRN_DIAG = 8
ATTN_QB = 4
HGRN_HALF = 128
LOG2E = 1.4426950408889634


def _params(sem):
    return pltpu.CompilerParams(dimension_semantics=sem, vmem_limit_bytes=VMEM_LIMIT_BYTES)


def _resident(shape):
    nd = len(shape)
    return pl.BlockSpec(shape, lambda *_: (0,) * nd, pipeline_mode=pl.Buffered(1))


def _layer_resident(arr, l):
    nd = arr.ndim - 1
    return pl.BlockSpec((None,) + arr.shape[1:], lambda *_: (l,) + (0,) * nd, pipeline_mode=pl.Buffered(1))


def _mod_spec(mod, l):
    return pl.BlockSpec((None, None) + mod.shape[2:], lambda b, i: (l, b, 0, 0))


def _rms_mod(x, nw, sc, sh):
    ms = jnp.mean(x * x, axis=-1, keepdims=True)
    return (x * lax.rsqrt(ms + EPS)) * (nw * (1.0 + sc)) + sh


def _silu(x):
    return x * jax.nn.sigmoid(x)


def _mod_kernel(ct_ref, w_ref, b_ref, o_ref):
    ca = _silu(ct_ref[...])
    w = w_ref[...]
    rows = [jnp.sum(w * ca[:, b:b + 1], axis=0, keepdims=True) for b in range(ca.shape[1])]
    o_ref[...] = jnp.concatenate(rows, axis=0) + b_ref[...]


def _modulation(c, ada_w, ada_b):
    depth, d, n = ada_w.shape
    bsz = c.shape[0]
    out = pl.pallas_call(
        _mod_kernel,
        grid=(depth, n // MOD_TN),
        in_specs=[
            pl.BlockSpec((d, bsz), lambda l, j: (0, 0)),
            pl.BlockSpec((None, d, MOD_TN), lambda l, j: (l, 0, j)),
            pl.BlockSpec((None, 1, MOD_TN), lambda l, j: (l, 0, j)),
        ],
        out_specs=pl.BlockSpec((None, bsz, MOD_TN), lambda l, j: (l, 0, j)),
        out_shape=jax.ShapeDtypeStruct((depth, bsz, n), F32),
        compiler_params=_params(("arbitrary", "arbitrary")),
        name="adaln_mod",
    )(c.T, ada_w, ada_b.reshape(depth, 1, n))
    return out.reshape(depth, bsz, N_MOD, d)


def _rope_kernel(pos_ref, invf_ref, ec_ref, es_ref, base_ref, cos_ref, sin_ref):
    tn_dims = (((0,), (0,)), ((), ()))
    ang = invf_ref[...] * pos_ref[...].astype(F32)

    def expand(vals, e_ref):
        pad = jnp.zeros(vals.shape, BF16)
        stacked = jnp.concatenate(list(_split3(vals)) + [pad], axis=0)
        return lax.dot_general(stacked, e_ref[...], tn_dims, preferred_element_type=F32)

    cos_ref[...] = expand(jnp.cos(ang), ec_ref) + base_ref[...]
    sin_ref[...] = expand(jnp.sin(ang), es_ref)


def _rope_tables(positions):
    bsz, t = positions.shape
    nf = ROPE_DIM // 2
    inv_freq = ROPE_THETA ** (-jnp.arange(0, ROPE_DIM, 2, dtype=F32) / ROPE_DIM)
    lane = np.arange(2 * HEAD_DIM) % HEAD_DIM
    rot = lane < ROPE_DIM
    pick = (np.arange(nf)[:, None] == (lane % nf)[None, :]) & rot[None, :]
    e_cos = np.tile(pick.astype(np.float32), (4, 1))
    e_cos[3 * nf:] = 0.0
    e_sin = e_cos * np.where(lane < nf, -1.0, 1.0)[None, :]
    base = (~rot).astype(np.float32).reshape(1, 2 * HEAD_DIM)
    shp = jax.ShapeDtypeStruct((bsz, t, 2 * HEAD_DIM), F32)
    const = lambda shape: pl.BlockSpec(shape, lambda b, i: (0, 0))
    return pl.pallas_call(
        _rope_kernel,
        grid=(bsz, t // ROPE_TM),
        in_specs=[
            pl.BlockSpec((None, 1, ROPE_TM), lambda b, i: (b, 0, i)),
            const((nf, 1)),
            const((4 * nf, 2 * HEAD_DIM)),
            const((4 * nf, 2 * HEAD_DIM)),
            const((1, 2 * HEAD_DIM)),
        ],
        out_specs=[pl.BlockSpec((None, ROPE_TM, 2 * HEAD_DIM), lambda b, i: (b, i, 0))] * 2,
        out_shape=[shp, shp],
        compiler_params=_params(("arbitrary", "arbitrary")),
        name="rope_tables",
    )(positions.reshape(bsz, 1, t), inv_freq.reshape(nf, 1), jnp.asarray(e_cos, BF16), jnp.asarray(e_sin, BF16),
      jnp.asarray(base))


def _head_norm_rope(t, seg, nw, cos, sin, swap_lo):
    ms = jnp.dot((t * t).astype(BF16), seg, preferred_element_type=F32)
    y = (t * lax.rsqrt(ms + EPS)) * nw
    swapped = jnp.where(swap_lo, pltpu.roll(y, LANES - ROPE_DIM // 2, 1), pltpu.roll(y, ROPE_DIM // 2, 1))
    return y * cos + swapped * sin


def _lower_bound_logs(lbl, layer):
    e = jnp.exp(lbl - jnp.max(lbl, axis=0, keepdims=True))
    sm = e / jnp.sum(e, axis=0, keepdims=True)
    cum0 = sm[0:1]
    cum = cum0
    for j in range(1, layer + 1):
        cum = cum + sm[j:j + 1]
    lb = cum - cum0
    return jnp.log(lb), jnp.log1p(-lb)


def _proj_kernel(x_ref, mod_ref, nw_ref, w_ref, qn_ref, kn_ref, cos_ref, sin_ref, seg_ref, lbl_ref,
                 hq_ref, g2_ref, lk_ref, hi_ref, hg_ref, aq_ref, ak_ref, av_ref, *, layer):
    m = mod_ref[...]
    h = _rms_mod(x_ref[...], nw_ref[...], m[4:5], m[3:4]).astype(BF16)

    def seg_dot(lo, width):
        return jnp.dot(h, w_ref[:, lo:lo + width], preferred_element_type=F32)

    hq_ref[...] = (seg_dot(0, HGRN_WIDTH) * (HGRN_DIM ** -0.5)).astype(BF16)

    fl2 = seg_dot(HGRN_WIDTH, HGRN_WIDTH) * LOG2E
    sp2 = jnp.log2(1.0 + jnp.exp2(-jnp.abs(fl2)))
    ls2 = jnp.minimum(fl2, 0.0) - sp2
    lsn2 = -(jnp.maximum(fl2, 0.0) + sp2)
    if layer == 0:
        g2_ref[...] = ls2
        lk_ref[...] = lsn2
    else:
        log_lb, l1m = _lower_bound_logs(lbl_ref[...], layer)
        log_lb2 = log_lb * LOG2E
        l1m2 = l1m * LOG2E
        cc2 = l1m2 + ls2
        lk_ref[...] = l1m2 + lsn2
        dlt = log_lb2 - cc2
        g2_ref[...] = jnp.maximum(log_lb2, cc2) + jnp.log2(1.0 + jnp.exp2(-jnp.abs(dlt)))

    hi_ref[...] = seg_dot(2 * HGRN_WIDTH, HGRN_WIDTH).astype(BF16)
    hg_ref[...] = _silu(seg_dot(3 * HGRN_WIDTH, HGRN_WIDTH)).astype(BF16)
    base = 4 * HGRN_WIDTH
    aq = seg_dot(base, ATTN_WIDTH)
    ak = seg_dot(base + ATTN_WIDTH, KV_WIDTH)
    av_ref[...] = seg_dot(base + ATTN_WIDTH + KV_WIDTH, KV_WIDTH).astype(BF16)

    seg = seg_ref[...]
    cos = cos_ref[...]
    sin = sin_ref[...]
    lane = lax.broadcasted_iota(jnp.int32, cos.shape, 1)
    swap_lo = (lane & (HEAD_DIM - 1)) < ROPE_DIM // 2
    qn = qn_ref[...]
    for j in range(ATTN_WIDTH // LANES):
        sl = slice(j * LANES, (j + 1) * LANES)
        aq_ref[:, sl] = _head_norm_rope(aq[:, sl], seg, qn, cos, sin, swap_lo).astype(BF16)
    kn = kn_ref[...]
    for j in range(KV_WIDTH // LANES):
        sl = slice(j * LANES, (j + 1) * LANES)
        ak_ref[:, sl] = _head_norm_rope(ak[:, sl], seg, kn, cos, sin, swap_lo).astype(BF16)


def _project(x, mod, nw, w_in, q_norm, k_norm, cos_t, sin_t, lb_logits, layer):
    bsz, t, d = x.shape
    tm = PROJ_TM
    seg_np = np.kron(np.eye(LANES // HEAD_DIM), np.full((HEAD_DIM, HEAD_DIM), 1.0 / HEAD_DIM))
    seg = jnp.asarray(seg_np, BF16)
    qn = jnp.tile(q_norm * (LOG2E * HEAD_DIM ** -0.5), LANES // HEAD_DIM).reshape(1, LANES)
    kn = jnp.tile(k_norm, LANES // HEAD_DIM).reshape(1, LANES)
    tok = lambda w: pl.BlockSpec((None, tm, w), lambda b, i: (b, i, 0))
    sds = lambda w, dt: jax.ShapeDtypeStruct((bsz, t, w), dt)
    return pl.pallas_call(
        functools.partial(_proj_kernel, layer=layer),
        grid=(bsz, t // tm),
        in_specs=[
            tok(d),
            _mod_spec(mod, layer),
            _resident((1, d)),
            _layer_resident(w_in, layer),
            _resident((1, LANES)),
            _resident((1, LANES)),
            tok(LANES),
            tok(LANES),
            _resident((LANES, LANES)),
            _resident(lb_logits.shape),
        ],
        out_specs=[tok(HGRN_WIDTH)] * 5 + [tok(ATTN_WIDTH), tok(KV_WIDTH), tok(KV_WIDTH)],
        out_shape=[sds(HGRN_WIDTH, BF16), sds(HGRN_WIDTH, F32), sds(HGRN_WIDTH, F32), sds(HGRN_WIDTH, BF16),
                   sds(HGRN_WIDTH, BF16), sds(ATTN_WIDTH, BF16), sds(KV_WIDTH, BF16), sds(KV_WIDTH, BF16)],
        compiler_params=_params(("arbitrary", "arbitrary")),
        name="in_proj",
    )(x, mod, nw.reshape(1, d), w_in, qn, kn, cos_t, sin_t, seg, lb_logits)


def _hgrn_inner_levels():
    lv = []
    m = HGRN_DIAG
    while m < HGRN_HALF:
        lv.append(m)
        m *= 2
    return tuple(lv)


def _hgrn_code_table():
    n = HGRN_HALF
    t = np.arange(n)[:, None]
    s = np.arange(n)[None, :]
    x = t ^ s
    code = np.full((n, n), -2, np.int32)
    for idx, m in enumerate(_hgrn_inner_levels()):
        code[(x >= m) & (x < 2 * m) & (t > s)] = idx
    code[(t // HGRN_DIAG == s // HGRN_DIAG) & (t >= s)] = -1
    return np.tile(code, (HGRN_CHUNK // HGRN_HALF, 1))


def _split3(g):
    hi = g.astype(BF16)
    r1 = g - hi.astype(F32)
    mid = r1.astype(BF16)
    lo = (r1 - mid.astype(F32)).astype(BF16)
    return hi, mid, lo


def _hgrn_kernel(nw_ref, q_ref, g2_ref, lk_ref, i_ref, gate_ref, tri_ref, code_ref, exp_ref, o_ref, st_ref):
    c = HGRN_CHUNK
    dg = HGRN_DIAG
    hb = HGRN_HALF
    dh = HGRN_DIM
    nt_dims = (((1,), (1,)), ((), ()))
    tn_dims = (((0,), (0,)), ((), ()))

    @pl.when(pl.program_id(1) == 0)
    def _():
        st_ref[...] = jnp.zeros(st_ref.shape, F32)

    tri = tri_ref[...]
    b_all = sum(jnp.dot(tri, p, preferred_element_type=F32) for p in _split3(g2_ref[...]))

    code = code_ref[...]
    expand = exp_ref[...]
    nw = nw_ref[...]
    nb = c // dg
    lane3 = lax.broadcasted_iota(jnp.int32, (1, dg, dh), 2)
    row3 = lax.broadcasted_iota(jnp.int32, (1, dg, dh), 1)
    diag_code = jnp.where(lane3 <= row3, lane3, -1)

    for hd in range(HGRN_HEADS):
        sl = slice(hd * dh, (hd + 1) * dh)
        q = q_ref[:, sl].astype(F32)
        v = i_ref[:, sl]
        b = b_all[:, sl]
        lk = lk_ref[:, sl]
        b_last = b[c - 1:c, :]

        st = st_ref[hd]
        qq0 = (q * jnp.exp2(b)).astype(BF16)
        o = lax.dot_general(qq0, st.astype(BF16), nt_dims, preferred_element_type=F32)
        kk_end = jnp.exp2(lk + (b_last - b)).astype(BF16)
        st_ref[hd] = st * jnp.exp2(b_last) + lax.dot_general(v, kk_end, tn_dims, preferred_element_type=F32)

        att_d = None
        for idx, m in reversed(list(enumerate(_hgrn_inner_levels()))):
            shp = (c // (2 * m), 2 * m, dh)
            b3 = b.reshape(shp)
            mid = b3[:, m - 1:m, :]
            zero = jnp.zeros((shp[0], m, dh), F32)
            q_hi = q.reshape(shp)[:, m:, :] * jnp.exp2(b3[:, m:, :] - mid)
            k_lo = jnp.exp2(lk.reshape(shp)[:, :m, :] + (mid - b3[:, :m, :]))
            qm = jnp.concatenate([zero, q_hi], axis=1).reshape(c, dh).astype(BF16)
            km = jnp.concatenate([k_lo, zero], axis=1).reshape(c, dh).astype(BF16)
            a = lax.dot_general(qm, km, nt_dims, preferred_element_type=F32)
            a_d = jnp.concatenate([a[:hb, :hb], a[hb:, hb:]], axis=0)
            att_d = a_d if att_d is None else jnp.where(code == idx, a_d, att_d)

        mid = b[hb - 1:hb, :]
        q_top = (q[hb:] * jnp.exp2(b[hb:] - mid)).astype(BF16)
        k_top = jnp.exp2(lk[:hb] + (mid - b[:hb])).astype(BF16)
        a_top = lax.dot_general(q_top, k_top, nt_dims, preferred_element_type=F32)

        q3 = q.reshape(nb, dg, dh)
        b3 = b.reshape(nb, dg, dh)
        bk3 = (b - lk).reshape(nb, dg, dh)
        compact = jnp.zeros((nb, dg, dh), F32)
        for s_loc in range(dg):
            z = q3 * jnp.exp2(b3 - bk3[:, s_loc:s_loc + 1, :])
            col = jnp.sum(z, axis=-1, keepdims=True)
            compact = jnp.where(diag_code == s_loc, col, compact)
        compact = compact.reshape(c, dh)
        a_diag = jnp.dot(compact.astype(BF16), expand, preferred_element_type=F32)
        att_d = jnp.where(code == -1, a_diag, att_d)

        att = jnp.concatenate([
            jnp.concatenate([att_d[:hb], jnp.zeros((hb, hb), F32)], axis=1),
            jnp.concatenate([a_top, att_d[hb:]], axis=1)], axis=0)
        o = o + jnp.dot(att.astype(BF16), v, preferred_element_type=F32)
        ms = jnp.mean(o * o, axis=-1, keepdims=True)
        y = (o * lax.rsqrt(ms + EPS)) * nw
        o_ref[:, sl] = (y * gate_ref[:, sl].astype(F32)).astype(BF16)


def _hgrn(norm_w, hq, g2, lk, hi, gate):
    bsz, t, w = hq.shape
    c = HGRN_CHUNK
    assert c == 2 * HGRN_HALF and HGRN_HALF == LANES
    tri = jnp.asarray(np.tril(np.ones((c, c))), BF16)
    code = jnp.asarray(_hgrn_code_table())
    expand_np = (np.arange(HGRN_DIM)[:, None] == (np.arange(HGRN_HALF)[None, :] % HGRN_DIAG)).astype(np.float32)
    expand = jnp.asarray(expand_np, BF16)
    tok = pl.BlockSpec((None, c, w), lambda b, i: (b, i, 0))
    return pl.pallas_call(
        _hgrn_kernel,
        grid=(bsz, t // c),
        in_specs=[
            _resident((1, HGRN_DIM)),
            tok, tok, tok, tok, tok,
            _resident((c, c)),
            _resident((c, HGRN_HALF)),
            _resident((HGRN_DIM, HGRN_HALF)),
        ],
        out_specs=tok,
        out_shape=jax.ShapeDtypeStruct((bsz, t, w), BF16),
        scratch_shapes=[pltpu.VMEM((HGRN_HEADS, HGRN_DIM, HGRN_DIM), F32)],
        compiler_params=_params(("arbitrary", "arbitrary")),
        name="hgrn_scan",
    )(norm_w.reshape(1, HGRN_DIM), hq, g2, lk, hi, gate, tri, code, expand)


def _attn_kernel(sink_ref, q_ref, kc_ref, kp_ref, vc_ref, vp_ref, o_ref):
    blk = WINDOW
    n = pl.program_id(1)
    nt_dims = (((1,), (1,)), ((), ()))
    k_all = jnp.concatenate([kp_ref[...], kc_ref[...]], axis=0).astype(F32)
    v_all = jnp.concatenate([vp_ref[...], vc_ref[...]], axis=0).astype(F32)
    k_sw = pltpu.roll(k_all, HEAD_DIM, 1)
    v_sw = pltpu.roll(v_all, HEAD_DIM, 1)
    lo = lax.broadcasted_iota(jnp.int32, k_all.shape, 1) < HEAD_DIM

    qi = lax.broadcasted_iota(jnp.int32, (blk, 2 * blk), 0)
    kj = lax.broadcasted_iota(jnp.int32, (blk, 2 * blk), 1) - blk
    delta = qi - kj
    neg = jnp.full((blk, 2 * blk), -jnp.inf, F32)
    bias = jnp.where(delta >= 0, jnp.where(delta < WINDOW, 0.0, neg), neg)
    bias_first = jnp.where(n > 0, bias, jnp.where(kj >= 0, bias, neg))
    top = lax.broadcasted_iota(jnp.int32, (2 * blk, 1), 0) < blk

    for kh in range(KV_HEADS):
        own_k, other_k = (k_all, k_sw) if kh == 0 else (k_sw, k_all)
        own_v, other_v = (v_all, v_sw) if kh == 0 else (v_sw, v_all)
        kv_par = ((jnp.where(lo, own_k, 0.0).astype(BF16), jnp.where(lo, own_v, 0.0).astype(BF16)),
                  (jnp.where(lo, 0.0, other_k).astype(BF16), jnp.where(lo, 0.0, other_v).astype(BF16)))
        base = kh * GROUP * HEAD_DIM
        for j in range(ATTN_QB):
            rows = slice(j * blk, (j + 1) * blk)
            keys = slice(j * blk, (j + 2) * blk)
            bias_j = bias_first if j == 0 else bias
            bias2 = jnp.concatenate([bias_j, bias_j], axis=0)
            qs = jnp.concatenate([q_ref[rows, base:base + LANES], q_ref[rows, base + LANES:base + 2 * LANES]], axis=0)
            o = None
            for par, (kk, vv) in enumerate(kv_par):
                s = lax.dot_general(qs, kk[keys], nt_dims, preferred_element_type=F32) + bias2
                sink = jnp.where(top, sink_ref[kh * GROUP + par], sink_ref[kh * GROUP + 2 + par]) * LOG2E
                mx = jnp.maximum(jnp.max(s, axis=-1, keepdims=True), sink)
                p = jnp.exp2(s - mx)
                den = jnp.sum(p, axis=-1, keepdims=True) + jnp.exp2(sink - mx)
                part = jnp.dot(p.astype(BF16), vv[keys], preferred_element_type=F32) / den
                o = part if o is None else o + part
            o_ref[rows, base:base + LANES] = o[:blk].astype(BF16)
            o_ref[rows, base + LANES:base + 2 * LANES] = o[blk:].astype(BF16)


def _attention(sinks_l, aq, ak, av):
    bsz, t, _ = aq.shape
    blk = WINDOW
    qb = ATTN_QB
    cur = lambda w: pl.BlockSpec((None, qb * blk, w), lambda b, i: (b, i, 0))
    prev = lambda w: pl.BlockSpec((None, blk, w), lambda b, i: (b, jnp.maximum(i * qb - 1, 0), 0))
    return pl.pallas_call(
        _attn_kernel,
        grid=(bsz, t // (qb * blk)),
        in_specs=[
            pl.BlockSpec(memory_space=pltpu.SMEM),
            cur(ATTN_WIDTH), cur(KV_WIDTH), prev(KV_WIDTH), cur(KV_WIDTH), prev(KV_WIDTH),
        ],
        out_specs=cur(ATTN_WIDTH),
        out_shape=jax.ShapeDtypeStruct((bsz, t, ATTN_WIDTH), BF16),
        compiler_params=_params(("arbitrary", "arbitrary")),
        name="swa_attn",
    )(sinks_l, aq, ak, ak, av, av)


def _ffn_kernel(*refs, rows, with_mix):
    if with_mix:
        (x_ref, mod_ref, nw_ref, wg_ref, wu_ref, wd_ref, oh_ref, oa_ref, wo_ref, o_ref, a_scr) = refs
    else:
        (x_ref, mod_ref, nw_ref, wg_ref, wu_ref, wd_ref, o_ref, a_scr) = refs
    m = mod_ref[...]
    sh, sc, gate = (m[r:r + 1] for r in rows)
    for sub in range(FFN_TM // FFN_SUB):
        rs = slice(sub * FFN_SUB, (sub + 1) * FFN_SUB)
        x = x_ref[rs, :]
        if with_mix:
            hw = oh_ref.shape[-1]
            mix = (jnp.dot(oh_ref[rs, :], wo_ref[:hw, :], preferred_element_type=F32)
                   + jnp.dot(oa_ref[rs, :], wo_ref[hw:, :], preferred_element_type=F32))
            x = x + m[5:6] * mix
        h = _rms_mod(x, nw_ref[...], sc, sh).astype(BF16)
        lo = 0
        for width in FFN_CHUNKS:
            gt = jnp.dot(h, wg_ref[:, lo:lo + width], preferred_element_type=F32)
            ut = jnp.dot(h, wu_ref[:, lo:lo + width], preferred_element_type=F32)
            a_scr[rs, lo:lo + width] = (_silu(gt) * ut).astype(BF16)
            lo += width
        y = jnp.dot(a_scr[rs, :], wd_ref[...], preferred_element_type=F32)
        o_ref[rs, :] = x + (0.5 * gate) * y


def _ffn(x, mod, layer, nw, wg, wu, wd, rows, mix=None):
    bsz, t, d = x.shape
    tm = FFN_TM
    dff = wg.shape[-1]
    assert sum(FFN_CHUNKS) == dff
    tok = lambda w: pl.BlockSpec((None, tm, w), lambda b, i: (b, i, 0))
    in_specs = [
        tok(d),
        _mod_spec(mod, layer),
        _resident((1, d)),
        _layer_resident(wg, layer),
        _layer_resident(wu, layer),
        _layer_resident(wd, layer),
    ]
    args = [x, mod, nw.reshape(1, d), wg, wu, wd]
    if mix is not None:
        o_h, o_a, w_out = mix
        in_specs += [tok(o_h.shape[-1]), tok(o_a.shape[-1]), _layer_resident(w_out, layer)]
        args += [o_h, o_a, w_out]
    return pl.pallas_call(
        functools.partial(_ffn_kernel, rows=rows, with_mix=mix is not None),
        grid=(bsz, t // tm),
        in_specs=in_specs,
        out_specs=tok(d),
        out_shape=jax.ShapeDtypeStruct((bsz, t, d), F32),
        scratch_shapes=[pltpu.VMEM((tm, dff), BF16)],
        compiler_params=_params(("arbitrary", "arbitrary")),
        name="ffn_mix" if mix is not None else "ffn",
    )(*args)


def kernel(x, c, positions, ada_w, ada_b, norm_ffn1, ffn1_w_gate, ffn1_w_up, ffn1_w_down, norm_mix, w_in,
           lb_logits, hgrn_norm, q_norm, k_norm, sinks, w_out, norm_ffn2, ffn2_w_gate, ffn2_w_up, ffn2_w_down):
    depth = ada_w.shape[0]
    mod = _modulation(c, ada_w, ada_b)
    cos_t, sin_t = _rope_tables(positions)
    bf = lambda w: w.astype(BF16)
    wg1, wu1, wd1 = bf(ffn1_w_gate), bf(ffn1_w_up), bf(ffn1_w_down)
    wg2, wu2, wd2 = bf(ffn2_w_gate), bf(ffn2_w_up), bf(ffn2_w_down)
    w_in_b, w_out_b = bf(w_in), bf(w_out)
    for l in range(depth):
        x = _ffn(x, mod, l, norm_ffn1[l], wg1, wu1, wd1, rows=(0, 1, 2))
        hq, g2, lk, hi, gate, aq, ak, av = _project(x, mod, norm_mix[l], w_in_b, q_norm[l], k_norm[l], cos_t, sin_t,
                                                    lb_logits, l)
        o_h = _hgrn(hgrn_norm[l], hq, g2, lk, hi, gate)
        o_a = _attention(sinks[l], aq, ak, av)
        x = _ffn(x, mod, l, norm_ffn2[l], wg2, wu2, wd2, rows=(6, 7, 8), mix=(o_h, o_a, w_out_b))
    return x
```

```python
import functools

import numpy as np
import jax
import jax.numpy as jnp
from jax import lax
from jax.experimental import pallas as pl
from jax.experimental.pallas import tpu as pltpu

F32 = jnp.float32
BF16 = jnp.bfloat16

HGRN_HEADS = 4
HGRN_DIM = 128
HGRN_WIDTH = HGRN_HEADS * HGRN_DIM
ATTN_HEADS = 8
KV_HEADS = 2
HEAD_DIM = 64
ATTN_WIDTH = ATTN_HEADS * HEAD_DIM
KV_WIDTH = KV_HEADS * HEAD_DIM
GROUP = ATTN_HEADS // KV_HEADS
WINDOW = 128
ROPE_THETA = 500000.0
ROPE_DIM = HEAD_DIM // 4
N_MOD = 9
EPS = 1e-6

LANES = 128
VMEM_LIMIT_BYTES = 56 * 1024 * 1024

MOD_TN = 2304
ROPE_TM = 1024
FFN_TM = 1024
FFN_SUB = 512
FFN_PROJ_TM = 512
FFN_PROJ_SUB = 256
FFN_CHUNKS = (1536, 1280)
HGRN_CHUNK = 256
HGRN_DIAG = 8
ATTN_QB = 4
HGRN_HALF = 128
LOG2E = 1.4426950408889634


def _params(sem):
    return pltpu.CompilerParams(dimension_semantics=sem, vmem_limit_bytes=VMEM_LIMIT_BYTES)


def _resident(shape):
    nd = len(shape)
    return pl.BlockSpec(shape, lambda *_: (0,) * nd, pipeline_mode=pl.Buffered(1))


def _layer_resident(arr, l):
    nd = arr.ndim - 1
    return pl.BlockSpec((None,) + arr.shape[1:], lambda *_: (l,) + (0,) * nd, pipeline_mode=pl.Buffered(1))


def _mod_spec(mod, l):
    return pl.BlockSpec((None, None) + mod.shape[2:], lambda b, i: (l, b, 0, 0))


def _rms_mod(x, nw, sc, sh):
    ms = jnp.mean(x * x, axis=-1, keepdims=True)
    return (x * lax.rsqrt(ms + EPS)) * (nw * (1.0 + sc)) + sh


def _silu(x):
    return x * jax.nn.sigmoid(x)


def _mod_kernel(ct_ref, w_ref, b_ref, o_ref):
    ca = _silu(ct_ref[...])
    w = w_ref[...]
    rows = [jnp.sum(w * ca[:, b:b + 1], axis=0, keepdims=True) for b in range(ca.shape[1])]
    o_ref[...] = jnp.concatenate(rows, axis=0) + b_ref[...]


def _modulation(c, ada_w, ada_b):
    depth, d, n = ada_w.shape
    bsz = c.shape[0]
    out = pl.pallas_call(
        _mod_kernel,
        grid=(depth, n // MOD_TN),
        in_specs=[
            pl.BlockSpec((d, bsz), lambda l, j: (0, 0)),
            pl.BlockSpec((None, d, MOD_TN), lambda l, j: (l, 0, j)),
            pl.BlockSpec((None, 1, MOD_TN), lambda l, j: (l, 0, j)),
        ],
        out_specs=pl.BlockSpec((None, bsz, MOD_TN), lambda l, j: (l, 0, j)),
        out_shape=jax.ShapeDtypeStruct((depth, bsz, n), F32),
        compiler_params=_params(("arbitrary", "arbitrary")),
        name="adaln_mod",
    )(c.T, ada_w, ada_b.reshape(depth, 1, n))
    return out.reshape(depth, bsz, N_MOD, d)


def _rope_kernel(pos_ref, invf_ref, ec_ref, es_ref, base_ref, cos_ref, sin_ref):
    tn_dims = (((0,), (0,)), ((), ()))
    ang = invf_ref[...] * pos_ref[...].astype(F32)

    def expand(vals, e_ref):
        pad = jnp.zeros(vals.shape, BF16)
        stacked = jnp.concatenate(list(_split3(vals)) + [pad], axis=0)
        return lax.dot_general(stacked, e_ref[...], tn_dims, preferred_element_type=F32)

    cos_ref[...] = expand(jnp.cos(ang), ec_ref) + base_ref[...]
    sin_ref[...] = expand(jnp.sin(ang), es_ref)


def _rope_tables(positions):
    bsz, t = positions.shape
    nf = ROPE_DIM // 2
    inv_freq = ROPE_THETA ** (-jnp.arange(0, ROPE_DIM, 2, dtype=F32) / ROPE_DIM)
    lane = np.arange(2 * HEAD_DIM) % HEAD_DIM
    rot = lane < ROPE_DIM
    pick = (np.arange(nf)[:, None] == (lane % nf)[None, :]) & rot[None, :]
    e_cos = np.tile(pick.astype(np.float32), (4, 1))
    e_cos[3 * nf:] = 0.0
    e_sin = e_cos * np.where(lane < nf, -1.0, 1.0)[None, :]
    base = (~rot).astype(np.float32).reshape(1, 2 * HEAD_DIM)
    shp = jax.ShapeDtypeStruct((bsz, t, 2 * HEAD_DIM), F32)
    const = lambda shape: pl.BlockSpec(shape, lambda b, i: (0, 0))
    return pl.pallas_call(
        _rope_kernel,
        grid=(bsz, t // ROPE_TM),
        in_specs=[
            pl.BlockSpec((None, 1, ROPE_TM), lambda b, i: (b, 0, i)),
            const((nf, 1)),
            const((4 * nf, 2 * HEAD_DIM)),
            const((4 * nf, 2 * HEAD_DIM)),
            const((1, 2 * HEAD_DIM)),
        ],
        out_specs=[pl.BlockSpec((None, ROPE_TM, 2 * HEAD_DIM), lambda b, i: (b, i, 0))] * 2,
        out_shape=[shp, shp],
        compiler_params=_params(("arbitrary", "arbitrary")),
        name="rope_tables",
    )(positions.reshape(bsz, 1, t), inv_freq.reshape(nf, 1), jnp.asarray(e_cos, BF16), jnp.asarray(e_sin, BF16),
      jnp.asarray(base))


def _head_norm_rope(t, seg, nw, cos, sin, swap_lo):
    ms = jnp.dot((t * t).astype(BF16), seg, preferred_element_type=F32)
    y = (t * lax.rsqrt(ms + EPS)) * nw
    swapped = jnp.where(swap_lo, pltpu.roll(y, LANES - ROPE_DIM // 2, 1), pltpu.roll(y, ROPE_DIM // 2, 1))
    return y * cos + swapped * sin


def _lower_bound_logs(lbl, layer):
    e = jnp.exp(lbl - jnp.max(lbl, axis=0, keepdims=True))
    sm = e / jnp.sum(e, axis=0, keepdims=True)
    cum0 = sm[0:1]
    cum = cum0
    for j in range(1, layer + 1):
        cum = cum + sm[j:j + 1]
    lb = cum - cum0
    return jnp.log(lb), jnp.log1p(-lb)


def _proj_rows(x, m, nw, w_ref, qn, kn, cos, sin, seg, lbl_ref, outs, rs, layer):
    hq_ref, g2_ref, lk_ref, hi_ref, hg_ref, aq_ref, ak_ref, av_ref = outs
    h = _rms_mod(x, nw, m[4:5], m[3:4]).astype(BF16)

    proj = jnp.dot(h, w_ref[...], preferred_element_type=F32)

    def seg_dot(lo, width):
        return proj[:, lo:lo + width]

    hq_ref[rs, :] = (seg_dot(0, HGRN_WIDTH) * (HGRN_DIM ** -0.5)).astype(BF16)

    fl2 = seg_dot(HGRN_WIDTH, HGRN_WIDTH) * LOG2E
    sp2 = jnp.log2(1.0 + jnp.exp2(-jnp.abs(fl2)))
    ls2 = jnp.minimum(fl2, 0.0) - sp2
    lsn2 = -(jnp.maximum(fl2, 0.0) + sp2)
    if layer == 0:
        g2_ref[rs, :] = ls2
        lk_ref[rs, :] = lsn2
    else:
        log_lb, l1m = _lower_bound_logs(lbl_ref[...], layer)
        log_lb2 = log_lb * LOG2E
        l1m2 = l1m * LOG2E
        cc2 = l1m2 + ls2
        lk_ref[rs, :] = l1m2 + lsn2
        dlt = log_lb2 - cc2
        g2_ref[rs, :] = jnp.maximum(log_lb2, cc2) + jnp.log2(1.0 + jnp.exp2(-jnp.abs(dlt)))

    hi_ref[rs, :] = seg_dot(2 * HGRN_WIDTH, HGRN_WIDTH).astype(BF16)
    hg_ref[rs, :] = _silu(seg_dot(3 * HGRN_WIDTH, HGRN_WIDTH)).astype(BF16)
    base = 4 * HGRN_WIDTH
    aq = seg_dot(base, ATTN_WIDTH)
    ak = seg_dot(base + ATTN_WIDTH, KV_WIDTH)
    av_ref[rs, :] = seg_dot(base + ATTN_WIDTH + KV_WIDTH, KV_WIDTH).astype(BF16)

    lane = lax.broadcasted_iota(jnp.int32, cos.shape, 1)
    swap_lo = (lane & (HEAD_DIM - 1)) < ROPE_DIM // 2
    for j in range(ATTN_WIDTH // LANES):
        sl = slice(j * LANES, (j + 1) * LANES)
        aq_ref[rs, sl] = _head_norm_rope(aq[:, sl], seg, qn, cos, sin, swap_lo).astype(BF16)
    for j in range(KV_WIDTH // LANES):
        sl = slice(j * LANES, (j + 1) * LANES)
        ak_ref[rs, sl] = _head_norm_rope(ak[:, sl], seg, kn, cos, sin, swap_lo).astype(BF16)


def _hgrn_inner_levels():
    lv = []
    m = HGRN_DIAG
    while m < HGRN_HALF:
        lv.append(m)
        m *= 2
    return tuple(lv)


def _hgrn_code_table():
    n = HGRN_HALF
    t = np.arange(n)[:, None]
    s = np.arange(n)[None, :]
    x = t ^ s
    code = np.full((n, n), -2, np.int32)
    for idx, m in enumerate(_hgrn_inner_levels()):
        code[(x >= m) & (x < 2 * m) & (t > s)] = idx
    code[(t // HGRN_DIAG == s // HGRN_DIAG) & (t >= s)] = -1
    return np.tile(code, (HGRN_CHUNK // HGRN_HALF, 1))


def _split3(g):
    hi = g.astype(BF16)
    r1 = g - hi.astype(F32)
    mid = r1.astype(BF16)
    lo = (r1 - mid.astype(F32)).astype(BF16)
    return hi, mid, lo


def _hgrn_kernel(nw_ref, q_ref, g2_ref, lk_ref, i_ref, gate_ref, tri_ref, code_ref, exp_ref, o_ref, st_ref):
    c = HGRN_CHUNK
    dg = HGRN_DIAG
    hb = HGRN_HALF
    dh = HGRN_DIM
    nt_dims = (((1,), (1,)), ((), ()))
    tn_dims = (((0,), (0,)), ((), ()))

    @pl.when(pl.program_id(1) == 0)
    def _():
        st_ref[...] = jnp.zeros(st_ref.shape, F32)

    tri = tri_ref[...]
    b_all = sum(jnp.dot(tri, p, preferred_element_type=F32) for p in _split3(g2_ref[...]))

    code = code_ref[...]
    expand = exp_ref[...]
    nw = nw_ref[...]
    nb = c // dg
    lane3 = lax.broadcasted_iota(jnp.int32, (1, dg, dh), 2)
    row3 = lax.broadcasted_iota(jnp.int32, (1, dg, dh), 1)
    diag_code = jnp.where(lane3 <= row3, lane3, -1)

    for hd in range(HGRN_HEADS):
        sl = slice(hd * dh, (hd + 1) * dh)
        q = q_ref[:, sl].astype(F32)
        v = i_ref[:, sl]
        b = b_all[:, sl]
        lk = lk_ref[:, sl]
        b_last = b[c - 1:c, :]

        st = st_ref[hd]
        qq0 = (q * jnp.exp2(b)).astype(BF16)
        o = lax.dot_general(qq0, st.astype(BF16), nt_dims, preferred_element_type=F32)
        kk_end = jnp.exp2(lk + (b_last - b)).astype(BF16)
        st_ref[hd] = st * jnp.exp2(b_last) + lax.dot_general(v, kk_end, tn_dims, preferred_element_type=F32)

        att_d = None
        for idx, m in reversed(list(enumerate(_hgrn_inner_levels()))):
            shp = (c // (2 * m), 2 * m, dh)
            b3 = b.reshape(shp)
            mid = b3[:, m - 1:m, :]
            zero = jnp.zeros((shp[0], m, dh), F32)
            q_hi = q.reshape(shp)[:, m:, :] * jnp.exp2(b3[:, m:, :] - mid)
            k_lo = jnp.exp2(lk.reshape(shp)[:, :m, :] + (mid - b3[:, :m, :]))
            qm = jnp.concatenate([zero, q_hi], axis=1).reshape(c, dh).astype(BF16)
            km = jnp.concatenate([k_lo, zero], axis=1).reshape(c, dh).astype(BF16)
            a = lax.dot_general(qm, km, nt_dims, preferred_element_type=F32)
            a_d = jnp.concatenate([a[:hb, :hb], a[hb:, hb:]], axis=0)
            att_d = a_d if att_d is None else jnp.where(code == idx, a_d, att_d)

        mid = b[hb - 1:hb, :]
        q_top = (q[hb:] * jnp.exp2(b[hb:] - mid)).astype(BF16)
        k_top = jnp.exp2(lk[:hb] + (mid - b[:hb])).astype(BF16)
        a_top = lax.dot_general(q_top, k_top, nt_dims, preferred_element_type=F32)

        q3 = q.reshape(nb, dg, dh)
        b3 = b.reshape(nb, dg, dh)
        bk3 = (b - lk).reshape(nb, dg, dh)
        compact = jnp.zeros((nb, dg, dh), F32)
        for s_loc in range(dg):
            z = q3 * jnp.exp2(b3 - bk3[:, s_loc:s_loc + 1, :])
            col = jnp.sum(z, axis=-1, keepdims=True)
            compact = jnp.where(diag_code == s_loc, col, compact)
        compact = compact.reshape(c, dh)
        a_diag = jnp.dot(compact.astype(BF16), expand, preferred_element_type=F32)
        att_d = jnp.where(code == -1, a_diag, att_d)

        att = jnp.concatenate([
            jnp.concatenate([att_d[:hb], jnp.zeros((hb, hb), F32)], axis=1),
            jnp.concatenate([a_top, att_d[hb:]], axis=1)], axis=0)
        o = o + jnp.dot(att.astype(BF16), v, preferred_element_type=F32)
        ms = jnp.mean(o * o, axis=-1, keepdims=True)
        y = (o * lax.rsqrt(ms + EPS)) * nw
        o_ref[:, sl] = (y * gate_ref[:, sl].astype(F32)).astype(BF16)


def _hgrn(norm_w, hq, g2, lk, hi, gate):
    bsz, t, w = hq.shape
    c = HGRN_CHUNK
    assert c == 2 * HGRN_HALF and HGRN_HALF == LANES
    tri = jnp.asarray(np.tril(np.ones((c, c))), BF16)
    code = jnp.asarray(_hgrn_code_table())
    expand_np = (np.arange(HGRN_DIM)[:, None] == (np.arange(HGRN_HALF)[None, :] % HGRN_DIAG)).astype(np.float32)
    expand = jnp.asarray(expand_np, BF16)
    tok = pl.BlockSpec((None, c, w), lambda b, i: (b, i, 0))
    return pl.pallas_call(
        _hgrn_kernel,
        grid=(bsz, t // c),
        in_specs=[
            _resident((1, HGRN_DIM)),
            tok, tok, tok, tok, tok,
            _resident((c, c)),
            _resident((c, HGRN_HALF)),
            _resident((HGRN_DIM, HGRN_HALF)),
        ],
        out_specs=tok,
        out_shape=jax.ShapeDtypeStruct((bsz, t, w), BF16),
        scratch_shapes=[pltpu.VMEM((HGRN_HEADS, HGRN_DIM, HGRN_DIM), F32)],
        compiler_params=_params(("arbitrary", "arbitrary")),
        name="hgrn_scan",
    )(norm_w.reshape(1, HGRN_DIM), hq, g2, lk, hi, gate, tri, code, expand)


def _attn_kernel(sink_ref, q_ref, kc_ref, kp_ref, vc_ref, vp_ref, o_ref):
    blk = WINDOW
    n = pl.program_id(1)
    nt_dims = (((1,), (1,)), ((), ()))
    k_all = jnp.concatenate([kp_ref[...], kc_ref[...]], axis=0).astype(F32)
    v_all = jnp.concatenate([vp_ref[...], vc_ref[...]], axis=0).astype(F32)
    k_sw = pltpu.roll(k_all, HEAD_DIM, 1)
    v_sw = pltpu.roll(v_all, HEAD_DIM, 1)
    lo = lax.broadcasted_iota(jnp.int32, k_all.shape, 1) < HEAD_DIM

    qi = lax.broadcasted_iota(jnp.int32, (blk, 2 * blk), 0)
    kj = lax.broadcasted_iota(jnp.int32, (blk, 2 * blk), 1) - blk
    delta = qi - kj
    neg = jnp.full((blk, 2 * blk), -jnp.inf, F32)
    bias = jnp.where(delta >= 0, jnp.where(delta < WINDOW, 0.0, neg), neg)
    bias_first = jnp.where(n > 0, bias, jnp.where(kj >= 0, bias, neg))
    top = lax.broadcasted_iota(jnp.int32, (2 * blk, 1), 0) < blk

    for kh in range(KV_HEADS):
        own_k, other_k = (k_all, k_sw) if kh == 0 else (k_sw, k_all)
        own_v, other_v = (v_all, v_sw) if kh == 0 else (v_sw, v_all)
        kv_par = ((jnp.where(lo, own_k, 0.0).astype(BF16), jnp.where(lo, own_v, 0.0).astype(BF16)),
                  (jnp.where(lo, 0.0, other_k).astype(BF16), jnp.where(lo, 0.0, other_v).astype(BF16)))
        base = kh * GROUP * HEAD_DIM
        for j in range(ATTN_QB):
            rows = slice(j * blk, (j + 1) * blk)
            keys = slice(j * blk, (j + 2) * blk)
            bias_j = bias_first if j == 0 else bias
            bias2 = jnp.concatenate([bias_j, bias_j], axis=0)
            qs = jnp.concatenate([q_ref[rows, base:base + LANES], q_ref[rows, base + LANES:base + 2 * LANES]], axis=0)
            o = None
            for par, (kk, vv) in enumerate(kv_par):
                s = lax.dot_general(qs, kk[keys], nt_dims, preferred_element_type=F32) + bias2
                sink = jnp.where(top, sink_ref[kh * GROUP + par], sink_ref[kh * GROUP + 2 + par]) * LOG2E
                mx = jnp.maximum(jnp.max(s, axis=-1, keepdims=True), sink)
                p = jnp.exp2(s - mx)
                den = jnp.sum(p, axis=-1, keepdims=True) + jnp.exp2(sink - mx)
                part = jnp.dot(p.astype(BF16), vv[keys], preferred_element_type=F32) / den
                o = part if o is None else o + part
            o_ref[rows, base:base + LANES] = o[:blk].astype(BF16)
            o_ref[rows, base + LANES:base + 2 * LANES] = o[blk:].astype(BF16)


def _attention(sinks_l, aq, ak, av):
    bsz, t, _ = aq.shape
    blk = WINDOW
    qb = ATTN_QB
    cur = lambda w: pl.BlockSpec((None, qb * blk, w), lambda b, i: (b, i, 0))
    prev = lambda w: pl.BlockSpec((None, blk, w), lambda b, i: (b, jnp.maximum(i * qb - 1, 0), 0))
    return pl.pallas_call(
        _attn_kernel,
        grid=(bsz, t // (qb * blk)),
        in_specs=[
            pl.BlockSpec(memory_space=pltpu.SMEM),
            cur(ATTN_WIDTH), cur(KV_WIDTH), prev(KV_WIDTH), cur(KV_WIDTH), prev(KV_WIDTH),
        ],
        out_specs=cur(ATTN_WIDTH),
        out_shape=jax.ShapeDtypeStruct((bsz, t, ATTN_WIDTH), BF16),
        compiler_params=_params(("arbitrary", "arbitrary")),
        name="swa_attn",
    )(sinks_l, aq, ak, ak, av, av)


def _ffn_kernel(*refs, rows, tm, sub_rows, with_mix, proj_layer):
    refs = list(refs)
    x_ref, mod_ref, nw_ref, wg_ref, wu_ref, wd_ref = refs[:6]
    del refs[:6]
    if with_mix:
        oh_ref, oa_ref, wo_ref = refs[:3]
        del refs[:3]
    if proj_layer is not None:
        pnw_ref, win_ref, qn_ref, kn_ref, cos_ref, sin_ref, seg_ref, lbl_ref = refs[:8]
        del refs[:8]
    o_ref = refs[0]
    proj_outs = refs[1:-1]
    a_scr = refs[-1]

    m = mod_ref[...]
    sh, sc, gate = (m[r:r + 1] for r in rows)
    for sub in range(tm // sub_rows):
        rs = slice(sub * sub_rows, (sub + 1) * sub_rows)
        x = x_ref[rs, :]
        if with_mix:
            hw = oh_ref.shape[-1]
            mix = (jnp.dot(oh_ref[rs, :], wo_ref[:hw, :], preferred_element_type=F32)
                   + jnp.dot(oa_ref[rs, :], wo_ref[hw:, :], preferred_element_type=F32))
            x = x + m[5:6] * mix
        h = _rms_mod(x, nw_ref[...], sc, sh).astype(BF16)
        lo = 0
        for width in FFN_CHUNKS:
            gt = jnp.dot(h, wg_ref[:, lo:lo + width], preferred_element_type=F32)
            ut = jnp.dot(h, wu_ref[:, lo:lo + width], preferred_element_type=F32)
            a_scr[rs, lo:lo + width] = (_silu(gt) * ut).astype(BF16)
            lo += width
        y = jnp.dot(a_scr[rs, :], wd_ref[...], preferred_element_type=F32)
        x = x + (0.5 * gate) * y
        o_ref[rs, :] = x
        if proj_layer is not None:
            _proj_rows(x, m, pnw_ref[...], win_ref, qn_ref[...], kn_ref[...], cos_ref[rs, :], sin_ref[rs, :],
                       seg_ref[...], lbl_ref, proj_outs, rs, proj_layer)


def _ffn(x, mod, layer, nw, wg, wu, wd, rows, tm, sub_rows, mix=None, proj=None):
    bsz, t, d = x.shape
    dff = wg.shape[-1]
    assert sum(FFN_CHUNKS) == dff and tm % sub_rows == 0
    tok = lambda w: pl.BlockSpec((None, tm, w), lambda b, i: (b, i, 0))
    sds = lambda w, dt: jax.ShapeDtypeStruct((bsz, t, w), dt)
    in_specs = [
        tok(d),
        _mod_spec(mod, layer),
        _resident((1, d)),
        _layer_resident(wg, layer),
        _layer_resident(wu, layer),
        _layer_resident(wd, layer),
    ]
    args = [x, mod, nw.reshape(1, d), wg, wu, wd]
    out_specs = [tok(d)]
    out_shape = [sds(d, F32)]
    if mix is not None:
        o_h, o_a, w_out = mix
        in_specs += [tok(o_h.shape[-1]), tok(o_a.shape[-1]), _layer_resident(w_out, layer)]
        args += [o_h, o_a, w_out]
    if proj is not None:
        pnw, w_in, q_norm, k_norm, cos_t, sin_t, lb_logits = proj
        seg_np = np.kron(np.eye(LANES // HEAD_DIM), np.full((HEAD_DIM, HEAD_DIM), 1.0 / HEAD_DIM))
        qn = jnp.tile(q_norm * (LOG2E * HEAD_DIM ** -0.5), LANES // HEAD_DIM).reshape(1, LANES)
        kn = jnp.tile(k_norm, LANES // HEAD_DIM).reshape(1, LANES)
        in_specs += [_resident((1, d)), _layer_resident(w_in, layer), _resident((1, LANES)), _resident((1, LANES)),
                     tok(LANES), tok(LANES), _resident((LANES, LANES)), _resident(lb_logits.shape)]
        args += [pnw.reshape(1, d), w_in, qn, kn, cos_t, sin_t, jnp.asarray(seg_np, BF16), lb_logits]
        out_specs += [tok(HGRN_WIDTH)] * 5 + [tok(ATTN_WIDTH), tok(KV_WIDTH), tok(KV_WIDTH)]
        out_shape += [sds(HGRN_WIDTH, BF16), sds(HGRN_WIDTH, F32), sds(HGRN_WIDTH, F32), sds(HGRN_WIDTH, BF16),
                      sds(HGRN_WIDTH, BF16), sds(ATTN_WIDTH, BF16), sds(KV_WIDTH, BF16), sds(KV_WIDTH, BF16)]
    outs = pl.pallas_call(
        functools.partial(_ffn_kernel, rows=rows, tm=tm, sub_rows=sub_rows, with_mix=mix is not None,
                          proj_layer=layer if proj is not None else None),
        grid=(bsz, t // tm),
        in_specs=in_specs,
        out_specs=out_specs,
        out_shape=out_shape,
        scratch_shapes=[pltpu.VMEM((tm, dff), BF16)],
        compiler_params=_params(("arbitrary", "arbitrary")),
        name="ffn_mix" if mix is not None else "ffn_proj",
    )(*args)
    return outs[0] if proj is None else outs


def kernel(x, c, positions, ada_w, ada_b, norm_ffn1, ffn1_w_gate, ffn1_w_up, ffn1_w_down, norm_mix, w_in,
           lb_logits, hgrn_norm, q_norm, k_norm, sinks, w_out, norm_ffn2, ffn2_w_gate, ffn2_w_up, ffn2_w_down):
    depth = ada_w.shape[0]
    mod = _modulation(c, ada_w, ada_b)
    cos_t, sin_t = _rope_tables(positions)
    bf = lambda w: w.astype(BF16)
    wg1, wu1, wd1 = bf(ffn1_w_gate), bf(ffn1_w_up), bf(ffn1_w_down)
    wg2, wu2, wd2 = bf(ffn2_w_gate), bf(ffn2_w_up), bf(ffn2_w_down)
    w_in_b, w_out_b = bf(w_in), bf(w_out)
    for l in range(depth):
        x, hq, g2, lk, hi, gate, aq, ak, av = _ffn(
            x, mod, l, norm_ffn1[l], wg1, wu1, wd1, rows=(0, 1, 2), tm=FFN_PROJ_TM, sub_rows=FFN_PROJ_SUB,
            proj=(norm_mix[l], w_in_b, q_norm[l], k_norm[l], cos_t, sin_t, lb_logits))
        o_h = _hgrn(hgrn_norm[l], hq, g2, lk, hi, gate)
        o_a = _attention(sinks[l], aq, ak, av)
        x = _ffn(x, mod, l, norm_ffn2[l], wg2, wu2, wd2, rows=(6, 7, 8), tm=FFN_TM, sub_rows=FFN_SUB,
                 mix=(o_h, o_a, w_out_b))
    return x
```

```python
import functools

import numpy as np
import jax
import jax.numpy as jnp
from jax import lax
from jax.experimental import pallas as pl
from jax.experimental.pallas import tpu as pltpu

F32 = jnp.float32
BF16 = jnp.bfloat16

HGRN_HEADS = 4
HGRN_DIM = 128
HGRN_WIDTH = HGRN_HEADS * HGRN_DIM
ATTN_HEADS = 8
KV_HEADS = 2
HEAD_DIM = 64
ATTN_WIDTH = ATTN_HEADS * HEAD_DIM
KV_WIDTH = KV_HEADS * HEAD_DIM
GROUP = ATTN_HEADS // KV_HEADS
WINDOW = 128
ROPE_THETA = 500000.0
ROPE_DIM = HEAD_DIM // 4
N_MOD = 9
EPS = 1e-6

LANES = 128
VMEM_LIMIT_BYTES = 56 * 1024 * 1024

MOD_TN = 2304
ROPE_TM = 1024
FFN_TM = 1024
FFN_SUB = 256
FFN_MIX_SUB = 512
PROJ_TM = 1024
PROJ_SUB = 256
FFN_CHUNKS = (1536, 1280)
HGRN_CHUNK = 256
HGRN_DIAG = 8
ATTN_QB = 4
HGRN_HALF = 128
LOG2E = 1.4426950408889634


def _params(sem):
    return pltpu.CompilerParams(dimension_semantics=sem, vmem_limit_bytes=VMEM_LIMIT_BYTES)


def _resident(shape):
    nd = len(shape)
    return pl.BlockSpec(shape, lambda *_: (0,) * nd, pipeline_mode=pl.Buffered(1))


def _layer_resident(arr, l):
    nd = arr.ndim - 1
    return pl.BlockSpec((None,) + arr.shape[1:], lambda *_: (l,) + (0,) * nd, pipeline_mode=pl.Buffered(1))


def _mod_spec(mod, l):
    return pl.BlockSpec((None, None) + mod.shape[2:], lambda b, i: (l, b, 0, 0))


def _rms_mod(x, nw, sc, sh):
    ms = jnp.mean(x * x, axis=-1, keepdims=True)
    return (x * lax.rsqrt(ms + EPS)) * (nw * (1.0 + sc)) + sh


def _silu(x):
    return x * jax.nn.sigmoid(x)


def _mod_kernel(ct_ref, w_ref, b_ref, o_ref):
    ca = _silu(ct_ref[...])
    w = w_ref[...]
    rows = [jnp.sum(w * ca[:, b:b + 1], axis=0, keepdims=True) for b in range(ca.shape[1])]
    o_ref[...] = jnp.concatenate(rows, axis=0) + b_ref[...]


def _modulation(c, ada_w, ada_b):
    depth, d, n = ada_w.shape
    bsz = c.shape[0]
    out = pl.pallas_call(
        _mod_kernel,
        grid=(depth, n // MOD_TN),
        in_specs=[
            pl.BlockSpec((d, bsz), lambda l, j: (0, 0)),
            pl.BlockSpec((None, d, MOD_TN), lambda l, j: (l, 0, j)),
            pl.BlockSpec((None, 1, MOD_TN), lambda l, j: (l, 0, j)),
        ],
        out_specs=pl.BlockSpec((None, bsz, MOD_TN), lambda l, j: (l, 0, j)),
        out_shape=jax.ShapeDtypeStruct((depth, bsz, n), F32),
        compiler_params=_params(("arbitrary", "arbitrary")),
        name="adaln_mod",
    )(c.T, ada_w, ada_b.reshape(depth, 1, n))
    return out.reshape(depth, bsz, N_MOD, d)


def _rope_kernel(pos_ref, invf_ref, ec_ref, es_ref, base_ref, cos_ref, sin_ref):
    tn_dims = (((0,), (0,)), ((), ()))
    ang = invf_ref[...] * pos_ref[...].astype(F32)

    def expand(vals, e_ref):
        pad = jnp.zeros(vals.shape, BF16)
        stacked = jnp.concatenate(list(_split3(vals)) + [pad], axis=0)
        return lax.dot_general(stacked, e_ref[...], tn_dims, preferred_element_type=F32)

    cos_ref[...] = expand(jnp.cos(ang), ec_ref) + base_ref[...]
    sin_ref[...] = expand(jnp.sin(ang), es_ref)


def _rope_tables(positions):
    bsz, t = positions.shape
    nf = ROPE_DIM // 2
    inv_freq = ROPE_THETA ** (-jnp.arange(0, ROPE_DIM, 2, dtype=F32) / ROPE_DIM)
    lane = np.arange(2 * HEAD_DIM) % HEAD_DIM
    rot = lane < ROPE_DIM
    pick = (np.arange(nf)[:, None] == (lane % nf)[None, :]) & rot[None, :]
    e_cos = np.tile(pick.astype(np.float32), (4, 1))
    e_cos[3 * nf:] = 0.0
    e_sin = e_cos * np.where(lane < nf, -1.0, 1.0)[None, :]
    base = (~rot).astype(np.float32).reshape(1, 2 * HEAD_DIM)
    shp = jax.ShapeDtypeStruct((bsz, t, 2 * HEAD_DIM), F32)
    const = lambda shape: pl.BlockSpec(shape, lambda b, i: (0, 0))
    return pl.pallas_call(
        _rope_kernel,
        grid=(bsz, t // ROPE_TM),
        in_specs=[
            pl.BlockSpec((None, 1, ROPE_TM), lambda b, i: (b, 0, i)),
            const((nf, 1)),
            const((4 * nf, 2 * HEAD_DIM)),
            const((4 * nf, 2 * HEAD_DIM)),
            const((1, 2 * HEAD_DIM)),
        ],
        out_specs=[pl.BlockSpec((None, ROPE_TM, 2 * HEAD_DIM), lambda b, i: (b, i, 0))] * 2,
        out_shape=[shp, shp],
        compiler_params=_params(("arbitrary", "arbitrary")),
        name="rope_tables",
    )(positions.reshape(bsz, 1, t), inv_freq.reshape(nf, 1), jnp.asarray(e_cos, BF16), jnp.asarray(e_sin, BF16),
      jnp.asarray(base))


def _head_norm_rope(t, nw, cos, sin, swap_lo):
    sq = t * t
    first = lax.broadcasted_iota(jnp.int32, t.shape, 1) < HEAD_DIM
    s_lo = jnp.sum(jnp.where(first, sq, 0.0), axis=-1, keepdims=True)
    s_all = jnp.sum(sq, axis=-1, keepdims=True)
    ms = jnp.where(first, s_lo, s_all - s_lo) * (1.0 / HEAD_DIM)
    y = (t * lax.rsqrt(ms + EPS)) * nw
    swapped = jnp.where(swap_lo, pltpu.roll(y, LANES - ROPE_DIM // 2, 1), pltpu.roll(y, ROPE_DIM // 2, 1))
    return y * cos + swapped * sin


def _lower_bound_logs(lbl, layer):
    e = jnp.exp(lbl - jnp.max(lbl, axis=0, keepdims=True))
    sm = e / jnp.sum(e, axis=0, keepdims=True)
    cum0 = sm[0:1]
    cum = cum0
    for j in range(1, layer + 1):
        cum = cum + sm[j:j + 1]
    lb = cum - cum0
    return jnp.log(lb), jnp.log1p(-lb)


def _proj_rows(x, m, nw, w_ref, qn, kn, cos, sin, lbl_ref, outs, rs, layer):
    hq_ref, g2_ref, lk_ref, hi_ref, hg_ref, aq_ref, ak_ref, av_ref = outs
    h = _rms_mod(x, nw, m[4:5], m[3:4]).astype(BF16)

    proj = jnp.dot(h, w_ref[...], preferred_element_type=F32)

    def seg_dot(lo, width):
        return proj[:, lo:lo + width]

    hq_ref[rs, :] = (seg_dot(0, HGRN_WIDTH) * (HGRN_DIM ** -0.5)).astype(BF16)

    fl2 = seg_dot(HGRN_WIDTH, HGRN_WIDTH) * LOG2E
    sp2 = jnp.log2(1.0 + jnp.exp2(-jnp.abs(fl2)))
    ls2 = jnp.minimum(fl2, 0.0) - sp2
    lsn2 = -(jnp.maximum(fl2, 0.0) + sp2)
    if layer == 0:
        g2_ref[rs, :] = ls2
        lk_ref[rs, :] = lsn2
    else:
        log_lb, l1m = _lower_bound_logs(lbl_ref[...], layer)
        log_lb2 = log_lb * LOG2E
        l1m2 = l1m * LOG2E
        cc2 = l1m2 + ls2
        lk_ref[rs, :] = l1m2 + lsn2
        dlt = log_lb2 - cc2
        g2_ref[rs, :] = jnp.maximum(log_lb2, cc2) + jnp.log2(1.0 + jnp.exp2(-jnp.abs(dlt)))

    hi_ref[rs, :] = seg_dot(2 * HGRN_WIDTH, HGRN_WIDTH).astype(BF16)
    hg_ref[rs, :] = _silu(seg_dot(3 * HGRN_WIDTH, HGRN_WIDTH)).astype(BF16)
    base = 4 * HGRN_WIDTH
    aq = seg_dot(base, ATTN_WIDTH)
    ak = seg_dot(base + ATTN_WIDTH, KV_WIDTH)
    av_ref[rs, :] = seg_dot(base + ATTN_WIDTH + KV_WIDTH, KV_WIDTH).astype(BF16)

    lane = lax.broadcasted_iota(jnp.int32, cos.shape, 1)
    swap_lo = (lane & (HEAD_DIM - 1)) < ROPE_DIM // 2
    for j in range(ATTN_WIDTH // LANES):
        sl = slice(j * LANES, (j + 1) * LANES)
        aq_ref[rs, sl] = _head_norm_rope(aq[:, sl], qn, cos, sin, swap_lo).astype(BF16)
    for j in range(KV_WIDTH // LANES):
        sl = slice(j * LANES, (j + 1) * LANES)
        ak_ref[rs, sl] = _head_norm_rope(ak[:, sl], kn, cos, sin, swap_lo).astype(BF16)


def _proj_kernel(x_ref, mod_ref, pnw_ref, win_ref, qn_ref, kn_ref, cos_ref, sin_ref, lbl_ref, *outs,
                 tm, sub_rows, layer):
    m = mod_ref[...]
    for sub in range(tm // sub_rows):
        rs = slice(sub * sub_rows, (sub + 1) * sub_rows)
        _proj_rows(x_ref[rs, :], m, pnw_ref[...], win_ref, qn_ref[...], kn_ref[...], cos_ref[rs, :], sin_ref[rs, :],
                   lbl_ref, outs, rs, layer)


def _project(x, mod, layer, pnw, w_in, q_norm, k_norm, cos_t, sin_t, lb_logits, tm, sub_rows):
    bsz, t, d = x.shape
    tok = lambda w: pl.BlockSpec((None, tm, w), lambda b, i: (b, i, 0))
    sds = lambda w, dt: jax.ShapeDtypeStruct((bsz, t, w), dt)
    qn = jnp.tile(q_norm * (LOG2E * HEAD_DIM ** -0.5), LANES // HEAD_DIM).reshape(1, LANES)
    kn = jnp.tile(k_norm, LANES // HEAD_DIM).reshape(1, LANES)
    return pl.pallas_call(
        functools.partial(_proj_kernel, tm=tm, sub_rows=sub_rows, layer=layer),
        grid=(bsz, t // tm),
        in_specs=[tok(d), _mod_spec(mod, layer), _resident((1, d)), _layer_resident(w_in, layer),
                  _resident((1, LANES)), _resident((1, LANES)), tok(LANES), tok(LANES), _resident(lb_logits.shape)],
        out_specs=[tok(HGRN_WIDTH)] * 5 + [tok(ATTN_WIDTH), tok(KV_WIDTH), tok(KV_WIDTH)],
        out_shape=[sds(HGRN_WIDTH, BF16), sds(HGRN_WIDTH, F32), sds(HGRN_WIDTH, F32), sds(HGRN_WIDTH, BF16),
                   sds(HGRN_WIDTH, BF16), sds(ATTN_WIDTH, BF16), sds(KV_WIDTH, BF16), sds(KV_WIDTH, BF16)],
        compiler_params=_params(("arbitrary", "arbitrary")),
        name="in_proj",
    )(x, mod, pnw.reshape(1, d), w_in, qn, kn, cos_t, sin_t, lb_logits)


def _hgrn_inner_levels():
    lv = []
    m = HGRN_DIAG
    while m < HGRN_HALF:
        lv.append(m)
        m *= 2
    return tuple(lv)


def _hgrn_code_table():
    n = HGRN_HALF
    t = np.arange(n)[:, None]
    s = np.arange(n)[None, :]
    x = t ^ s
    code = np.full((n, n), -2, np.int32)
    for idx, m in enumerate(_hgrn_inner_levels()):
        code[(x >= m) & (x < 2 * m) & (t > s)] = idx
    code[(t // HGRN_DIAG == s // HGRN_DIAG) & (t >= s)] = -1
    return np.tile(code, (HGRN_CHUNK // HGRN_HALF, 1))


def _split3(g):
    hi = g.astype(BF16)
    r1 = g - hi.astype(F32)
    mid = r1.astype(BF16)
    lo = (r1 - mid.astype(F32)).astype(BF16)
    return hi, mid, lo


def _hgrn_kernel(nw_ref, q_ref, g2_ref, lk_ref, i_ref, gate_ref, tri_ref, code_ref, exp_ref, o_ref, st_ref):
    c = HGRN_CHUNK
    dg = HGRN_DIAG
    hb = HGRN_HALF
    dh = HGRN_DIM
    nt_dims = (((1,), (1,)), ((), ()))
    tn_dims = (((0,), (0,)), ((), ()))

    @pl.when(pl.program_id(1) == 0)
    def _():
        st_ref[...] = jnp.zeros(st_ref.shape, F32)

    tri = tri_ref[...]
    b_all = sum(jnp.dot(tri, p, preferred_element_type=F32) for p in _split3(g2_ref[...]))

    code = code_ref[...]
    expand = exp_ref[...]
    nw = nw_ref[...]
    nb = c // dg
    lane3 = lax.broadcasted_iota(jnp.int32, (1, dg, dh), 2)
    row3 = lax.broadcasted_iota(jnp.int32, (1, dg, dh), 1)
    diag_code = jnp.where(lane3 <= row3, lane3, -1)

    for hd in range(HGRN_HEADS):
        sl = slice(hd * dh, (hd + 1) * dh)
        q = q_ref[:, sl].astype(F32)
        v = i_ref[:, sl]
        b = b_all[:, sl]
        lk = lk_ref[:, sl]
        b_last = b[c - 1:c, :]

        st = st_ref[hd]
        qq0 = (q * jnp.exp2(b)).astype(BF16)
        o = lax.dot_general(qq0, st.astype(BF16), nt_dims, preferred_element_type=F32)
        kk_end = jnp.exp2(lk + (b_last - b)).astype(BF16)
        st_ref[hd] = st * jnp.exp2(b_last) + lax.dot_general(v, kk_end, tn_dims, preferred_element_type=F32)

        att_d = None
        for idx, m in reversed(list(enumerate(_hgrn_inner_levels()))):
            shp = (c // (2 * m), 2 * m, dh)
            b3 = b.reshape(shp)
            mid = b3[:, m - 1:m, :]
            zero = jnp.zeros((shp[0], m, dh), F32)
            q_hi = q.reshape(shp)[:, m:, :] * jnp.exp2(b3[:, m:, :] - mid)
            k_lo = jnp.exp2(lk.reshape(shp)[:, :m, :] + (mid - b3[:, :m, :]))
            qm = jnp.concatenate([zero, q_hi], axis=1).reshape(c, dh).astype(BF16)
            km = jnp.concatenate([k_lo, zero], axis=1).reshape(c, dh).astype(BF16)
            a = lax.dot_general(qm, km, nt_dims, preferred_element_type=F32)
            a_d = jnp.concatenate([a[:hb, :hb], a[hb:, hb:]], axis=0)
            att_d = a_d if att_d is None else jnp.where(code == idx, a_d, att_d)

        mid = b[hb - 1:hb, :]
        q_top = (q[hb:] * jnp.exp2(b[hb:] - mid)).astype(BF16)
        k_top = jnp.exp2(lk[:hb] + (mid - b[:hb])).astype(BF16)
        a_top = lax.dot_general(q_top, k_top, nt_dims, preferred_element_type=F32)

        q3 = q.reshape(nb, dg, dh)
        b3 = b.reshape(nb, dg, dh)
        bk3 = (b - lk).reshape(nb, dg, dh)
        compact = jnp.zeros((nb, dg, dh), F32)
        for s_loc in range(dg):
            z = q3 * jnp.exp2(b3 - bk3[:, s_loc:s_loc + 1, :])
            col = jnp.sum(z, axis=-1, keepdims=True)
            compact = jnp.where(diag_code == s_loc, col, compact)
        compact = compact.reshape(c, dh)
        a_diag = jnp.dot(compact.astype(BF16), expand, preferred_element_type=F32)
        att_d = jnp.where(code == -1, a_diag, att_d)

        att = jnp.concatenate([
            jnp.concatenate([att_d[:hb], jnp.zeros((hb, hb), F32)], axis=1),
            jnp.concatenate([a_top, att_d[hb:]], axis=1)], axis=0)
        o = o + jnp.dot(att.astype(BF16), v, preferred_element_type=F32)
        ms = jnp.mean(o * o, axis=-1, keepdims=True)
        y = (o * lax.rsqrt(ms + EPS)) * nw
        o_ref[:, sl] = (y * gate_ref[:, sl].astype(F32)).astype(BF16)


def _hgrn(norm_w, hq, g2, lk, hi, gate):
    bsz, t, w = hq.shape
    c = HGRN_CHUNK
    assert c == 2 * HGRN_HALF and HGRN_HALF == LANES
    tri = jnp.asarray(np.tril(np.ones((c, c))), BF16)
    code = jnp.asarray(_hgrn_code_table())
    expand_np = (np.arange(HGRN_DIM)[:, None] == (np.arange(HGRN_HALF)[None, :] % HGRN_DIAG)).astype(np.float32)
    expand = jnp.asarray(expand_np, BF16)
    tok = pl.BlockSpec((None, c, w), lambda b, i: (b, i, 0))
    return pl.pallas_call(
        _hgrn_kernel,
        grid=(bsz, t // c),
        in_specs=[
            _resident((1, HGRN_DIM)),
            tok, tok, tok, tok, tok,
            _resident((c, c)),
            _resident((c, HGRN_HALF)),
            _resident((HGRN_DIM, HGRN_HALF)),
        ],
        out_specs=tok,
        out_shape=jax.ShapeDtypeStruct((bsz, t, w), BF16),
        scratch_shapes=[pltpu.VMEM((HGRN_HEADS, HGRN_DIM, HGRN_DIM), F32)],
        compiler_params=_params(("arbitrary", "arbitrary")),
        name="hgrn_scan",
    )(norm_w.reshape(1, HGRN_DIM), hq, g2, lk, hi, gate, tri, code, expand)


def _attn_kernel(sink_ref, q_ref, kc_ref, kp_ref, vc_ref, vp_ref, o_ref):
    blk = WINDOW
    n = pl.program_id(1)
    nt_dims = (((1,), (1,)), ((), ()))
    k_all = jnp.concatenate([kp_ref[...], kc_ref[...]], axis=0).astype(F32)
    v_all = jnp.concatenate([vp_ref[...], vc_ref[...]], axis=0).astype(F32)
    k_sw = pltpu.roll(k_all, HEAD_DIM, 1)
    v_sw = pltpu.roll(v_all, HEAD_DIM, 1)
    lo = lax.broadcasted_iota(jnp.int32, k_all.shape, 1) < HEAD_DIM

    qi = lax.broadcasted_iota(jnp.int32, (blk, 2 * blk), 0)
    kj = lax.broadcasted_iota(jnp.int32, (blk, 2 * blk), 1) - blk
    delta = qi - kj
    neg = jnp.full((blk, 2 * blk), -jnp.inf, F32)
    bias = jnp.where(delta >= 0, jnp.where(delta < WINDOW, 0.0, neg), neg)
    bias_first = jnp.where(n > 0, bias, jnp.where(kj >= 0, bias, neg))
    top = lax.broadcasted_iota(jnp.int32, (2 * blk, 1), 0) < blk

    for kh in range(KV_HEADS):
        own_k, other_k = (k_all, k_sw) if kh == 0 else (k_sw, k_all)
        own_v, other_v = (v_all, v_sw) if kh == 0 else (v_sw, v_all)
        kv_par = ((jnp.where(lo, own_k, 0.0).astype(BF16), jnp.where(lo, own_v, 0.0).astype(BF16)),
                  (jnp.where(lo, 0.0, other_k).astype(BF16), jnp.where(lo, 0.0, other_v).astype(BF16)))
        base = kh * GROUP * HEAD_DIM
        for j in range(ATTN_QB):
            rows = slice(j * blk, (j + 1) * blk)
            keys = slice(j * blk, (j + 2) * blk)
            bias_j = bias_first if j == 0 else bias
            bias2 = jnp.concatenate([bias_j, bias_j], axis=0)
            qs = jnp.concatenate([q_ref[rows, base:base + LANES], q_ref[rows, base + LANES:base + 2 * LANES]], axis=0)
            o = None
            for par, (kk, vv) in enumerate(kv_par):
                s = lax.dot_general(qs, kk[keys], nt_dims, preferred_element_type=F32) + bias2
                sink = jnp.where(top, sink_ref[kh * GROUP + par], sink_ref[kh * GROUP + 2 + par]) * LOG2E
                mx = jnp.maximum(jnp.max(s, axis=-1, keepdims=True), sink)
                p = jnp.exp2(s - mx)
                den = jnp.sum(p, axis=-1, keepdims=True) + jnp.exp2(sink - mx)
                part = jnp.dot(p.astype(BF16), vv[keys], preferred_element_type=F32) / den
                o = part if o is None else o + part
            o_ref[rows, base:base + LANES] = o[:blk].astype(BF16)
            o_ref[rows, base + LANES:base + 2 * LANES] = o[blk:].astype(BF16)


def _attention(sinks_l, aq, ak, av):
    bsz, t, _ = aq.shape
    blk = WINDOW
    qb = ATTN_QB
    cur = lambda w: pl.BlockSpec((None, qb * blk, w), lambda b, i: (b, i, 0))
    prev = lambda w: pl.BlockSpec((None, blk, w), lambda b, i: (b, jnp.maximum(i * qb - 1, 0), 0))
    return pl.pallas_call(
        _attn_kernel,
        grid=(bsz, t // (qb * blk)),
        in_specs=[
            pl.BlockSpec(memory_space=pltpu.SMEM),
            cur(ATTN_WIDTH), cur(KV_WIDTH), prev(KV_WIDTH), cur(KV_WIDTH), prev(KV_WIDTH),
        ],
        out_specs=cur(ATTN_WIDTH),
        out_shape=jax.ShapeDtypeStruct((bsz, t, ATTN_WIDTH), BF16),
        compiler_params=_params(("arbitrary", "arbitrary")),
        name="swa_attn",
    )(sinks_l, aq, ak, ak, av, av)


def _ffn_kernel(*refs, rows, tm, sub_rows, with_mix):
    if with_mix:
        x_ref, mod_ref, nw_ref, wg_ref, wu_ref, wd_ref, oh_ref, oa_ref, wo_ref, o_ref, a_scr = refs
    else:
        x_ref, mod_ref, nw_ref, wg_ref, wu_ref, wd_ref, o_ref, a_scr = refs
    m = mod_ref[...]
    sh, sc, gate = (m[r:r + 1] for r in rows)
    for sub in range(tm // sub_rows):
        rs = slice(sub * sub_rows, (sub + 1) * sub_rows)
        x = x_ref[rs, :]
        if with_mix:
            hw = oh_ref.shape[-1]
            mix = (jnp.dot(oh_ref[rs, :], wo_ref[:hw, :], preferred_element_type=F32)
                   + jnp.dot(oa_ref[rs, :], wo_ref[hw:, :], preferred_element_type=F32))
            x = x + m[5:6] * mix
        h = _rms_mod(x, nw_ref[...], sc, sh).astype(BF16)
        lo = 0
        for width in FFN_CHUNKS:
            gt = jnp.dot(h, wg_ref[:, lo:lo + width], preferred_element_type=F32)
            ut = jnp.dot(h, wu_ref[:, lo:lo + width], preferred_element_type=F32)
            a_scr[rs, lo:lo + width] = (_silu(gt) * ut).astype(BF16)
            lo += width
        y = jnp.dot(a_scr[rs, :], wd_ref[...], preferred_element_type=F32)
        o_ref[rs, :] = x + (0.5 * gate) * y


def _ffn(x, mod, layer, nw, wg, wu, wd, rows, tm, sub_rows, mix=None):
    bsz, t, d = x.shape
    dff = wg.shape[-1]
    assert sum(FFN_CHUNKS) == dff and tm % sub_rows == 0
    tok = lambda w: pl.BlockSpec((None, tm, w), lambda b, i: (b, i, 0))
    in_specs = [
        tok(d),
        _mod_spec(mod, layer),
        _resident((1, d)),
        _layer_resident(wg, layer),
        _layer_resident(wu, layer),
        _layer_resident(wd, layer),
    ]
    args = [x, mod, nw.reshape(1, d), wg, wu, wd]
    if mix is not None:
        o_h, o_a, w_out = mix
        in_specs += [tok(o_h.shape[-1]), tok(o_a.shape[-1]), _layer_resident(w_out, layer)]
        args += [o_h, o_a, w_out]
    return pl.pallas_call(
        functools.partial(_ffn_kernel, rows=rows, tm=tm, sub_rows=sub_rows, with_mix=mix is not None),
        grid=(bsz, t // tm),
        in_specs=in_specs,
        out_specs=tok(d),
        out_shape=jax.ShapeDtypeStruct((bsz, t, d), F32),
        scratch_shapes=[pltpu.VMEM((tm, dff), BF16)],
        compiler_params=_params(("arbitrary", "arbitrary")),
        name="ffn_mix" if mix is not None else "ffn",
    )(*args)


def kernel(x, c, positions, ada_w, ada_b, norm_ffn1, ffn1_w_gate, ffn1_w_up, ffn1_w_down, norm_mix, w_in,
           lb_logits, hgrn_norm, q_norm, k_norm, sinks, w_out, norm_ffn2, ffn2_w_gate, ffn2_w_up, ffn2_w_down):
    depth = ada_w.shape[0]
    mod = _modulation(c, ada_w, ada_b)
    cos_t, sin_t = _rope_tables(positions)
    bf = lambda w: w.astype(BF16)
    wg1, wu1, wd1 = bf(ffn1_w_gate), bf(ffn1_w_up), bf(ffn1_w_down)
    wg2, wu2, wd2 = bf(ffn2_w_gate), bf(ffn2_w_up), bf(ffn2_w_down)
    w_in_b, w_out_b = bf(w_in), bf(w_out)
    for l in range(depth):
        x = _ffn(x, mod, l, norm_ffn1[l], wg1, wu1, wd1, rows=(0, 1, 2), tm=FFN_TM, sub_rows=FFN_SUB)
        hq, g2, lk, hi, gate, aq, ak, av = _project(x, mod, l, norm_mix[l], w_in_b, q_norm[l], k_norm[l], cos_t, sin_t,
                                                    lb_logits, tm=PROJ_TM, sub_rows=PROJ_SUB)
        o_h = _hgrn(hgrn_norm[l], hq, g2, lk, hi, gate)
        o_a = _attention(sinks[l], aq, ak, av)
        x = _ffn(x, mod, l, norm_ffn2[l], wg2, wu2, wd2, rows=(6, 7, 8), tm=FFN_TM, sub_rows=FFN_MIX_SUB,
                 mix=(o_h, o_a, w_out_b))
    return x
```

```python
import functools

import numpy as np
import jax
import jax.numpy as jnp
from jax import lax
from jax.experimental import pallas as pl
from jax.experimental.pallas import tpu as pltpu

F32 = jnp.float32
BF16 = jnp.bfloat16

HGRN_HEADS = 4
HGRN_DIM = 128
HGRN_WIDTH = HGRN_HEADS * HGRN_DIM
ATTN_HEADS = 8
KV_HEADS = 2
HEAD_DIM = 64
ATTN_WIDTH = ATTN_HEADS * HEAD_DIM
KV_WIDTH = KV_HEADS * HEAD_DIM
MIX_WIDTH = HGRN_WIDTH + ATTN_WIDTH
GROUP = ATTN_HEADS // KV_HEADS
WINDOW = 128
ROPE_THETA = 500000.0
ROPE_DIM = HEAD_DIM // 4
N_MOD = 9
EPS = 1e-6

LANES = 128
VMEM_LIMIT_BYTES = 56 * 1024 * 1024

MOD_TN = 2304
ROPE_TM = 1024
FFN_TM = 1024
FFN_SUB = 256
FFN_MIX_SUB = 512
PROJ_TM = 1024
PROJ_SUB = 256
FFN_CHUNKS = (1536, 1280)
HGRN_CHUNK = 256
HGRN_DIAG = 8
MIX_TM = 1024
HGRN_HALF = 128
LOG2E = 1.4426950408889634


def _params(sem):
    return pltpu.CompilerParams(dimension_semantics=sem, vmem_limit_bytes=VMEM_LIMIT_BYTES)


def _resident(shape):
    nd = len(shape)
    return pl.BlockSpec(shape, lambda *_: (0,) * nd, pipeline_mode=pl.Buffered(1))


def _layer_resident(arr, l):
    nd = arr.ndim - 1
    return pl.BlockSpec((None,) + arr.shape[1:], lambda *_: (l,) + (0,) * nd, pipeline_mode=pl.Buffered(1))


def _mod_spec(mod, l):
    return pl.BlockSpec((None, None) + mod.shape[2:], lambda b, i: (l, b, 0, 0))


def _rms_mod(x, nw, sc, sh):
    ms = jnp.mean(x * x, axis=-1, keepdims=True)
    return (x * lax.rsqrt(ms + EPS)) * (nw * (1.0 + sc)) + sh


def _silu(x):
    return x * jax.nn.sigmoid(x)


def _mod_kernel(ct_ref, w_ref, b_ref, o_ref):
    ca = _silu(ct_ref[...])
    w = w_ref[...]
    rows = [jnp.sum(w * ca[:, b:b + 1], axis=0, keepdims=True) for b in range(ca.shape[1])]
    o_ref[...] = jnp.concatenate(rows, axis=0) + b_ref[...]


def _modulation(c, ada_w, ada_b):
    depth, d, n = ada_w.shape
    bsz = c.shape[0]
    out = pl.pallas_call(
        _mod_kernel,
        grid=(depth, n // MOD_TN),
        in_specs=[
            pl.BlockSpec((d, bsz), lambda l, j: (0, 0)),
            pl.BlockSpec((None, d, MOD_TN), lambda l, j: (l, 0, j)),
            pl.BlockSpec((None, 1, MOD_TN), lambda l, j: (l, 0, j)),
        ],
        out_specs=pl.BlockSpec((None, bsz, MOD_TN), lambda l, j: (l, 0, j)),
        out_shape=jax.ShapeDtypeStruct((depth, bsz, n), F32),
        compiler_params=_params(("arbitrary", "arbitrary")),
        name="adaln_mod",
    )(c.T, ada_w, ada_b.reshape(depth, 1, n))
    return out.reshape(depth, bsz, N_MOD, d)


def _rope_kernel(pos_ref, invf_ref, ec_ref, es_ref, base_ref, cos_ref, sin_ref):
    tn_dims = (((0,), (0,)), ((), ()))
    ang = invf_ref[...] * pos_ref[...].astype(F32)

    def expand(vals, e_ref):
        pad = jnp.zeros(vals.shape, BF16)
        stacked = jnp.concatenate(list(_split3(vals)) + [pad], axis=0)
        return lax.dot_general(stacked, e_ref[...], tn_dims, preferred_element_type=F32)

    cos_ref[...] = expand(jnp.cos(ang), ec_ref) + base_ref[...]
    sin_ref[...] = expand(jnp.sin(ang), es_ref)


def _rope_tables(positions):
    bsz, t = positions.shape
    nf = ROPE_DIM // 2
    inv_freq = ROPE_THETA ** (-jnp.arange(0, ROPE_DIM, 2, dtype=F32) / ROPE_DIM)
    lane = np.arange(2 * HEAD_DIM) % HEAD_DIM
    rot = lane < ROPE_DIM
    pick = (np.arange(nf)[:, None] == (lane % nf)[None, :]) & rot[None, :]
    e_cos = np.tile(pick.astype(np.float32), (4, 1))
    e_cos[3 * nf:] = 0.0
    e_sin = e_cos * np.where(lane < nf, -1.0, 1.0)[None, :]
    base = (~rot).astype(np.float32).reshape(1, 2 * HEAD_DIM)
    shp = jax.ShapeDtypeStruct((bsz, t, 2 * HEAD_DIM), F32)
    const = lambda shape: pl.BlockSpec(shape, lambda b, i: (0, 0))
    return pl.pallas_call(
        _rope_kernel,
        grid=(bsz, t // ROPE_TM),
        in_specs=[
            pl.BlockSpec((None, 1, ROPE_TM), lambda b, i: (b, 0, i)),
            const((nf, 1)),
            const((4 * nf, 2 * HEAD_DIM)),
            const((4 * nf, 2 * HEAD_DIM)),
            const((1, 2 * HEAD_DIM)),
        ],
        out_specs=[pl.BlockSpec((None, ROPE_TM, 2 * HEAD_DIM), lambda b, i: (b, i, 0))] * 2,
        out_shape=[shp, shp],
        compiler_params=_params(("arbitrary", "arbitrary")),
        name="rope_tables",
    )(positions.reshape(bsz, 1, t), inv_freq.reshape(nf, 1), jnp.asarray(e_cos, BF16), jnp.asarray(e_sin, BF16),
      jnp.asarray(base))


def _head_norm_rope(t, nw, cos, sin, swap_lo):
    sq = t * t
    first = lax.broadcasted_iota(jnp.int32, t.shape, 1) < HEAD_DIM
    s_lo = jnp.sum(jnp.where(first, sq, 0.0), axis=-1, keepdims=True)
    s_all = jnp.sum(sq, axis=-1, keepdims=True)
    ms = jnp.where(first, s_lo, s_all - s_lo) * (1.0 / HEAD_DIM)
    y = (t * lax.rsqrt(ms + EPS)) * nw
    swapped = jnp.where(swap_lo, pltpu.roll(y, LANES - ROPE_DIM // 2, 1), pltpu.roll(y, ROPE_DIM // 2, 1))
    return y * cos + swapped * sin


def _lower_bound_logs(lbl, layer):
    e = jnp.exp(lbl - jnp.max(lbl, axis=0, keepdims=True))
    sm = e / jnp.sum(e, axis=0, keepdims=True)
    cum0 = sm[0:1]
    cum = cum0
    for j in range(1, layer + 1):
        cum = cum + sm[j:j + 1]
    lb = cum - cum0
    return jnp.log(lb), jnp.log1p(-lb)


def _proj_rows(x, m, nw, w_ref, qn, kn, cos, sin, lbl_ref, outs, rs, layer):
    hq_ref, g2_ref, lk_ref, hi_ref, hg_ref, aq_ref, ak_ref, av_ref = outs
    h = _rms_mod(x, nw, m[4:5], m[3:4]).astype(BF16)

    proj = jnp.dot(h, w_ref[...], preferred_element_type=F32)

    def seg_dot(lo, width):
        return proj[:, lo:lo + width]

    hq_ref[rs, :] = (seg_dot(0, HGRN_WIDTH) * (HGRN_DIM ** -0.5)).astype(BF16)

    fl2 = seg_dot(HGRN_WIDTH, HGRN_WIDTH) * LOG2E
    sp2 = jnp.log2(1.0 + jnp.exp2(-jnp.abs(fl2)))
    ls2 = jnp.minimum(fl2, 0.0) - sp2
    lsn2 = -(jnp.maximum(fl2, 0.0) + sp2)
    if layer == 0:
        g2_ref[rs, :] = ls2
        lk_ref[rs, :] = lsn2
    else:
        log_lb, l1m = _lower_bound_logs(lbl_ref[...], layer)
        log_lb2 = log_lb * LOG2E
        l1m2 = l1m * LOG2E
        cc2 = l1m2 + ls2
        lk_ref[rs, :] = l1m2 + lsn2
        dlt = log_lb2 - cc2
        g2_ref[rs, :] = jnp.maximum(log_lb2, cc2) + jnp.log2(1.0 + jnp.exp2(-jnp.abs(dlt)))

    hi_ref[rs, :] = seg_dot(2 * HGRN_WIDTH, HGRN_WIDTH).astype(BF16)
    hg_ref[rs, :] = _silu(seg_dot(3 * HGRN_WIDTH, HGRN_WIDTH)).astype(BF16)
    base = 4 * HGRN_WIDTH
    aq = seg_dot(base, ATTN_WIDTH)
    ak = seg_dot(base + ATTN_WIDTH, KV_WIDTH)
    av_ref[rs, :] = seg_dot(base + ATTN_WIDTH + KV_WIDTH, KV_WIDTH).astype(BF16)

    lane = lax.broadcasted_iota(jnp.int32, cos.shape, 1)
    swap_lo = (lane & (HEAD_DIM - 1)) < ROPE_DIM // 2
    for j in range(ATTN_WIDTH // LANES):
        sl = slice(j * LANES, (j + 1) * LANES)
        aq_ref[rs, sl] = _head_norm_rope(aq[:, sl], qn, cos, sin, swap_lo).astype(BF16)
    for j in range(KV_WIDTH // LANES):
        sl = slice(j * LANES, (j + 1) * LANES)
        ak_ref[rs, sl] = _head_norm_rope(ak[:, sl], kn, cos, sin, swap_lo).astype(BF16)


def _proj_kernel(x_ref, mod_ref, pnw_ref, win_ref, qn_ref, kn_ref, cos_ref, sin_ref, lbl_ref, *outs,
                 tm, sub_rows, layer):
    m = mod_ref[...]
    for sub in range(tm // sub_rows):
        rs = slice(sub * sub_rows, (sub + 1) * sub_rows)
        _proj_rows(x_ref[rs, :], m, pnw_ref[...], win_ref, qn_ref[...], kn_ref[...], cos_ref[rs, :], sin_ref[rs, :],
                   lbl_ref, outs, rs, layer)


def _project(x, mod, layer, pnw, w_in, q_norm, k_norm, cos_t, sin_t, lb_logits, tm, sub_rows):
    bsz, t, d = x.shape
    tok = lambda w: pl.BlockSpec((None, tm, w), lambda b, i: (b, i, 0))
    sds = lambda w, dt: jax.ShapeDtypeStruct((bsz, t, w), dt)
    qn = jnp.tile(q_norm * (LOG2E * HEAD_DIM ** -0.5), LANES // HEAD_DIM).reshape(1, LANES)
    kn = jnp.tile(k_norm, LANES // HEAD_DIM).reshape(1, LANES)
    return pl.pallas_call(
        functools.partial(_proj_kernel, tm=tm, sub_rows=sub_rows, layer=layer),
        grid=(bsz, t // tm),
        in_specs=[tok(d), _mod_spec(mod, layer), _resident((1, d)), _layer_resident(w_in, layer),
                  _resident((1, LANES)), _resident((1, LANES)), tok(LANES), tok(LANES), _resident(lb_logits.shape)],
        out_specs=[tok(HGRN_WIDTH)] * 5 + [tok(ATTN_WIDTH), tok(KV_WIDTH), tok(KV_WIDTH)],
        out_shape=[sds(HGRN_WIDTH, BF16), sds(HGRN_WIDTH, F32), sds(HGRN_WIDTH, F32), sds(HGRN_WIDTH, BF16),
                   sds(HGRN_WIDTH, BF16), sds(ATTN_WIDTH, BF16), sds(KV_WIDTH, BF16), sds(KV_WIDTH, BF16)],
        compiler_params=_params(("arbitrary", "arbitrary")),
        name="in_proj",
    )(x, mod, pnw.reshape(1, d), w_in, qn, kn, cos_t, sin_t, lb_logits)


def _hgrn_inner_levels():
    lv = []
    m = HGRN_DIAG
    while m < HGRN_HALF:
        lv.append(m)
        m *= 2
    return tuple(lv)


def _hgrn_code_table():
    n = HGRN_HALF
    t = np.arange(n)[:, None]
    s = np.arange(n)[None, :]
    x = t ^ s
    code = np.full((n, n), -2, np.int32)
    for idx, m in enumerate(_hgrn_inner_levels()):
        code[(x >= m) & (x < 2 * m) & (t > s)] = idx
    code[(t // HGRN_DIAG == s // HGRN_DIAG) & (t >= s)] = -1
    return np.tile(code, (HGRN_CHUNK // HGRN_HALF, 1))


def _split3(g):
    hi = g.astype(BF16)
    r1 = g - hi.astype(F32)
    mid = r1.astype(BF16)
    lo = (r1 - mid.astype(F32)).astype(BF16)
    return hi, mid, lo


def _hgrn_chunk(nw_ref, q_ref, g2_ref, lk_ref, i_ref, gate_ref, tri_ref, code_ref, exp_ref, o_ref, st_ref, rs):
    c = HGRN_CHUNK
    dg = HGRN_DIAG
    hb = HGRN_HALF
    dh = HGRN_DIM
    nt_dims = (((1,), (1,)), ((), ()))
    tn_dims = (((0,), (0,)), ((), ()))

    tri = tri_ref[...]
    b_all = sum(jnp.dot(tri, p, preferred_element_type=F32) for p in _split3(g2_ref[rs, :]))

    code = code_ref[...]
    expand = exp_ref[...]
    nw = nw_ref[...]
    nb = c // dg
    lane3 = lax.broadcasted_iota(jnp.int32, (1, dg, dh), 2)
    row3 = lax.broadcasted_iota(jnp.int32, (1, dg, dh), 1)
    diag_code = jnp.where(lane3 <= row3, lane3, -1)

    for hd in range(HGRN_HEADS):
        sl = slice(hd * dh, (hd + 1) * dh)
        q = q_ref[rs, sl].astype(F32)
        v = i_ref[rs, sl]
        b = b_all[:, sl]
        lk = lk_ref[rs, sl]
        b_last = b[c - 1:c, :]

        st = st_ref[hd]
        qq0 = (q * jnp.exp2(b)).astype(BF16)
        o = lax.dot_general(qq0, st.astype(BF16), nt_dims, preferred_element_type=F32)
        kk_end = jnp.exp2(lk + (b_last - b)).astype(BF16)
        st_ref[hd] = st * jnp.exp2(b_last) + lax.dot_general(v, kk_end, tn_dims, preferred_element_type=F32)

        att_d = None
        for idx, m in reversed(list(enumerate(_hgrn_inner_levels()))):
            shp = (c // (2 * m), 2 * m, dh)
            b3 = b.reshape(shp)
            mid = b3[:, m - 1:m, :]
            zero = jnp.zeros((shp[0], m, dh), F32)
            q_hi = q.reshape(shp)[:, m:, :] * jnp.exp2(b3[:, m:, :] - mid)
            k_lo = jnp.exp2(lk.reshape(shp)[:, :m, :] + (mid - b3[:, :m, :]))
            qm = jnp.concatenate([zero, q_hi], axis=1).reshape(c, dh).astype(BF16)
            km = jnp.concatenate([k_lo, zero], axis=1).reshape(c, dh).astype(BF16)
            a = lax.dot_general(qm, km, nt_dims, preferred_element_type=F32)
            a_d = jnp.concatenate([a[:hb, :hb], a[hb:, hb:]], axis=0)
            att_d = a_d if att_d is None else jnp.where(code == idx, a_d, att_d)

        mid = b[hb - 1:hb, :]
        q_top = (q[hb:] * jnp.exp2(b[hb:] - mid)).astype(BF16)
        k_top = jnp.exp2(lk[:hb] + (mid - b[:hb])).astype(BF16)
        a_top = lax.dot_general(q_top, k_top, nt_dims, preferred_element_type=F32)

        q3 = q.reshape(nb, dg, dh)
        b3 = b.reshape(nb, dg, dh)
        bk3 = (b - lk).reshape(nb, dg, dh)
        compact = jnp.zeros((nb, dg, dh), F32)
        for s_loc in range(dg):
            z = q3 * jnp.exp2(b3 - bk3[:, s_loc:s_loc + 1, :])
            col = jnp.sum(z, axis=-1, keepdims=True)
            compact = jnp.where(diag_code == s_loc, col, compact)
        compact = compact.reshape(c, dh)
        a_diag = jnp.dot(compact.astype(BF16), expand, preferred_element_type=F32)
        att_d = jnp.where(code == -1, a_diag, att_d)

        att = jnp.concatenate([
            jnp.concatenate([att_d[:hb], jnp.zeros((hb, hb), F32)], axis=1),
            jnp.concatenate([a_top, att_d[hb:]], axis=1)], axis=0)
        o = o + jnp.dot(att.astype(BF16), v, preferred_element_type=F32)
        ms = jnp.mean(o * o, axis=-1, keepdims=True)
        y = (o * lax.rsqrt(ms + EPS)) * nw
        o_ref[rs, sl] = (y * gate_ref[rs, sl].astype(F32)).astype(BF16)


def _attn_blocks(sink_ref, q_ref, kc_ref, kp_ref, vc_ref, vp_ref, o_ref, n):
    blk = WINDOW
    nt_dims = (((1,), (1,)), ((), ()))
    k_all = jnp.concatenate([kp_ref[...], kc_ref[...]], axis=0)
    v_all = jnp.concatenate([vp_ref[...], vc_ref[...]], axis=0)
    k_sw = pltpu.roll(k_all.astype(F32), HEAD_DIM, 1).astype(BF16)
    v_sw = pltpu.roll(v_all.astype(F32), HEAD_DIM, 1).astype(BF16)
    lane = lax.broadcasted_iota(jnp.int32, (1, LANES), 1)
    m_lo = jnp.where(lane < HEAD_DIM, 1.0, 0.0).astype(BF16)
    m_hi = jnp.where(lane < HEAD_DIM, 0.0, 1.0).astype(BF16)

    qi = lax.broadcasted_iota(jnp.int32, (2 * blk, blk), 0) & (blk - 1)
    kj = lax.broadcasted_iota(jnp.int32, (2 * blk, blk), 1)
    own = kj <= qi
    top = lax.broadcasted_iota(jnp.int32, (2 * blk, 1), 0) < blk
    first_bias = jnp.where(n > 0, 0.0, -jnp.inf).astype(F32)

    for kh in range(KV_HEADS):
        own_k, other_k = (k_all, k_sw) if kh == 0 else (k_sw, k_all)
        own_v, other_v = (v_all, v_sw) if kh == 0 else (v_sw, v_all)
        kv_par = ((own_k * m_lo, own_v * m_lo), (other_k * m_hi, other_v * m_hi))
        base = kh * GROUP * HEAD_DIM
        ob = HGRN_WIDTH + base
        for j in range(MIX_TM // WINDOW):
            rows = slice(j * blk, (j + 1) * blk)
            keys = slice(j * blk, (j + 2) * blk)
            qs = jnp.concatenate([q_ref[rows, base:base + LANES], q_ref[rows, base + LANES:base + 2 * LANES]], axis=0)
            o = None
            for par, (kk, vv) in enumerate(kv_par):
                s = lax.dot_general(qs, kk[keys], nt_dims, preferred_element_type=F32)
                s_prev = s[:, :blk] + first_bias if j == 0 else s[:, :blk]
                sc = jnp.where(own, s[:, blk:], s_prev)
                sink = jnp.where(top, sink_ref[kh * GROUP + par], sink_ref[kh * GROUP + 2 + par]) * LOG2E
                mx = jnp.maximum(jnp.max(sc, axis=-1, keepdims=True), sink)
                p = jnp.exp2(sc - mx)
                den = jnp.sum(p, axis=-1, keepdims=True) + jnp.exp2(sink - mx)
                p2 = jnp.concatenate([jnp.where(own, 0.0, p), jnp.where(own, p, 0.0)], axis=1).astype(BF16)
                part = jnp.dot(p2, vv[keys], preferred_element_type=F32) / den
                o = part if o is None else o + part
            o_ref[rows, ob:ob + LANES] = o[:blk].astype(BF16)
            o_ref[rows, ob + LANES:ob + 2 * LANES] = o[blk:].astype(BF16)


def _mixer_kernel(sink_ref, nw_ref, hq_ref, g2_ref, lk_ref, hi_ref, gate_ref, tri_ref, code_ref, exp_ref,
                  aq_ref, kc_ref, kp_ref, vc_ref, vp_ref, o_ref, st_ref):
    n = pl.program_id(1)

    @pl.when(n == 0)
    def _():
        st_ref[...] = jnp.zeros(st_ref.shape, F32)

    for ch in range(MIX_TM // HGRN_CHUNK):
        rs = slice(ch * HGRN_CHUNK, (ch + 1) * HGRN_CHUNK)
        _hgrn_chunk(nw_ref, hq_ref, g2_ref, lk_ref, hi_ref, gate_ref, tri_ref, code_ref, exp_ref, o_ref, st_ref, rs)
    _attn_blocks(sink_ref, aq_ref, kc_ref, kp_ref, vc_ref, vp_ref, o_ref, n)


def _mixer(sinks_l, norm_w, hq, g2, lk, hi, gate, aq, ak, av):
    bsz, t, w = hq.shape
    c = HGRN_CHUNK
    tm = MIX_TM
    blk = WINDOW
    assert c == 2 * HGRN_HALF and HGRN_HALF == LANES and tm % c == 0 and tm % blk == 0
    tri = jnp.asarray(np.tril(np.ones((c, c))), BF16)
    code = jnp.asarray(_hgrn_code_table())
    expand_np = (np.arange(HGRN_DIM)[:, None] == (np.arange(HGRN_HALF)[None, :] % HGRN_DIAG)).astype(np.float32)
    expand = jnp.asarray(expand_np, BF16)
    cur = lambda width: pl.BlockSpec((None, tm, width), lambda b, i: (b, i, 0))
    prev = lambda width: pl.BlockSpec((None, blk, width), lambda b, i: (b, jnp.maximum(i * (tm // blk) - 1, 0), 0))
    return pl.pallas_call(
        _mixer_kernel,
        grid=(bsz, t // tm),
        in_specs=[
            pl.BlockSpec(memory_space=pltpu.SMEM),
            _resident((1, HGRN_DIM)),
            cur(w), cur(w), cur(w), cur(w), cur(w),
            _resident((c, c)),
            _resident((c, HGRN_HALF)),
            _resident((HGRN_DIM, HGRN_HALF)),
            cur(ATTN_WIDTH), cur(KV_WIDTH), prev(KV_WIDTH), cur(KV_WIDTH), prev(KV_WIDTH),
        ],
        out_specs=cur(MIX_WIDTH),
        out_shape=jax.ShapeDtypeStruct((bsz, t, MIX_WIDTH), BF16),
        scratch_shapes=[pltpu.VMEM((HGRN_HEADS, HGRN_DIM, HGRN_DIM), F32)],
        compiler_params=_params(("arbitrary", "arbitrary")),
        name="mixer",
    )(sinks_l, norm_w.reshape(1, HGRN_DIM), hq, g2, lk, hi, gate, tri, code, expand, aq, ak, ak, av, av)


def _ffn_kernel(*refs, rows, tm, sub_rows, with_mix):
    if with_mix:
        x_ref, mod_ref, nw_ref, wg_ref, wu_ref, wd_ref, om_ref, wo_ref, o_ref, a_scr = refs
    else:
        x_ref, mod_ref, nw_ref, wg_ref, wu_ref, wd_ref, o_ref, a_scr = refs
    m = mod_ref[...]
    sh, sc, gate = (m[r:r + 1] for r in rows)
    for sub in range(tm // sub_rows):
        rs = slice(sub * sub_rows, (sub + 1) * sub_rows)
        x = x_ref[rs, :]
        if with_mix:
            x = x + m[5:6] * jnp.dot(om_ref[rs, :], wo_ref[...], preferred_element_type=F32)
        h = _rms_mod(x, nw_ref[...], sc, sh).astype(BF16)
        lo = 0
        for width in FFN_CHUNKS:
            gt = jnp.dot(h, wg_ref[:, lo:lo + width], preferred_element_type=F32)
            ut = jnp.dot(h, wu_ref[:, lo:lo + width], preferred_element_type=F32)
            a_scr[rs, lo:lo + width] = (_silu(gt) * ut).astype(BF16)
            lo += width
        y = jnp.dot(a_scr[rs, :], wd_ref[...], preferred_element_type=F32)
        o_ref[rs, :] = x + (0.5 * gate) * y


def _ffn(x, mod, layer, nw, wg, wu, wd, rows, tm, sub_rows, mix=None):
    bsz, t, d = x.shape
    dff = wg.shape[-1]
    assert sum(FFN_CHUNKS) == dff and tm % sub_rows == 0
    tok = lambda w: pl.BlockSpec((None, tm, w), lambda b, i: (b, i, 0))
    in_specs = [
        tok(d),
        _mod_spec(mod, layer),
        _resident((1, d)),
        _layer_resident(wg, layer),
        _layer_resident(wu, layer),
        _layer_resident(wd, layer),
    ]
    args = [x, mod, nw.reshape(1, d), wg, wu, wd]
    if mix is not None:
        o_mix, w_out = mix
        in_specs += [tok(o_mix.shape[-1]), _layer_resident(w_out, layer)]
        args += [o_mix, w_out]
    return pl.pallas_call(
        functools.partial(_ffn_kernel, rows=rows, tm=tm, sub_rows=sub_rows, with_mix=mix is not None),
        grid=(bsz, t // tm),
        in_specs=in_specs,
        out_specs=tok(d),
        out_shape=jax.ShapeDtypeStruct((bsz, t, d), F32),
        scratch_shapes=[pltpu.VMEM((tm, dff), BF16)],
        compiler_params=_params(("arbitrary", "arbitrary")),
        name="ffn_mix" if mix is not None else "ffn",
    )(*args)


def kernel(x, c, positions, ada_w, ada_b, norm_ffn1, ffn1_w_gate, ffn1_w_up, ffn1_w_down, norm_mix, w_in,
           lb_logits, hgrn_norm, q_norm, k_norm, sinks, w_out, norm_ffn2, ffn2_w_gate, ffn2_w_up, ffn2_w_down):
    depth = ada_w.shape[0]
    mod = _modulation(c, ada_w, ada_b)
    cos_t, sin_t = _rope_tables(positions)
    bf = lambda w: w.astype(BF16)
    wg1, wu1, wd1 = bf(ffn1_w_gate), bf(ffn1_w_up), bf(ffn1_w_down)
    wg2, wu2, wd2 = bf(ffn2_w_gate), bf(ffn2_w_up), bf(ffn2_w_down)
    w_in_b, w_out_b = bf(w_in), bf(w_out)
    for l in range(depth):
        x = _ffn(x, mod, l, norm_ffn1[l], wg1, wu1, wd1, rows=(0, 1, 2), tm=FFN_TM, sub_rows=FFN_SUB)
        hq, g2, lk, hi, gate, aq, ak, av = _project(x, mod, l, norm_mix[l], w_in_b, q_norm[l], k_norm[l], cos_t, sin_t,
                                                    lb_logits, tm=PROJ_TM, sub_rows=PROJ_SUB)
        o_mix = _mixer(sinks[l], hgrn_norm[l], hq, g2, lk, hi, gate, aq, ak, av)
        x = _ffn(x, mod, l, norm_ffn2[l], wg2, wu2, wd2, rows=(6, 7, 8), tm=FFN_TM, sub_rows=FFN_MIX_SUB,
                 mix=(o_mix, w_out_b))
    return x
```

```python
import functools

import numpy as np
import jax
import jax.numpy as jnp
from jax import lax
from jax.experimental import pallas as pl
from jax.experimental.pallas import tpu as pltpu

F32 = jnp.float32
BF16 = jnp.bfloat16

HGRN_HEADS = 4
HGRN_DIM = 128
HGRN_WIDTH = HGRN_HEADS * HGRN_DIM
ATTN_HEADS = 8
KV_HEADS = 2
HEAD_DIM = 64
ATTN_WIDTH = ATTN_HEADS * HEAD_DIM
KV_WIDTH = KV_HEADS * HEAD_DIM
MIX_WIDTH = HGRN_WIDTH + ATTN_WIDTH
GROUP = ATTN_HEADS // KV_HEADS
WINDOW = 128
ROPE_THETA = 500000.0
ROPE_DIM = HEAD_DIM // 4
N_MOD = 9
EPS = 1e-6

LANES = 128
VMEM_LIMIT_BYTES = 56 * 1024 * 1024

MOD_TN = 2304
CAST_BANDS = 16
ROPE_TM = 1024
FFN_TM = 1024
FFN_SUB = 256
FFN_MIX_SUB = 512
PROJ_TM = 1024
PROJ_SUB = 256
FFN_CHUNKS = (1536, 1280)
HGRN_CHUNK = 256
HGRN_DIAG = 8
MIX_TM = 1024
HGRN_HALF = 128
LOG2E = 1.4426950408889634


def _params(sem):
    return pltpu.CompilerParams(dimension_semantics=sem, vmem_limit_bytes=VMEM_LIMIT_BYTES)


def _resident(shape):
    nd = len(shape)
    return pl.BlockSpec(shape, lambda *_: (0,) * nd, pipeline_mode=pl.Buffered(1))


def _cast_specs(arr, l, steps_inner, bands):
    _, r, c = arr.shape
    assert r % bands == 0 and (r // bands) % 16 == 0
    rb = r // bands
    return (pl.BlockSpec((None, rb, c), lambda b, i: (l, b * steps_inner + i, 0)),
            pl.BlockSpec((rb, c), lambda b, i: (b * steps_inner + i, 0)),
            jax.ShapeDtypeStruct((r, c), BF16))


def _mod_spec(mod, l):
    return pl.BlockSpec((None, None) + mod.shape[2:], lambda b, i: (l, b, 0, 0))


def _rms_mod(x, nw, sc, sh):
    ms = jnp.mean(x * x, axis=-1, keepdims=True)
    return (x * lax.rsqrt(ms + EPS)) * (nw * (1.0 + sc)) + sh


def _silu(x):
    return x * jax.nn.sigmoid(x)


def _cast_kernel(*refs):
    n = len(refs) // 2
    for src_ref, dst_ref in zip(refs[:n], refs[n:]):
        dst_ref[...] = src_ref[...].astype(BF16)


def _cast_layer(arrs, l):
    specs = [_cast_specs(a, l, CAST_BANDS, CAST_BANDS) for a in arrs]
    return pl.pallas_call(
        _cast_kernel,
        grid=(1, CAST_BANDS),
        in_specs=[s[0] for s in specs],
        out_specs=[s[1] for s in specs],
        out_shape=[s[2] for s in specs],
        compiler_params=_params(("arbitrary", "arbitrary")),
        name="cast_weights",
    )(*arrs)


def _mod_kernel(ct_ref, w_ref, b_ref, o_ref):
    ca = _silu(ct_ref[...])
    w = w_ref[...]
    rows = [jnp.sum(w * ca[:, b:b + 1], axis=0, keepdims=True) for b in range(ca.shape[1])]
    o_ref[...] = jnp.concatenate(rows, axis=0) + b_ref[...]


def _modulation(c, ada_w, ada_b):
    depth, d, n = ada_w.shape
    bsz = c.shape[0]
    out = pl.pallas_call(
        _mod_kernel,
        grid=(depth, n // MOD_TN),
        in_specs=[
            pl.BlockSpec((d, bsz), lambda l, j: (0, 0)),
            pl.BlockSpec((None, d, MOD_TN), lambda l, j: (l, 0, j)),
            pl.BlockSpec((None, 1, MOD_TN), lambda l, j: (l, 0, j)),
        ],
        out_specs=pl.BlockSpec((None, bsz, MOD_TN), lambda l, j: (l, 0, j)),
        out_shape=jax.ShapeDtypeStruct((depth, bsz, n), F32),
        compiler_params=_params(("arbitrary", "arbitrary")),
        name="adaln_mod",
    )(c.T, ada_w, ada_b.reshape(depth, 1, n))
    return out.reshape(depth, bsz, N_MOD, d)


def _rope_kernel(pos_ref, invf_ref, ec_ref, es_ref, base_ref, cos_ref, sin_ref):
    tn_dims = (((0,), (0,)), ((), ()))
    ang = invf_ref[...] * pos_ref[...].astype(F32)

    def expand(vals, e_ref):
        pad = jnp.zeros(vals.shape, BF16)
        stacked = jnp.concatenate(list(_split3(vals)) + [pad], axis=0)
        return lax.dot_general(stacked, e_ref[...], tn_dims, preferred_element_type=F32)

    cos_ref[...] = expand(jnp.cos(ang), ec_ref) + base_ref[...]
    sin_ref[...] = expand(jnp.sin(ang), es_ref)


def _rope_tables(positions):
    bsz, t = positions.shape
    nf = ROPE_DIM // 2
    inv_freq = ROPE_THETA ** (-jnp.arange(0, ROPE_DIM, 2, dtype=F32) / ROPE_DIM)
    lane = np.arange(2 * HEAD_DIM) % HEAD_DIM
    rot = lane < ROPE_DIM
    pick = (np.arange(nf)[:, None] == (lane % nf)[None, :]) & rot[None, :]
    e_cos = np.tile(pick.astype(np.float32), (4, 1))
    e_cos[3 * nf:] = 0.0
    e_sin = e_cos * np.where(lane < nf, -1.0, 1.0)[None, :]
    base = (~rot).astype(np.float32).reshape(1, 2 * HEAD_DIM)
    shp = jax.ShapeDtypeStruct((bsz, t, 2 * HEAD_DIM), F32)
    const = lambda shape: pl.BlockSpec(shape, lambda b, i: (0, 0))
    return pl.pallas_call(
        _rope_kernel,
        grid=(bsz, t // ROPE_TM),
        in_specs=[
            pl.BlockSpec((None, 1, ROPE_TM), lambda b, i: (b, 0, i)),
            const((nf, 1)),
            const((4 * nf, 2 * HEAD_DIM)),
            const((4 * nf, 2 * HEAD_DIM)),
            const((1, 2 * HEAD_DIM)),
        ],
        out_specs=[pl.BlockSpec((None, ROPE_TM, 2 * HEAD_DIM), lambda b, i: (b, i, 0))] * 2,
        out_shape=[shp, shp],
        compiler_params=_params(("arbitrary", "arbitrary")),
        name="rope_tables",
    )(positions.reshape(bsz, 1, t), inv_freq.reshape(nf, 1), jnp.asarray(e_cos, BF16), jnp.asarray(e_sin, BF16),
      jnp.asarray(base))


def _head_norm_rope(t, nw, cos, sin, swap_lo):
    sq = t * t
    first = lax.broadcasted_iota(jnp.int32, t.shape, 1) < HEAD_DIM
    s_lo = jnp.sum(jnp.where(first, sq, 0.0), axis=-1, keepdims=True)
    s_all = jnp.sum(sq, axis=-1, keepdims=True)
    ms = jnp.where(first, s_lo, s_all - s_lo) * (1.0 / HEAD_DIM)
    y = (t * lax.rsqrt(ms + EPS)) * nw
    swapped = jnp.where(swap_lo, pltpu.roll(y, LANES - ROPE_DIM // 2, 1), pltpu.roll(y, ROPE_DIM // 2, 1))
    return y * cos + swapped * sin


def _lower_bound_logs(lbl, layer):
    e = jnp.exp(lbl - jnp.max(lbl, axis=0, keepdims=True))
    sm = e / jnp.sum(e, axis=0, keepdims=True)
    cum0 = sm[0:1]
    cum = cum0
    for j in range(1, layer + 1):
        cum = cum + sm[j:j + 1]
    lb = cum - cum0
    return jnp.log(lb), jnp.log1p(-lb)


def _proj_rows(x, m, nw, w_ref, qn, kn, cos, sin, lbl_ref, outs, rs, layer):
    hq_ref, g2_ref, lk_ref, hi_ref, hg_ref, aq_ref, ak_ref, av_ref = outs
    h = _rms_mod(x, nw, m[4:5], m[3:4]).astype(BF16)

    proj = jnp.dot(h, w_ref[...], preferred_element_type=F32)

    def seg_dot(lo, width):
        return proj[:, lo:lo + width]

    hq_ref[rs, :] = (seg_dot(0, HGRN_WIDTH) * (HGRN_DIM ** -0.5)).astype(BF16)

    fl2 = seg_dot(HGRN_WIDTH, HGRN_WIDTH) * LOG2E
    sp2 = jnp.log2(1.0 + jnp.exp2(-jnp.abs(fl2)))
    ls2 = jnp.minimum(fl2, 0.0) - sp2
    lsn2 = -(jnp.maximum(fl2, 0.0) + sp2)
    if layer == 0:
        g2_ref[rs, :] = ls2
        lk_ref[rs, :] = lsn2
    else:
        log_lb, l1m = _lower_bound_logs(lbl_ref[...], layer)
        log_lb2 = log_lb * LOG2E
        l1m2 = l1m * LOG2E
        cc2 = l1m2 + ls2
        lk_ref[rs, :] = l1m2 + lsn2
        dlt = log_lb2 - cc2
        g2_ref[rs, :] = jnp.maximum(log_lb2, cc2) + jnp.log2(1.0 + jnp.exp2(-jnp.abs(dlt)))

    hi_ref[rs, :] = seg_dot(2 * HGRN_WIDTH, HGRN_WIDTH).astype(BF16)
    hg_ref[rs, :] = _silu(seg_dot(3 * HGRN_WIDTH, HGRN_WIDTH)).astype(BF16)
    base = 4 * HGRN_WIDTH
    aq = seg_dot(base, ATTN_WIDTH)
    ak = seg_dot(base + ATTN_WIDTH, KV_WIDTH)
    av_ref[rs, :] = seg_dot(base + ATTN_WIDTH + KV_WIDTH, KV_WIDTH).astype(BF16)

    lane = lax.broadcasted_iota(jnp.int32, cos.shape, 1)
    swap_lo = (lane & (HEAD_DIM - 1)) < ROPE_DIM // 2
    for j in range(ATTN_WIDTH // LANES):
        sl = slice(j * LANES, (j + 1) * LANES)
        aq_ref[rs, sl] = _head_norm_rope(aq[:, sl], qn, cos, sin, swap_lo).astype(BF16)
    for j in range(KV_WIDTH // LANES):
        sl = slice(j * LANES, (j + 1) * LANES)
        ak_ref[rs, sl] = _head_norm_rope(ak[:, sl], kn, cos, sin, swap_lo).astype(BF16)


def _proj_kernel(x_ref, mod_ref, pnw_ref, win_ref, qn_ref, kn_ref, cos_ref, sin_ref, lbl_ref, *outs,
                 tm, sub_rows, layer):
    m = mod_ref[...]
    for sub in range(tm // sub_rows):
        rs = slice(sub * sub_rows, (sub + 1) * sub_rows)
        _proj_rows(x_ref[rs, :], m, pnw_ref[...], win_ref, qn_ref[...], kn_ref[...], cos_ref[rs, :], sin_ref[rs, :],
                   lbl_ref, outs, rs, layer)


def _project(x, mod, layer, pnw, w_in, q_norm, k_norm, cos_t, sin_t, lb_logits, tm, sub_rows):
    bsz, t, d = x.shape
    tok = lambda w: pl.BlockSpec((None, tm, w), lambda b, i: (b, i, 0))
    sds = lambda w, dt: jax.ShapeDtypeStruct((bsz, t, w), dt)
    qn = jnp.tile(q_norm * (LOG2E * HEAD_DIM ** -0.5), LANES // HEAD_DIM).reshape(1, LANES)
    kn = jnp.tile(k_norm, LANES // HEAD_DIM).reshape(1, LANES)
    return pl.pallas_call(
        functools.partial(_proj_kernel, tm=tm, sub_rows=sub_rows, layer=layer),
        grid=(bsz, t // tm),
        in_specs=[tok(d), _mod_spec(mod, layer), _resident((1, d)), _resident(w_in.shape),
                  _resident((1, LANES)), _resident((1, LANES)), tok(LANES), tok(LANES), _resident(lb_logits.shape)],
        out_specs=[tok(HGRN_WIDTH)] * 5 + [tok(ATTN_WIDTH), tok(KV_WIDTH), tok(KV_WIDTH)],
        out_shape=[sds(HGRN_WIDTH, BF16), sds(HGRN_WIDTH, F32), sds(HGRN_WIDTH, F32), sds(HGRN_WIDTH, BF16),
                   sds(HGRN_WIDTH, BF16), sds(ATTN_WIDTH, BF16), sds(KV_WIDTH, BF16), sds(KV_WIDTH, BF16)],
        compiler_params=_params(("arbitrary", "arbitrary")),
        name="in_proj",
    )(x, mod, pnw.reshape(1, d), w_in, qn, kn, cos_t, sin_t, lb_logits)


def _hgrn_inner_levels():
    lv = []
    m = HGRN_DIAG
    while m < HGRN_HALF:
        lv.append(m)
        m *= 2
    return tuple(lv)


def _hgrn_code_table():
    n = HGRN_HALF
    t = np.arange(n)[:, None]
    s = np.arange(n)[None, :]
    x = t ^ s
    code = np.full((n, n), -2, np.int32)
    for idx, m in enumerate(_hgrn_inner_levels()):
        code[(x >= m) & (x < 2 * m) & (t > s)] = idx
    code[(t // HGRN_DIAG == s // HGRN_DIAG) & (t >= s)] = -1
    return np.tile(code, (HGRN_CHUNK // HGRN_HALF, 1))


def _split3(g):
    hi = g.astype(BF16)
    r1 = g - hi.astype(F32)
    mid = r1.astype(BF16)
    lo = (r1 - mid.astype(F32)).astype(BF16)
    return hi, mid, lo


def _hgrn_chunk(nw_ref, q_ref, g2_ref, lk_ref, i_ref, gate_ref, tri_ref, code_ref, exp_ref, o_ref, st_ref, rs):
    c = HGRN_CHUNK
    dg = HGRN_DIAG
    hb = HGRN_HALF
    dh = HGRN_DIM
    nt_dims = (((1,), (1,)), ((), ()))
    tn_dims = (((0,), (0,)), ((), ()))

    tri = tri_ref[...]
    b_all = sum(jnp.dot(tri, p, preferred_element_type=F32) for p in _split3(g2_ref[rs, :]))

    code = code_ref[...]
    expand = exp_ref[...]
    nw = nw_ref[...]
    nb = c // dg
    lane3 = lax.broadcasted_iota(jnp.int32, (1, dg, dh), 2)
    row3 = lax.broadcasted_iota(jnp.int32, (1, dg, dh), 1)
    diag_code = jnp.where(lane3 <= row3, lane3, -1)

    for hd in range(HGRN_HEADS):
        sl = slice(hd * dh, (hd + 1) * dh)
        q = q_ref[rs, sl].astype(F32)
        v = i_ref[rs, sl]
        b = b_all[:, sl]
        lk = lk_ref[rs, sl]
        b_last = b[c - 1:c, :]

        st = st_ref[hd]
        qq0 = (q * jnp.exp2(b)).astype(BF16)
        o = lax.dot_general(qq0, st.astype(BF16), nt_dims, preferred_element_type=F32)
        kk_end = jnp.exp2(lk + (b_last - b)).astype(BF16)
        st_ref[hd] = st * jnp.exp2(b_last) + lax.dot_general(v, kk_end, tn_dims, preferred_element_type=F32)

        att_d = None
        for idx, m in reversed(list(enumerate(_hgrn_inner_levels()))):
            shp = (c // (2 * m), 2 * m, dh)
            b3 = b.reshape(shp)
            mid = b3[:, m - 1:m, :]
            zero = jnp.zeros((shp[0], m, dh), F32)
            q_hi = q.reshape(shp)[:, m:, :] * jnp.exp2(b3[:, m:, :] - mid)
            k_lo = jnp.exp2(lk.reshape(shp)[:, :m, :] + (mid - b3[:, :m, :]))
            qm = jnp.concatenate([zero, q_hi], axis=1).reshape(c, dh).astype(BF16)
            km = jnp.concatenate([k_lo, zero], axis=1).reshape(c, dh).astype(BF16)
            a = lax.dot_general(qm, km, nt_dims, preferred_element_type=F32)
            a_d = jnp.concatenate([a[:hb, :hb], a[hb:, hb:]], axis=0)
            att_d = a_d if att_d is None else jnp.where(code == idx, a_d, att_d)

        mid = b[hb - 1:hb, :]
        q_top = (q[hb:] * jnp.exp2(b[hb:] - mid)).astype(BF16)
        k_top = jnp.exp2(lk[:hb] + (mid - b[:hb])).astype(BF16)
        a_top = lax.dot_general(q_top, k_top, nt_dims, preferred_element_type=F32)

        q3 = q.reshape(nb, dg, dh)
        b3 = b.reshape(nb, dg, dh)
        bk3 = (b - lk).reshape(nb, dg, dh)
        compact = jnp.zeros((nb, dg, dh), F32)
        for s_loc in range(dg):
            z = q3 * jnp.exp2(b3 - bk3[:, s_loc:s_loc + 1, :])
            col = jnp.sum(z, axis=-1, keepdims=True)
            compact = jnp.where(diag_code == s_loc, col, compact)
        compact = compact.reshape(c, dh)
        a_diag = jnp.dot(compact.astype(BF16), expand, preferred_element_type=F32)
        att_d = jnp.where(code == -1, a_diag, att_d)

        att = jnp.concatenate([
            jnp.concatenate([att_d[:hb], jnp.zeros((hb, hb), F32)], axis=1),
            jnp.concatenate([a_top, att_d[hb:]], axis=1)], axis=0)
        o = o + jnp.dot(att.astype(BF16), v, preferred_element_type=F32)
        ms = jnp.mean(o * o, axis=-1, keepdims=True)
        y = (o * lax.rsqrt(ms + EPS)) * nw
        o_ref[rs, sl] = (y * gate_ref[rs, sl].astype(F32)).astype(BF16)


def _attn_blocks(sink_ref, q_ref, kc_ref, kp_ref, vc_ref, vp_ref, o_ref, n):
    blk = WINDOW
    nt_dims = (((1,), (1,)), ((), ()))
    k_all = jnp.concatenate([kp_ref[...], kc_ref[...]], axis=0)
    v_all = jnp.concatenate([vp_ref[...], vc_ref[...]], axis=0)
    k_sw = pltpu.roll(k_all.astype(F32), HEAD_DIM, 1).astype(BF16)
    v_sw = pltpu.roll(v_all.astype(F32), HEAD_DIM, 1).astype(BF16)
    lane = lax.broadcasted_iota(jnp.int32, (1, LANES), 1)
    m_lo = jnp.where(lane < HEAD_DIM, 1.0, 0.0).astype(BF16)
    m_hi = jnp.where(lane < HEAD_DIM, 0.0, 1.0).astype(BF16)

    qi = lax.broadcasted_iota(jnp.int32, (2 * blk, blk), 0) & (blk - 1)
    kj = lax.broadcasted_iota(jnp.int32, (2 * blk, blk), 1)
    own = kj <= qi
    top = lax.broadcasted_iota(jnp.int32, (2 * blk, 1), 0) < blk
    first_bias = jnp.where(n > 0, 0.0, -jnp.inf).astype(F32)

    for kh in range(KV_HEADS):
        own_k, other_k = (k_all, k_sw) if kh == 0 else (k_sw, k_all)
        own_v, other_v = (v_all, v_sw) if kh == 0 else (v_sw, v_all)
        kv_par = ((own_k * m_lo, own_v * m_lo), (other_k * m_hi, other_v * m_hi))
        base = kh * GROUP * HEAD_DIM
        ob = HGRN_WIDTH + base
        for j in range(MIX_TM // WINDOW):
            rows = slice(j * blk, (j + 1) * blk)
            keys = slice(j * blk, (j + 2) * blk)
            qs = jnp.concatenate([q_ref[rows, base:base + LANES], q_ref[rows, base + LANES:base + 2 * LANES]], axis=0)
            o = None
            for par, (kk, vv) in enumerate(kv_par):
                s = lax.dot_general(qs, kk[keys], nt_dims, preferred_element_type=F32)
                s_prev = s[:, :blk] + first_bias if j == 0 else s[:, :blk]
                sc = jnp.where(own, s[:, blk:], s_prev)
                sink = jnp.where(top, sink_ref[kh * GROUP + par], sink_ref[kh * GROUP + 2 + par]) * LOG2E
                mx = jnp.maximum(jnp.max(sc, axis=-1, keepdims=True), sink)
                p = jnp.exp2(sc - mx)
                den = jnp.sum(p, axis=-1, keepdims=True) + jnp.exp2(sink - mx)
                p2 = jnp.concatenate([jnp.where(own, 0.0, p), jnp.where(own, p, 0.0)], axis=1).astype(BF16)
                part = jnp.dot(p2, vv[keys], preferred_element_type=F32) / den
                o = part if o is None else o + part
            o_ref[rows, ob:ob + LANES] = o[:blk].astype(BF16)
            o_ref[rows, ob + LANES:ob + 2 * LANES] = o[blk:].astype(BF16)


def _mixer_kernel(sink_ref, nw_ref, hq_ref, g2_ref, lk_ref, hi_ref, gate_ref, tri_ref, code_ref, exp_ref,
                  aq_ref, kc_ref, kp_ref, vc_ref, vp_ref, o_ref, st_ref):
    n = pl.program_id(1)

    @pl.when(n == 0)
    def _():
        st_ref[...] = jnp.zeros(st_ref.shape, F32)

    for ch in range(MIX_TM // HGRN_CHUNK):
        rs = slice(ch * HGRN_CHUNK, (ch + 1) * HGRN_CHUNK)
        _hgrn_chunk(nw_ref, hq_ref, g2_ref, lk_ref, hi_ref, gate_ref, tri_ref, code_ref, exp_ref, o_ref, st_ref, rs)
    _attn_blocks(sink_ref, aq_ref, kc_ref, kp_ref, vc_ref, vp_ref, o_ref, n)


def _mixer(sinks_l, norm_w, hq, g2, lk, hi, gate, aq, ak, av):
    bsz, t, w = hq.shape
    c = HGRN_CHUNK
    tm = MIX_TM
    blk = WINDOW
    assert c == 2 * HGRN_HALF and HGRN_HALF == LANES and tm % c == 0 and tm % blk == 0
    tri = jnp.asarray(np.tril(np.ones((c, c))), BF16)
    code = jnp.asarray(_hgrn_code_table())
    expand_np = (np.arange(HGRN_DIM)[:, None] == (np.arange(HGRN_HALF)[None, :] % HGRN_DIAG)).astype(np.float32)
    expand = jnp.asarray(expand_np, BF16)
    cur = lambda width: pl.BlockSpec((None, tm, width), lambda b, i: (b, i, 0))
    prev = lambda width: pl.BlockSpec((None, blk, width), lambda b, i: (b, jnp.maximum(i * (tm // blk) - 1, 0), 0))
    return pl.pallas_call(
        _mixer_kernel,
        grid=(bsz, t // tm),
        in_specs=[
            pl.BlockSpec(memory_space=pltpu.SMEM),
            _resident((1, HGRN_DIM)),
            cur(w), cur(w), cur(w), cur(w), cur(w),
            _resident((c, c)),
            _resident((c, HGRN_HALF)),
            _resident((HGRN_DIM, HGRN_HALF)),
            cur(ATTN_WIDTH), cur(KV_WIDTH), prev(KV_WIDTH), cur(KV_WIDTH), prev(KV_WIDTH),
        ],
        out_specs=cur(MIX_WIDTH),
        out_shape=jax.ShapeDtypeStruct((bsz, t, MIX_WIDTH), BF16),
        scratch_shapes=[pltpu.VMEM((HGRN_HEADS, HGRN_DIM, HGRN_DIM), F32)],
        compiler_params=_params(("arbitrary", "arbitrary")),
        name="mixer",
    )(sinks_l, norm_w.reshape(1, HGRN_DIM), hq, g2, lk, hi, gate, tri, code, expand, aq, ak, ak, av, av)


def _ffn_kernel(*refs, rows, tm, sub_rows, with_mix, n_cast):
    refs = list(refs)
    a_scr = refs.pop()
    cast_dst = [refs.pop() for _ in range(n_cast)][::-1]
    o_ref = refs.pop()
    cast_src = [refs.pop() for _ in range(n_cast)][::-1]
    if with_mix:
        x_ref, mod_ref, nw_ref, wg_ref, wu_ref, wd_ref, om_ref, wo_ref = refs
    else:
        x_ref, mod_ref, nw_ref, wg_ref, wu_ref, wd_ref = refs
    m = mod_ref[...]
    sh, sc, gate = (m[r:r + 1] for r in rows)
    for sub in range(tm // sub_rows):
        rs = slice(sub * sub_rows, (sub + 1) * sub_rows)
        x = x_ref[rs, :]
        if with_mix:
            x = x + m[5:6] * jnp.dot(om_ref[rs, :], wo_ref[...], preferred_element_type=F32)
        h = _rms_mod(x, nw_ref[...], sc, sh).astype(BF16)
        lo = 0
        for width in FFN_CHUNKS:
            gt = jnp.dot(h, wg_ref[:, lo:lo + width], preferred_element_type=F32)
            ut = jnp.dot(h, wu_ref[:, lo:lo + width], preferred_element_type=F32)
            a_scr[rs, lo:lo + width] = (_silu(gt) * ut).astype(BF16)
            lo += width
        y = jnp.dot(a_scr[rs, :], wd_ref[...], preferred_element_type=F32)
        o_ref[rs, :] = x + (0.5 * gate) * y
    for src_ref, dst_ref in zip(cast_src, cast_dst):
        dst_ref[...] = src_ref[...].astype(BF16)


def _ffn(x, mod, layer, nw, wg, wu, wd, rows, tm, sub_rows, mix=None, casts=()):
    bsz, t, d = x.shape
    dff = wg.shape[-1]
    assert sum(FFN_CHUNKS) == dff and tm % sub_rows == 0
    steps = t // tm
    tok = lambda w: pl.BlockSpec((None, tm, w), lambda b, i: (b, i, 0))
    in_specs = [
        tok(d),
        _mod_spec(mod, layer),
        _resident((1, d)),
        _resident(wg.shape),
        _resident(wu.shape),
        _resident(wd.shape),
    ]
    args = [x, mod, nw.reshape(1, d), wg, wu, wd]
    if mix is not None:
        o_mix, w_out = mix
        in_specs += [tok(o_mix.shape[-1]), _resident(w_out.shape)]
        args += [o_mix, w_out]
    cast_specs = [_cast_specs(arr, l, steps, bsz * steps) for arr, l in casts]
    in_specs += [s[0] for s in cast_specs]
    args += [arr for arr, _ in casts]
    return pl.pallas_call(
        functools.partial(_ffn_kernel, rows=rows, tm=tm, sub_rows=sub_rows, with_mix=mix is not None,
                          n_cast=len(casts)),
        grid=(bsz, steps),
        in_specs=in_specs,
        out_specs=[tok(d)] + [s[1] for s in cast_specs],
        out_shape=[jax.ShapeDtypeStruct((bsz, t, d), F32)] + [s[2] for s in cast_specs],
        scratch_shapes=[pltpu.VMEM((tm, dff), BF16)],
        compiler_params=_params(("arbitrary", "arbitrary")),
        name="ffn_mix" if mix is not None else "ffn",
    )(*args)


def kernel(x, c, positions, ada_w, ada_b, norm_ffn1, ffn1_w_gate, ffn1_w_up, ffn1_w_down, norm_mix, w_in,
           lb_logits, hgrn_norm, q_norm, k_norm, sinks, w_out, norm_ffn2, ffn2_w_gate, ffn2_w_up, ffn2_w_down):
    depth = ada_w.shape[0]
    mod = _modulation(c, ada_w, ada_b)
    cos_t, sin_t = _rope_tables(positions)
    wg1, wu1, wd1 = _cast_layer((ffn1_w_gate, ffn1_w_up, ffn1_w_down), 0)
    for l in range(depth):
        x, w_in_l, wg2, wu2, wd2, w_out_l = _ffn(
            x, mod, l, norm_ffn1[l], wg1, wu1, wd1, rows=(0, 1, 2), tm=FFN_TM, sub_rows=FFN_SUB,
            casts=((w_in, l), (ffn2_w_gate, l), (ffn2_w_up, l), (ffn2_w_down, l), (w_out, l)))
        hq, g2, lk, hi, gate, aq, ak, av = _project(x, mod, l, norm_mix[l], w_in_l, q_norm[l], k_norm[l], cos_t, sin_t,
                                                    lb_logits, tm=PROJ_TM, sub_rows=PROJ_SUB)
        o_mix = _mixer(sinks[l], hgrn_norm[l], hq, g2, lk, hi, gate, aq, ak, av)
        nxt = ((ffn1_w_gate, l + 1), (ffn1_w_up, l + 1), (ffn1_w_down, l + 1)) if l + 1 < depth else ()
        x, *nxt_w = _ffn(x, mod, l, norm_ffn2[l], wg2, wu2, wd2, rows=(6, 7, 8), tm=FFN_TM, sub_rows=FFN_MIX_SUB,
                         mix=(o_mix, w_out_l), casts=nxt)
        if nxt_w:
            wg1, wu1, wd1 = nxt_w
    return x
```

```python
import functools

import numpy as np
import jax
import jax.numpy as jnp
from jax import lax
from jax.experimental import pallas as pl
from jax.experimental.pallas import tpu as pltpu

F32 = jnp.float32
BF16 = jnp.bfloat16

HGRN_HEADS = 4
HGRN_DIM = 128
HGRN_WIDTH = HGRN_HEADS * HGRN_DIM
ATTN_HEADS = 8
KV_HEADS = 2
HEAD_DIM = 64
ATTN_WIDTH = ATTN_HEADS * HEAD_DIM
KV_WIDTH = KV_HEADS * HEAD_DIM
MIX_WIDTH = HGRN_WIDTH + ATTN_WIDTH
GROUP = ATTN_HEADS // KV_HEADS
WINDOW = 128
ROPE_THETA = 500000.0
ROPE_DIM = HEAD_DIM // 4
N_MOD = 9
EPS = 1e-6

LANES = 128
VMEM_LIMIT_BYTES = 56 * 1024 * 1024

MOD_TN = 2304
ROPE_TM = 1024
FFN_TM = 1024
FFN_SUB = 256
FFN_MIX_SUB = 512
PROJ_TM = 1024
PROJ_SUB = 256
FFN_CHUNKS = (1536, 1280)
HGRN_CHUNK = 256
HGRN_DIAG = 8
MIX_TM = 1024
HGRN_HALF = 128
LOG2E = 1.4426950408889634


def _params(sem):
    return pltpu.CompilerParams(dimension_semantics=sem, vmem_limit_bytes=VMEM_LIMIT_BYTES)


def _resident(shape):
    nd = len(shape)
    return pl.BlockSpec(shape, lambda *_: (0,) * nd, pipeline_mode=pl.Buffered(1))


def _cast_specs(arr, l, steps_inner, bands):
    _, r, c = arr.shape
    assert r % bands == 0 and (r // bands) % 16 == 0
    rb = r // bands
    return (pl.BlockSpec((None, rb, c), lambda b, i: (l, b * steps_inner + i, 0)),
            pl.BlockSpec((rb, c), lambda b, i: (b * steps_inner + i, 0)),
            jax.ShapeDtypeStruct((r, c), BF16))


def _mod_spec(mod):
    return pl.BlockSpec((None,) + mod.shape[1:], lambda b, i: (b, 0, 0))


def _rms_mod(x, nw, sc, sh):
    ms = jnp.mean(x * x, axis=-1, keepdims=True)
    return (x * lax.rsqrt(ms + EPS)) * (nw * (1.0 + sc)) + sh


def _silu(x):
    return x * jax.nn.sigmoid(x)


def _cast_rope_kernel(pos_ref, invf_ref, ec_ref, es_ref, base_ref, *refs):
    n = (len(refs) - 2) // 2
    for src_ref, dst_ref in zip(refs[:n], refs[n:2 * n]):
        dst_ref[...] = src_ref[...].astype(BF16)
    _rope_rows(pos_ref, invf_ref, ec_ref, es_ref, base_ref, refs[2 * n], refs[2 * n + 1])


def _cast_layer_and_rope(arrs, l, positions):
    bsz, t = positions.shape
    steps = t // ROPE_TM
    bands = bsz * steps
    specs = [_cast_specs(a, l, steps, bands) for a in arrs]
    rope_in, rope_args, rope_out, rope_shape = _rope_specs(positions)
    return pl.pallas_call(
        _cast_rope_kernel,
        grid=(bsz, steps),
        in_specs=rope_in + [s[0] for s in specs],
        out_specs=[s[1] for s in specs] + rope_out,
        out_shape=[s[2] for s in specs] + rope_shape,
        compiler_params=_params(("arbitrary", "arbitrary")),
        name="cast_rope",
    )(*rope_args, *arrs)


def _mod_rows(ct_ref, w_ref, b_ref, o_ref):
    ca = _silu(ct_ref[...])
    w = w_ref[...]
    rows = [jnp.sum(w * ca[:, b:b + 1], axis=0, keepdims=True) for b in range(ca.shape[1])]
    o_ref[...] = jnp.concatenate(rows, axis=0) + b_ref[...]


def _mod_specs(c, ada_w, ada_b, l, tn, passes=1):
    depth, d, n = ada_w.shape
    bsz = c.shape[0]
    last = n // tn - 1
    band = lambda b, i: jnp.where(b == 0, i, last)
    in_specs = [
        pl.BlockSpec((d, bsz), lambda b, i: (0, 0)),
        pl.BlockSpec((None, d, tn), lambda b, i: (l, 0, band(b, i))),
        pl.BlockSpec((None, 1, tn), lambda b, i: (l, 0, band(b, i))),
    ]
    args = [c.T, ada_w, ada_b.reshape(depth, 1, n)]
    out_spec = pl.BlockSpec((bsz, tn), lambda b, i: (0, b * (last + 1) + i))
    return in_specs, args, out_spec, jax.ShapeDtypeStruct((bsz, passes * n), F32)


def _modulation(c, ada_w, ada_b, l):
    n = ada_w.shape[-1]
    in_specs, args, out_spec, out_shape = _mod_specs(c, ada_w, ada_b, l, MOD_TN)
    out = pl.pallas_call(
        _mod_rows,
        grid=(1, n // MOD_TN),
        in_specs=in_specs,
        out_specs=out_spec,
        out_shape=out_shape,
        compiler_params=_params(("arbitrary", "arbitrary")),
        name="adaln_mod",
    )(*args)
    return out.reshape(c.shape[0], N_MOD, -1)


def _rope_rows(pos_ref, invf_ref, ec_ref, es_ref, base_ref, cos_ref, sin_ref):
    tn_dims = (((0,), (0,)), ((), ()))
    ang = invf_ref[...] * pos_ref[...].astype(F32)

    def expand(vals, e_ref):
        pad = jnp.zeros(vals.shape, BF16)
        stacked = jnp.concatenate(list(_split3(vals)) + [pad], axis=0)
        return lax.dot_general(stacked, e_ref[...], tn_dims, preferred_element_type=F32)

    cos_ref[...] = expand(jnp.cos(ang), ec_ref) + base_ref[...]
    sin_ref[...] = expand(jnp.sin(ang), es_ref)


def _rope_specs(positions):
    bsz, t = positions.shape
    nf = ROPE_DIM // 2
    inv_freq = ROPE_THETA ** (-jnp.arange(0, ROPE_DIM, 2, dtype=F32) / ROPE_DIM)
    lane = np.arange(2 * HEAD_DIM) % HEAD_DIM
    rot = lane < ROPE_DIM
    pick = (np.arange(nf)[:, None] == (lane % nf)[None, :]) & rot[None, :]
    e_cos = np.tile(pick.astype(np.float32), (4, 1))
    e_cos[3 * nf:] = 0.0
    e_sin = e_cos * np.where(lane < nf, -1.0, 1.0)[None, :]
    base = (~rot).astype(np.float32).reshape(1, 2 * HEAD_DIM)
    shp = jax.ShapeDtypeStruct((bsz, t, 2 * HEAD_DIM), F32)
    const = lambda shape: pl.BlockSpec(shape, lambda b, i: (0, 0))
    in_specs = [
        pl.BlockSpec((None, 1, ROPE_TM), lambda b, i: (b, 0, i)),
        const((nf, 1)),
        const((4 * nf, 2 * HEAD_DIM)),
        const((4 * nf, 2 * HEAD_DIM)),
        const((1, 2 * HEAD_DIM)),
    ]
    args = [positions.reshape(bsz, 1, t), inv_freq.reshape(nf, 1), jnp.asarray(e_cos, BF16), jnp.asarray(e_sin, BF16),
            jnp.asarray(base)]
    out_specs = [pl.BlockSpec((None, ROPE_TM, 2 * HEAD_DIM), lambda b, i: (b, i, 0))] * 2
    return in_specs, args, out_specs, [shp, shp]


def _head_norm_rope(t, nw, cos, sin, swap_lo):
    sq = t * t
    first = lax.broadcasted_iota(jnp.int32, t.shape, 1) < HEAD_DIM
    s_lo = jnp.sum(jnp.where(first, sq, 0.0), axis=-1, keepdims=True)
    s_all = jnp.sum(sq, axis=-1, keepdims=True)
    ms = jnp.where(first, s_lo, s_all - s_lo) * (1.0 / HEAD_DIM)
    y = (t * lax.rsqrt(ms + EPS)) * nw
    swapped = jnp.where(swap_lo, pltpu.roll(y, LANES - ROPE_DIM // 2, 1), pltpu.roll(y, ROPE_DIM // 2, 1))
    return y * cos + swapped * sin


def _lower_bound_logs(lbl, layer):
    e = jnp.exp(lbl - jnp.max(lbl, axis=0, keepdims=True))
    sm = e / jnp.sum(e, axis=0, keepdims=True)
    cum0 = sm[0:1]
    cum = cum0
    for j in range(1, layer + 1):
        cum = cum + sm[j:j + 1]
    lb = cum - cum0
    return jnp.log(lb), jnp.log1p(-lb)


def _proj_rows(x, m, nw, w_ref, qn, kn, cos, sin, lbl_ref, outs, rs, layer):
    hq_ref, g2_ref, lk_ref, hi_ref, hg_ref, aq_ref, ak_ref, av_ref = outs
    h = _rms_mod(x, nw, m[4:5], m[3:4]).astype(BF16)

    proj = jnp.dot(h, w_ref[...], preferred_element_type=F32)

    def seg_dot(lo, width):
        return proj[:, lo:lo + width]

    hq_ref[rs, :] = (seg_dot(0, HGRN_WIDTH) * (HGRN_DIM ** -0.5)).astype(BF16)

    fl2 = seg_dot(HGRN_WIDTH, HGRN_WIDTH) * LOG2E
    sp2 = jnp.log2(1.0 + jnp.exp2(-jnp.abs(fl2)))
    ls2 = jnp.minimum(fl2, 0.0) - sp2
    lsn2 = -(jnp.maximum(fl2, 0.0) + sp2)
    if layer == 0:
        g2_ref[rs, :] = ls2
        lk_ref[rs, :] = lsn2
    else:
        log_lb, l1m = _lower_bound_logs(lbl_ref[...], layer)
        log_lb2 = log_lb * LOG2E
        l1m2 = l1m * LOG2E
        cc2 = l1m2 + ls2
        lk_ref[rs, :] = l1m2 + lsn2
        dlt = log_lb2 - cc2
        g2_ref[rs, :] = jnp.maximum(log_lb2, cc2) + jnp.log2(1.0 + jnp.exp2(-jnp.abs(dlt)))

    hi_ref[rs, :] = seg_dot(2 * HGRN_WIDTH, HGRN_WIDTH).astype(BF16)
    hg_ref[rs, :] = _silu(seg_dot(3 * HGRN_WIDTH, HGRN_WIDTH)).astype(BF16)
    base = 4 * HGRN_WIDTH
    aq = seg_dot(base, ATTN_WIDTH)
    ak = seg_dot(base + ATTN_WIDTH, KV_WIDTH)
    av_ref[rs, :] = seg_dot(base + ATTN_WIDTH + KV_WIDTH, KV_WIDTH).astype(BF16)

    lane = lax.broadcasted_iota(jnp.int32, cos.shape, 1)
    swap_lo = (lane & (HEAD_DIM - 1)) < ROPE_DIM // 2
    for j in range(ATTN_WIDTH // LANES):
        sl = slice(j * LANES, (j + 1) * LANES)
        aq_ref[rs, sl] = _head_norm_rope(aq[:, sl], qn, cos, sin, swap_lo).astype(BF16)
    for j in range(KV_WIDTH // LANES):
        sl = slice(j * LANES, (j + 1) * LANES)
        ak_ref[rs, sl] = _head_norm_rope(ak[:, sl], kn, cos, sin, swap_lo).astype(BF16)


def _proj_kernel(x_ref, mod_ref, pnw_ref, win_ref, qn_ref, kn_ref, cos_ref, sin_ref, lbl_ref, *outs,
                 tm, sub_rows, layer):
    m = mod_ref[...]
    for sub in range(tm // sub_rows):
        rs = slice(sub * sub_rows, (sub + 1) * sub_rows)
        _proj_rows(x_ref[rs, :], m, pnw_ref[...], win_ref, qn_ref[...], kn_ref[...], cos_ref[rs, :], sin_ref[rs, :],
                   lbl_ref, outs, rs, layer)


def _project(x, mod, layer, pnw, w_in, q_norm, k_norm, cos_t, sin_t, lb_logits, tm, sub_rows):
    bsz, t, d = x.shape
    tok = lambda w: pl.BlockSpec((None, tm, w), lambda b, i: (b, i, 0))
    sds = lambda w, dt: jax.ShapeDtypeStruct((bsz, t, w), dt)
    qn = jnp.tile(q_norm * (LOG2E * HEAD_DIM ** -0.5), LANES // HEAD_DIM).reshape(1, LANES)
    kn = jnp.tile(k_norm, LANES // HEAD_DIM).reshape(1, LANES)
    return pl.pallas_call(
        functools.partial(_proj_kernel, tm=tm, sub_rows=sub_rows, layer=layer),
        grid=(bsz, t // tm),
        in_specs=[tok(d), _mod_spec(mod), _resident((1, d)), _resident(w_in.shape),
                  _resident((1, LANES)), _resident((1, LANES)), tok(LANES), tok(LANES), _resident(lb_logits.shape)],
        out_specs=[tok(HGRN_WIDTH)] * 5 + [tok(ATTN_WIDTH), tok(KV_WIDTH), tok(KV_WIDTH)],
        out_shape=[sds(HGRN_WIDTH, BF16), sds(HGRN_WIDTH, F32), sds(HGRN_WIDTH, F32), sds(HGRN_WIDTH, BF16),
                   sds(HGRN_WIDTH, BF16), sds(ATTN_WIDTH, BF16), sds(KV_WIDTH, BF16), sds(KV_WIDTH, BF16)],
        compiler_params=_params(("arbitrary", "arbitrary")),
        name="in_proj",
    )(x, mod, pnw.reshape(1, d), w_in, qn, kn, cos_t, sin_t, lb_logits)


def _hgrn_inner_levels():
    lv = []
    m = HGRN_DIAG
    while m < HGRN_HALF:
        lv.append(m)
        m *= 2
    return tuple(lv)


def _hgrn_code_table():
    n = HGRN_HALF
    t = np.arange(n)[:, None]
    s = np.arange(n)[None, :]
    x = t ^ s
    code = np.full((n, n), -2, np.int32)
    for idx, m in enumerate(_hgrn_inner_levels()):
        code[(x >= m) & (x < 2 * m) & (t > s)] = idx
    code[(t // HGRN_DIAG == s // HGRN_DIAG) & (t >= s)] = -1
    return np.tile(code, (HGRN_CHUNK // HGRN_HALF, 1))


def _split3(g):
    hi = g.astype(BF16)
    r1 = g - hi.astype(F32)
    mid = r1.astype(BF16)
    lo = (r1 - mid.astype(F32)).astype(BF16)
    return hi, mid, lo


def _hgrn_chunk(nw_ref, q_ref, g2_ref, lk_ref, i_ref, gate_ref, tri_ref, code_ref, exp_ref, o_ref, st_ref, rs):
    c = HGRN_CHUNK
    dg = HGRN_DIAG
    hb = HGRN_HALF
    dh = HGRN_DIM
    nt_dims = (((1,), (1,)), ((), ()))
    tn_dims = (((0,), (0,)), ((), ()))

    tri = tri_ref[...]
    b_all = sum(jnp.dot(tri, p, preferred_element_type=F32) for p in _split3(g2_ref[rs, :]))

    code = code_ref[...]
    expand = exp_ref[...]
    nw = nw_ref[...]
    nb = c // dg
    lane3 = lax.broadcasted_iota(jnp.int32, (1, dg, dh), 2)
    row3 = lax.broadcasted_iota(jnp.int32, (1, dg, dh), 1)
    diag_code = jnp.where(lane3 <= row3, lane3, -1)

    for hd in range(HGRN_HEADS):
        sl = slice(hd * dh, (hd + 1) * dh)
        q = q_ref[rs, sl].astype(F32)
        v = i_ref[rs, sl]
        b = b_all[:, sl]
        lk = lk_ref[rs, sl]
        b_last = b[c - 1:c, :]

        st = st_ref[hd]
        qq0 = (q * jnp.exp2(b)).astype(BF16)
        o = lax.dot_general(qq0, st.astype(BF16), nt_dims, preferred_element_type=F32)
        kk_end = jnp.exp2(lk + (b_last - b)).astype(BF16)
        st_ref[hd] = st * jnp.exp2(b_last) + lax.dot_general(v, kk_end, tn_dims, preferred_element_type=F32)

        att_d = None
        for idx, m in reversed(list(enumerate(_hgrn_inner_levels()))):
            shp = (c // (2 * m), 2 * m, dh)
            b3 = b.reshape(shp)
            mid = b3[:, m - 1:m, :]
            zero = jnp.zeros((shp[0], m, dh), F32)
            q_hi = q.reshape(shp)[:, m:, :] * jnp.exp2(b3[:, m:, :] - mid)
            k_lo = jnp.exp2(lk.reshape(shp)[:, :m, :] + (mid - b3[:, :m, :]))
            qm = jnp.concatenate([zero, q_hi], axis=1).reshape(c, dh).astype(BF16)
            km = jnp.concatenate([k_lo, zero], axis=1).reshape(c, dh).astype(BF16)
            a = lax.dot_general(qm, km, nt_dims, preferred_element_type=F32)
            a_d = jnp.concatenate([a[:hb, :hb], a[hb:, hb:]], axis=0)
            att_d = a_d if att_d is None else jnp.where(code == idx, a_d, att_d)

        mid = b[hb - 1:hb, :]
        q_top = (q[hb:] * jnp.exp2(b[hb:] - mid)).astype(BF16)
        k_top = jnp.exp2(lk[:hb] + (mid - b[:hb])).astype(BF16)
        a_top = lax.dot_general(q_top, k_top, nt_dims, preferred_element_type=F32)

        q3 = q.reshape(nb, dg, dh)
        b3 = b.reshape(nb, dg, dh)
        bk3 = (b - lk).reshape(nb, dg, dh)
        compact = jnp.zeros((nb, dg, dh), F32)
        for s_loc in range(dg):
            z = q3 * jnp.exp2(b3 - bk3[:, s_loc:s_loc + 1, :])
            col = jnp.sum(z, axis=-1, keepdims=True)
            compact = jnp.where(diag_code == s_loc, col, compact)
        compact = compact.reshape(c, dh)
        a_diag = jnp.dot(compact.astype(BF16), expand, preferred_element_type=F32)
        att_d = jnp.where(code == -1, a_diag, att_d)

        att = jnp.concatenate([
            jnp.concatenate([att_d[:hb], jnp.zeros((hb, hb), F32)], axis=1),
            jnp.concatenate([a_top, att_d[hb:]], axis=1)], axis=0)
        o = o + jnp.dot(att.astype(BF16), v, preferred_element_type=F32)
        ms = jnp.mean(o * o, axis=-1, keepdims=True)
        y = (o * lax.rsqrt(ms + EPS)) * nw
        o_ref[rs, sl] = (y * gate_ref[rs, sl].astype(F32)).astype(BF16)


def _attn_blocks(sink_ref, q_ref, kc_ref, kp_ref, vc_ref, vp_ref, o_ref, n):
    blk = WINDOW
    nt_dims = (((1,), (1,)), ((), ()))
    k_all = jnp.concatenate([kp_ref[...], kc_ref[...]], axis=0)
    v_all = jnp.concatenate([vp_ref[...], vc_ref[...]], axis=0)
    k_sw = pltpu.roll(k_all.astype(F32), HEAD_DIM, 1).astype(BF16)
    v_sw = pltpu.roll(v_all.astype(F32), HEAD_DIM, 1).astype(BF16)
    lane = lax.broadcasted_iota(jnp.int32, (1, LANES), 1)
    m_lo = jnp.where(lane < HEAD_DIM, 1.0, 0.0).astype(BF16)
    m_hi = jnp.where(lane < HEAD_DIM, 0.0, 1.0).astype(BF16)

    qi = lax.broadcasted_iota(jnp.int32, (2 * blk, blk), 0) & (blk - 1)
    kj = lax.broadcasted_iota(jnp.int32, (2 * blk, blk), 1)
    own = kj <= qi
    top = lax.broadcasted_iota(jnp.int32, (2 * blk, 1), 0) < blk
    first_bias = jnp.where(n > 0, 0.0, -jnp.inf).astype(F32)

    for kh in range(KV_HEADS):
        own_k, other_k = (k_all, k_sw) if kh == 0 else (k_sw, k_all)
        own_v, other_v = (v_all, v_sw) if kh == 0 else (v_sw, v_all)
        kv_par = ((own_k * m_lo, own_v * m_lo), (other_k * m_hi, other_v * m_hi))
        base = kh * GROUP * HEAD_DIM
        ob = HGRN_WIDTH + base
        for j in range(MIX_TM // WINDOW):
            rows = slice(j * blk, (j + 1) * blk)
            keys = slice(j * blk, (j + 2) * blk)
            qs = jnp.concatenate([q_ref[rows, base:base + LANES], q_ref[rows, base + LANES:base + 2 * LANES]], axis=0)
            o = None
            for par, (kk, vv) in enumerate(kv_par):
                s = lax.dot_general(qs, kk[keys], nt_dims, preferred_element_type=F32)
                s_prev = s[:, :blk] + first_bias if j == 0 else s[:, :blk]
                sc = jnp.where(own, s[:, blk:], s_prev)
                sink = jnp.where(top, sink_ref[kh * GROUP + par], sink_ref[kh * GROUP + 2 + par]) * LOG2E
                mx = jnp.maximum(jnp.max(sc, axis=-1, keepdims=True), sink)
                p = jnp.exp2(sc - mx)
                den = jnp.sum(p, axis=-1, keepdims=True) + jnp.exp2(sink - mx)
                p2 = jnp.concatenate([jnp.where(own, 0.0, p), jnp.where(own, p, 0.0)], axis=1).astype(BF16)
                part = jnp.dot(p2, vv[keys], preferred_element_type=F32) / den
                o = part if o is None else o + part
            o_ref[rows, ob:ob + LANES] = o[:blk].astype(BF16)
            o_ref[rows, ob + LANES:ob + 2 * LANES] = o[blk:].astype(BF16)


def _mixer_kernel(sink_ref, nw_ref, hq_ref, g2_ref, lk_ref, hi_ref, gate_ref, tri_ref, code_ref, exp_ref,
                  aq_ref, kc_ref, kp_ref, vc_ref, vp_ref, o_ref, st_ref):
    n = pl.program_id(1)

    @pl.when(n == 0)
    def _():
        st_ref[...] = jnp.zeros(st_ref.shape, F32)

    for ch in range(MIX_TM // HGRN_CHUNK):
        rs = slice(ch * HGRN_CHUNK, (ch + 1) * HGRN_CHUNK)
        _hgrn_chunk(nw_ref, hq_ref, g2_ref, lk_ref, hi_ref, gate_ref, tri_ref, code_ref, exp_ref, o_ref, st_ref, rs)
    _attn_blocks(sink_ref, aq_ref, kc_ref, kp_ref, vc_ref, vp_ref, o_ref, n)


def _mixer(sinks_l, norm_w, hq, g2, lk, hi, gate, aq, ak, av):
    bsz, t, w = hq.shape
    c = HGRN_CHUNK
    tm = MIX_TM
    blk = WINDOW
    assert c == 2 * HGRN_HALF and HGRN_HALF == LANES and tm % c == 0 and tm % blk == 0
    tri = jnp.asarray(np.tril(np.ones((c, c))), BF16)
    code = jnp.asarray(_hgrn_code_table())
    expand_np = (np.arange(HGRN_DIM)[:, None] == (np.arange(HGRN_HALF)[None, :] % HGRN_DIAG)).astype(np.float32)
    expand = jnp.asarray(expand_np, BF16)
    cur = lambda width: pl.BlockSpec((None, tm, width), lambda b, i: (b, i, 0))
    prev = lambda width: pl.BlockSpec((None, blk, width), lambda b, i: (b, jnp.maximum(i * (tm // blk) - 1, 0), 0))
    return pl.pallas_call(
        _mixer_kernel,
        grid=(bsz, t // tm),
        in_specs=[
            pl.BlockSpec(memory_space=pltpu.SMEM),
            _resident((1, HGRN_DIM)),
            cur(w), cur(w), cur(w), cur(w), cur(w),
            _resident((c, c)),
            _resident((c, HGRN_HALF)),
            _resident((HGRN_DIM, HGRN_HALF)),
            cur(ATTN_WIDTH), cur(KV_WIDTH), prev(KV_WIDTH), cur(KV_WIDTH), prev(KV_WIDTH),
        ],
        out_specs=cur(MIX_WIDTH),
        out_shape=jax.ShapeDtypeStruct((bsz, t, MIX_WIDTH), BF16),
        scratch_shapes=[pltpu.VMEM((HGRN_HEADS, HGRN_DIM, HGRN_DIM), F32)],
        compiler_params=_params(("arbitrary", "arbitrary")),
        name="mixer",
    )(sinks_l, norm_w.reshape(1, HGRN_DIM), hq, g2, lk, hi, gate, tri, code, expand, aq, ak, ak, av, av)


def _ffn_kernel(*refs, rows, tm, sub_rows, with_mix, n_cast, with_next_mod):
    refs = list(refs)
    a_scr = refs.pop()
    mod_out = refs.pop() if with_next_mod else None
    cast_dst = [refs.pop() for _ in range(n_cast)][::-1]
    o_ref = refs.pop()
    mod_in = [refs.pop() for _ in range(3)][::-1] if with_next_mod else None
    cast_src = [refs.pop() for _ in range(n_cast)][::-1]
    if with_mix:
        x_ref, mod_ref, nw_ref, wg_ref, wu_ref, wd_ref, om_ref, wo_ref = refs
    else:
        x_ref, mod_ref, nw_ref, wg_ref, wu_ref, wd_ref = refs
    m = mod_ref[...]
    sh, sc, gate = (m[r:r + 1] for r in rows)
    for sub in range(tm // sub_rows):
        rs = slice(sub * sub_rows, (sub + 1) * sub_rows)
        x = x_ref[rs, :]
        if with_mix:
            x = x + m[5:6] * jnp.dot(om_ref[rs, :], wo_ref[...], preferred_element_type=F32)
        h = _rms_mod(x, nw_ref[...], sc, sh).astype(BF16)
        lo = 0
        for width in FFN_CHUNKS:
            gt = jnp.dot(h, wg_ref[:, lo:lo + width], preferred_element_type=F32)
            ut = jnp.dot(h, wu_ref[:, lo:lo + width], preferred_element_type=F32)
            a_scr[rs, lo:lo + width] = (_silu(gt) * ut).astype(BF16)
            lo += width
        y = jnp.dot(a_scr[rs, :], wd_ref[...], preferred_element_type=F32)
        o_ref[rs, :] = x + (0.5 * gate) * y
    for src_ref, dst_ref in zip(cast_src, cast_dst):
        dst_ref[...] = src_ref[...].astype(BF16)
    if with_next_mod:
        @pl.when(pl.program_id(0) == 0)
        def _():
            _mod_rows(*mod_in, mod_out)

        @pl.when(pl.program_id(0) > 0)
        def _():
            mod_out[...] = jnp.zeros(mod_out.shape, F32)


def _ffn(x, mod, nw, wg, wu, wd, rows, tm, sub_rows, mix=None, casts=(), next_mod=None):
    bsz, t, d = x.shape
    dff = wg.shape[-1]
    assert sum(FFN_CHUNKS) == dff and tm % sub_rows == 0
    steps = t // tm
    tok = lambda w: pl.BlockSpec((None, tm, w), lambda b, i: (b, i, 0))
    in_specs = [
        tok(d),
        _mod_spec(mod),
        _resident((1, d)),
        _resident(wg.shape),
        _resident(wu.shape),
        _resident(wd.shape),
    ]
    args = [x, mod, nw.reshape(1, d), wg, wu, wd]
    if mix is not None:
        o_mix, w_out = mix
        in_specs += [tok(o_mix.shape[-1]), _resident(w_out.shape)]
        args += [o_mix, w_out]
    cast_specs = [_cast_specs(arr, l, steps, bsz * steps) for arr, l in casts]
    in_specs += [s[0] for s in cast_specs]
    args += [arr for arr, _ in casts]
    out_specs = [tok(d)] + [s[1] for s in cast_specs]
    out_shape = [jax.ShapeDtypeStruct((bsz, t, d), F32)] + [s[2] for s in cast_specs]
    if next_mod is not None:
        c, ada_w, ada_b, l_next = next_mod
        assert ada_w.shape[-1] % steps == 0 and (ada_w.shape[-1] // steps) % LANES == 0
        m_in, m_args, m_out, m_shape = _mod_specs(c, ada_w, ada_b, l_next, ada_w.shape[-1] // steps, passes=bsz)
        in_specs += m_in
        args += m_args
        out_specs.append(m_out)
        out_shape.append(m_shape)
    outs = list(pl.pallas_call(
        functools.partial(_ffn_kernel, rows=rows, tm=tm, sub_rows=sub_rows, with_mix=mix is not None,
                          n_cast=len(casts), with_next_mod=next_mod is not None),
        grid=(bsz, steps),
        in_specs=in_specs,
        out_specs=out_specs,
        out_shape=out_shape,
        scratch_shapes=[pltpu.VMEM((tm, dff), BF16)],
        compiler_params=_params(("arbitrary", "arbitrary")),
        name="ffn_mix" if mix is not None else "ffn",
    )(*args))
    if next_mod is not None:
        outs[-1] = outs[-1][:, :N_MOD * d].reshape(bsz, N_MOD, d)
    return outs


def kernel(x, c, positions, ada_w, ada_b, norm_ffn1, ffn1_w_gate, ffn1_w_up, ffn1_w_down, norm_mix, w_in,
           lb_logits, hgrn_norm, q_norm, k_norm, sinks, w_out, norm_ffn2, ffn2_w_gate, ffn2_w_up, ffn2_w_down):
    depth = ada_w.shape[0]
    mod = _modulation(c, ada_w, ada_b, 0)
    wg1, wu1, wd1, cos_t, sin_t = _cast_layer_and_rope((ffn1_w_gate, ffn1_w_up, ffn1_w_down), 0, positions)
    for l in range(depth):
        last = l + 1 == depth
        x, w_in_l, wg2, wu2, wd2, w_out_l, *mod_next = _ffn(
            x, mod, norm_ffn1[l], wg1, wu1, wd1, rows=(0, 1, 2), tm=FFN_TM, sub_rows=FFN_SUB,
            casts=((w_in, l), (ffn2_w_gate, l), (ffn2_w_up, l), (ffn2_w_down, l), (w_out, l)),
            next_mod=None if last else (c, ada_w, ada_b, l + 1))
        hq, g2, lk, hi, gate, aq, ak, av = _project(x, mod, l, norm_mix[l], w_in_l, q_norm[l], k_norm[l], cos_t, sin_t,
                                                    lb_logits, tm=PROJ_TM, sub_rows=PROJ_SUB)
        o_mix = _mixer(sinks[l], hgrn_norm[l], hq, g2, lk, hi, gate, aq, ak, av)
        nxt = () if last else ((ffn1_w_gate, l + 1), (ffn1_w_up, l + 1), (ffn1_w_down, l + 1))
        x, *nxt_w = _ffn(x, mod, norm_ffn2[l], wg2, wu2, wd2, rows=(6, 7, 8), tm=FFN_TM, sub_rows=FFN_MIX_SUB,
                         mix=(o_mix, w_out_l), casts=nxt)
        if not last:
            wg1, wu1, wd1 = nxt_w
            mod, = mod_next
    return x
```

```python
import functools

import numpy as np
import jax
import jax.numpy as jnp
from jax import lax
from jax.experimental import pallas as pl
from jax.experimental.pallas import tpu as pltpu

F32 = jnp.float32
BF16 = jnp.bfloat16

HGRN_HEADS = 4
HGRN_DIM = 128
HGRN_WIDTH = HGRN_HEADS * HGRN_DIM
ATTN_HEADS = 8
KV_HEADS = 2
HEAD_DIM = 64
ATTN_WIDTH = ATTN_HEADS * HEAD_DIM
KV_WIDTH = KV_HEADS * HEAD_DIM
MIX_WIDTH = HGRN_WIDTH + ATTN_WIDTH
GROUP = ATTN_HEADS // KV_HEADS
WINDOW = 128
ROPE_THETA = 500000.0
ROPE_DIM = HEAD_DIM // 4
N_MOD = 9
EPS = 1e-6

LANES = 128
VMEM_LIMIT_BYTES = 56 * 1024 * 1024

MOD_TN = 2304
ROPE_TM = 1024
FFN_TM = 1024
FFN_SUB = 256
FFN_MIX_SUB = 512
PROJ_TM = 1024
PROJ_SUB = 256
FFN_CHUNKS = (1536, 1280)
HGRN_CHUNK = 256
HGRN_DIAG = 8
MIX_TM = 1024
HGRN_HALF = 128
LOG2E = 1.4426950408889634


def _params(sem):
    return pltpu.CompilerParams(dimension_semantics=sem, vmem_limit_bytes=VMEM_LIMIT_BYTES)


def _resident(shape):
    nd = len(shape)
    return pl.BlockSpec(shape, lambda *_: (0,) * nd, pipeline_mode=pl.Buffered(1))


def _cast_specs(arr, l, steps_inner, bands):
    _, r, c = arr.shape
    assert r % bands == 0 and (r // bands) % 16 == 0
    rb = r // bands
    return (pl.BlockSpec((None, rb, c), lambda b, i: (l, b * steps_inner + i, 0)),
            pl.BlockSpec((rb, c), lambda b, i: (b * steps_inner + i, 0)),
            jax.ShapeDtypeStruct((r, c), BF16))


def _mod_spec(mod, l):
    return pl.BlockSpec((None, None) + mod.shape[2:], lambda b, i: (l, b, 0, 0))


def _rms_mod(x, nw, sc, sh):
    ms = jnp.mean(x * x, axis=-1, keepdims=True)
    return (x * lax.rsqrt(ms + EPS)) * (nw * (1.0 + sc)) + sh


def _silu(x):
    return x * jax.nn.sigmoid(x)


def _cast_rope_kernel(pos_ref, invf_ref, ec_ref, es_ref, base_ref, *refs):
    n = (len(refs) - 2) // 2
    for src_ref, dst_ref in zip(refs[:n], refs[n:2 * n]):
        dst_ref[...] = src_ref[...].astype(BF16)
    _rope_rows(pos_ref, invf_ref, ec_ref, es_ref, base_ref, refs[2 * n], refs[2 * n + 1])


def _cast_layer_and_rope(arrs, l, positions):
    bsz, t = positions.shape
    steps = t // ROPE_TM
    bands = bsz * steps
    specs = [_cast_specs(a, l, steps, bands) for a in arrs]
    rope_in, rope_args, rope_out, rope_shape = _rope_specs(positions)
    return pl.pallas_call(
        _cast_rope_kernel,
        grid=(bsz, steps),
        in_specs=rope_in + [s[0] for s in specs],
        out_specs=[s[1] for s in specs] + rope_out,
        out_shape=[s[2] for s in specs] + rope_shape,
        compiler_params=_params(("arbitrary", "arbitrary")),
        name="cast_rope",
    )(*rope_args, *arrs)


def _mod_kernel(ct_ref, w_ref, b_ref, o_ref):
    ca = _silu(ct_ref[...])
    w = w_ref[...]
    rows = [jnp.sum(w * ca[:, b:b + 1], axis=0, keepdims=True) for b in range(ca.shape[1])]
    o_ref[...] = jnp.concatenate(rows, axis=0) + b_ref[...]


def _modulation(c, ada_w, ada_b):
    depth, d, n = ada_w.shape
    bsz = c.shape[0]
    out = pl.pallas_call(
        _mod_kernel,
        grid=(depth, n // MOD_TN),
        in_specs=[
            pl.BlockSpec((d, bsz), lambda l, j: (0, 0)),
            pl.BlockSpec((None, d, MOD_TN), lambda l, j: (l, 0, j)),
            pl.BlockSpec((None, 1, MOD_TN), lambda l, j: (l, 0, j)),
        ],
        out_specs=pl.BlockSpec((None, bsz, MOD_TN), lambda l, j: (l, 0, j)),
        out_shape=jax.ShapeDtypeStruct((depth, bsz, n), F32),
        compiler_params=_params(("arbitrary", "arbitrary")),
        name="adaln_mod",
    )(c.T, ada_w, ada_b.reshape(depth, 1, n))
    return out.reshape(depth, bsz, N_MOD, d)


def _rope_rows(pos_ref, invf_ref, ec_ref, es_ref, base_ref, cos_ref, sin_ref):
    tn_dims = (((0,), (0,)), ((), ()))
    ang = invf_ref[...] * pos_ref[...].astype(F32)

    def expand(vals, e_ref):
        pad = jnp.zeros(vals.shape, BF16)
        stacked = jnp.concatenate(list(_split3(vals)) + [pad], axis=0)
        return lax.dot_general(stacked, e_ref[...], tn_dims, preferred_element_type=F32)

    cos_ref[...] = expand(jnp.cos(ang), ec_ref) + base_ref[...]
    sin_ref[...] = expand(jnp.sin(ang), es_ref)


def _rope_specs(positions):
    bsz, t = positions.shape
    nf = ROPE_DIM // 2
    inv_freq = ROPE_THETA ** (-jnp.arange(0, ROPE_DIM, 2, dtype=F32) / ROPE_DIM)
    lane = np.arange(2 * HEAD_DIM) % HEAD_DIM
    rot = lane < ROPE_DIM
    pick = (np.arange(nf)[:, None] == (lane % nf)[None, :]) & rot[None, :]
    e_cos = np.tile(pick.astype(np.float32), (4, 1))
    e_cos[3 * nf:] = 0.0
    e_sin = e_cos * np.where(lane < nf, -1.0, 1.0)[None, :]
    base = (~rot).astype(np.float32).reshape(1, 2 * HEAD_DIM)
    shp = jax.ShapeDtypeStruct((bsz, t, 2 * HEAD_DIM), F32)
    const = lambda shape: pl.BlockSpec(shape, lambda b, i: (0, 0))
    in_specs = [
        pl.BlockSpec((None, 1, ROPE_TM), lambda b, i: (b, 0, i)),
        const((nf, 1)),
        const((4 * nf, 2 * HEAD_DIM)),
        const((4 * nf, 2 * HEAD_DIM)),
        const((1, 2 * HEAD_DIM)),
    ]
    args = [positions.reshape(bsz, 1, t), inv_freq.reshape(nf, 1), jnp.asarray(e_cos, BF16), jnp.asarray(e_sin, BF16),
            jnp.asarray(base)]
    out_specs = [pl.BlockSpec((None, ROPE_TM, 2 * HEAD_DIM), lambda b, i: (b, i, 0))] * 2
    return in_specs, args, out_specs, [shp, shp]


def _head_norm_rope(t, nw, cos, sin, swap_lo):
    sq = t * t
    first = lax.broadcasted_iota(jnp.int32, t.shape, 1) < HEAD_DIM
    s_lo = jnp.sum(jnp.where(first, sq, 0.0), axis=-1, keepdims=True)
    s_all = jnp.sum(sq, axis=-1, keepdims=True)
    ms = jnp.where(first, s_lo, s_all - s_lo) * (1.0 / HEAD_DIM)
    y = (t * lax.rsqrt(ms + EPS)) * nw
    swapped = jnp.where(swap_lo, pltpu.roll(y, LANES - ROPE_DIM // 2, 1), pltpu.roll(y, ROPE_DIM // 2, 1))
    return y * cos + swapped * sin


def _lower_bound_logs(lbl, layer):
    e = jnp.exp(lbl - jnp.max(lbl, axis=0, keepdims=True))
    sm = e / jnp.sum(e, axis=0, keepdims=True)
    cum0 = sm[0:1]
    cum = cum0
    for j in range(1, layer + 1):
        cum = cum + sm[j:j + 1]
    lb = cum - cum0
    return jnp.log(lb), jnp.log1p(-lb)


def _proj_rows(x, m, nw, w_ref, qn, kn, cos, sin, lbl_ref, outs, rs, layer):
    hq_ref, g2_ref, lk_ref, hi_ref, hg_ref, aq_ref, ak_ref, av_ref = outs
    h = _rms_mod(x, nw, m[4:5], m[3:4]).astype(BF16)

    proj = jnp.dot(h, w_ref[...], preferred_element_type=F32)

    def seg_dot(lo, width):
        return proj[:, lo:lo + width]

    hq_ref[rs, :] = (seg_dot(0, HGRN_WIDTH) * (HGRN_DIM ** -0.5)).astype(BF16)

    fl2 = seg_dot(HGRN_WIDTH, HGRN_WIDTH) * LOG2E
    sp2 = jnp.log2(1.0 + jnp.exp2(-jnp.abs(fl2)))
    ls2 = jnp.minimum(fl2, 0.0) - sp2
    lsn2 = -(jnp.maximum(fl2, 0.0) + sp2)
    if layer == 0:
        g2_ref[rs, :] = ls2
        lk_ref[rs, :] = lsn2
    else:
        log_lb, l1m = _lower_bound_logs(lbl_ref[...], layer)
        log_lb2 = log_lb * LOG2E
        l1m2 = l1m * LOG2E
        cc2 = l1m2 + ls2
        lk_ref[rs, :] = l1m2 + lsn2
        dlt = log_lb2 - cc2
        g2_ref[rs, :] = jnp.maximum(log_lb2, cc2) + jnp.log2(1.0 + jnp.exp2(-jnp.abs(dlt)))

    hi_ref[rs, :] = seg_dot(2 * HGRN_WIDTH, HGRN_WIDTH).astype(BF16)
    hg_ref[rs, :] = _silu(seg_dot(3 * HGRN_WIDTH, HGRN_WIDTH)).astype(BF16)
    base = 4 * HGRN_WIDTH
    aq = seg_dot(base, ATTN_WIDTH)
    ak = seg_dot(base + ATTN_WIDTH, KV_WIDTH)
    av_ref[rs, :] = seg_dot(base + ATTN_WIDTH + KV_WIDTH, KV_WIDTH).astype(BF16)

    lane = lax.broadcasted_iota(jnp.int32, cos.shape, 1)
    swap_lo = (lane & (HEAD_DIM - 1)) < ROPE_DIM // 2
    for j in range(ATTN_WIDTH // LANES):
        sl = slice(j * LANES, (j + 1) * LANES)
        aq_ref[rs, sl] = _head_norm_rope(aq[:, sl], qn, cos, sin, swap_lo).astype(BF16)
    for j in range(KV_WIDTH // LANES):
        sl = slice(j * LANES, (j + 1) * LANES)
        ak_ref[rs, sl] = _head_norm_rope(ak[:, sl], kn, cos, sin, swap_lo).astype(BF16)


def _proj_kernel(x_ref, mod_ref, pnw_ref, win_ref, qn_ref, kn_ref, cos_ref, sin_ref, lbl_ref, *outs,
                 tm, sub_rows, layer):
    m = mod_ref[...]
    for sub in range(tm // sub_rows):
        rs = slice(sub * sub_rows, (sub + 1) * sub_rows)
        _proj_rows(x_ref[rs, :], m, pnw_ref[...], win_ref, qn_ref[...], kn_ref[...], cos_ref[rs, :], sin_ref[rs, :],
                   lbl_ref, outs, rs, layer)


def _project(x, mod, layer, pnw, w_in, q_norm, k_norm, cos_t, sin_t, lb_logits, tm, sub_rows):
    bsz, t, d = x.shape
    tok = lambda w: pl.BlockSpec((None, tm, w), lambda b, i: (b, i, 0))
    sds = lambda w, dt: jax.ShapeDtypeStruct((bsz, t, w), dt)
    qn = jnp.tile(q_norm * (LOG2E * HEAD_DIM ** -0.5), LANES // HEAD_DIM).reshape(1, LANES)
    kn = jnp.tile(k_norm, LANES // HEAD_DIM).reshape(1, LANES)
    return pl.pallas_call(
        functools.partial(_proj_kernel, tm=tm, sub_rows=sub_rows, layer=layer),
        grid=(bsz, t // tm),
        in_specs=[tok(d), _mod_spec(mod, layer), _resident((1, d)), _resident(w_in.shape),
                  _resident((1, LANES)), _resident((1, LANES)), tok(LANES), tok(LANES), _resident(lb_logits.shape)],
        out_specs=[tok(HGRN_WIDTH)] * 5 + [tok(ATTN_WIDTH), tok(KV_WIDTH), tok(KV_WIDTH)],
        out_shape=[sds(HGRN_WIDTH, BF16), sds(HGRN_WIDTH, F32), sds(HGRN_WIDTH, F32), sds(HGRN_WIDTH, BF16),
                   sds(HGRN_WIDTH, BF16), sds(ATTN_WIDTH, BF16), sds(KV_WIDTH, BF16), sds(KV_WIDTH, BF16)],
        compiler_params=_params(("arbitrary", "arbitrary")),
        name="in_proj",
    )(x, mod, pnw.reshape(1, d), w_in, qn, kn, cos_t, sin_t, lb_logits)


def _hgrn_inner_levels():
    lv = []
    m = HGRN_DIAG
    while m < HGRN_HALF:
        lv.append(m)
        m *= 2
    return tuple(lv)


def _hgrn_code_table():
    n = HGRN_HALF
    t = np.arange(n)[:, None]
    s = np.arange(n)[None, :]
    x = t ^ s
    code = np.full((n, n), -2, np.int32)
    for idx, m in enumerate(_hgrn_inner_levels()):
        code[(x >= m) & (x < 2 * m) & (t > s)] = idx
    code[(t // HGRN_DIAG == s // HGRN_DIAG) & (t >= s)] = -1
    return np.tile(code, (HGRN_CHUNK // HGRN_HALF, 1))


def _split3(g):
    hi = g.astype(BF16)
    r1 = g - hi.astype(F32)
    mid = r1.astype(BF16)
    lo = (r1 - mid.astype(F32)).astype(BF16)
    return hi, mid, lo


def _hgrn_chunk(nw_ref, q_ref, g2_ref, lk_ref, i_ref, gate_ref, tri_ref, code_ref, exp_ref, o_ref, st_ref, rs):
    c = HGRN_CHUNK
    dg = HGRN_DIAG
    hb = HGRN_HALF
    dh = HGRN_DIM
    nt_dims = (((1,), (1,)), ((), ()))
    tn_dims = (((0,), (0,)), ((), ()))

    tri = tri_ref[...]
    b_all = sum(jnp.dot(tri, p, preferred_element_type=F32) for p in _split3(g2_ref[rs, :]))

    code = code_ref[...]
    expand = exp_ref[...]
    nw = nw_ref[...]
    nb = c // dg
    lane3 = lax.broadcasted_iota(jnp.int32, (1, dg, dh), 2)
    row3 = lax.broadcasted_iota(jnp.int32, (1, dg, dh), 1)
    diag_code = jnp.where(lane3 <= row3, lane3, -1)

    for hd in range(HGRN_HEADS):
        sl = slice(hd * dh, (hd + 1) * dh)
        q = q_ref[rs, sl].astype(F32)
        v = i_ref[rs, sl]
        b = b_all[:, sl]
        lk = lk_ref[rs, sl]
        b_last = b[c - 1:c, :]

        st = st_ref[hd]
        qq0 = (q * jnp.exp2(b)).astype(BF16)
        o = lax.dot_general(qq0, st.astype(BF16), nt_dims, preferred_element_type=F32)
        kk_end = jnp.exp2(lk + (b_last - b)).astype(BF16)
        st_ref[hd] = st * jnp.exp2(b_last) + lax.dot_general(v, kk_end, tn_dims, preferred_element_type=F32)

        att_d = None
        for idx, m in reversed(list(enumerate(_hgrn_inner_levels()))):
            shp = (c // (2 * m), 2 * m, dh)
            b3 = b.reshape(shp)
            mid = b3[:, m - 1:m, :]
            zero = jnp.zeros((shp[0], m, dh), F32)
            q_hi = q.reshape(shp)[:, m:, :] * jnp.exp2(b3[:, m:, :] - mid)
            k_lo = jnp.exp2(lk.reshape(shp)[:, :m, :] + (mid - b3[:, :m, :]))
            qm = jnp.concatenate([zero, q_hi], axis=1).reshape(c, dh).astype(BF16)
            km = jnp.concatenate([k_lo, zero], axis=1).reshape(c, dh).astype(BF16)
            a = lax.dot_general(qm, km, nt_dims, preferred_element_type=F32)
            a_d = jnp.concatenate([a[:hb, :hb], a[hb:, hb:]], axis=0)
            att_d = a_d if att_d is None else jnp.where(code == idx, a_d, att_d)

        mid = b[hb - 1:hb, :]
        q_top = (q[hb:] * jnp.exp2(b[hb:] - mid)).astype(BF16)
        k_top = jnp.exp2(lk[:hb] + (mid - b[:hb])).astype(BF16)
        a_top = lax.dot_general(q_top, k_top, nt_dims, preferred_element_type=F32)

        q3 = q.reshape(nb, dg, dh)
        b3 = b.reshape(nb, dg, dh)
        bk3 = (b - lk).reshape(nb, dg, dh)
        compact = jnp.zeros((nb, dg, dh), F32)
        for s_loc in range(dg):
            z = q3 * jnp.exp2(b3 - bk3[:, s_loc:s_loc + 1, :])
            col = jnp.sum(z, axis=-1, keepdims=True)
            compact = jnp.where(diag_code == s_loc, col, compact)
        compact = compact.reshape(c, dh)
        a_diag = jnp.dot(compact.astype(BF16), expand, preferred_element_type=F32)
        att_d = jnp.where(code == -1, a_diag, att_d)

        att = jnp.concatenate([
            jnp.concatenate([att_d[:hb], jnp.zeros((hb, hb), F32)], axis=1),
            jnp.concatenate([a_top, att_d[hb:]], axis=1)], axis=0)
        o = o + jnp.dot(att.astype(BF16), v, preferred_element_type=F32)
        ms = jnp.mean(o * o, axis=-1, keepdims=True)
        y = (o * lax.rsqrt(ms + EPS)) * nw
        o_ref[rs, sl] = (y * gate_ref[rs, sl].astype(F32)).astype(BF16)


def _attn_blocks(sink_ref, q_ref, kc_ref, kp_ref, vc_ref, vp_ref, o_ref, n):
    blk = WINDOW
    nt_dims = (((1,), (1,)), ((), ()))
    k_all = jnp.concatenate([kp_ref[...], kc_ref[...]], axis=0)
    v_all = jnp.concatenate([vp_ref[...], vc_ref[...]], axis=0)
    k_sw = pltpu.roll(k_all.astype(F32), HEAD_DIM, 1).astype(BF16)
    v_sw = pltpu.roll(v_all.astype(F32), HEAD_DIM, 1).astype(BF16)
    lane = lax.broadcasted_iota(jnp.int32, (1, LANES), 1)
    m_lo = jnp.where(lane < HEAD_DIM, 1.0, 0.0).astype(BF16)
    m_hi = jnp.where(lane < HEAD_DIM, 0.0, 1.0).astype(BF16)

    qi = lax.broadcasted_iota(jnp.int32, (2 * blk, blk), 0) & (blk - 1)
    kj = lax.broadcasted_iota(jnp.int32, (2 * blk, blk), 1)
    own = kj <= qi
    top = lax.broadcasted_iota(jnp.int32, (2 * blk, 1), 0) < blk
    first_bias = jnp.where(n > 0, 0.0, -jnp.inf).astype(F32)

    for kh in range(KV_HEADS):
        own_k, other_k = (k_all, k_sw) if kh == 0 else (k_sw, k_all)
        own_v, other_v = (v_all, v_sw) if kh == 0 else (v_sw, v_all)
        kv_par = ((own_k * m_lo, own_v * m_lo), (other_k * m_hi, other_v * m_hi))
        base = kh * GROUP * HEAD_DIM
        ob = HGRN_WIDTH + base
        for j in range(MIX_TM // WINDOW):
            rows = slice(j * blk, (j + 1) * blk)
            keys = slice(j * blk, (j + 2) * blk)
            qs = jnp.concatenate([q_ref[rows, base:base + LANES], q_ref[rows, base + LANES:base + 2 * LANES]], axis=0)
            o = None
            for par, (kk, vv) in enumerate(kv_par):
                s = lax.dot_general(qs, kk[keys], nt_dims, preferred_element_type=F32)
                s_prev = s[:, :blk] + first_bias if j == 0 else s[:, :blk]
                sc = jnp.where(own, s[:, blk:], s_prev)
                sink = jnp.where(top, sink_ref[kh * GROUP + par], sink_ref[kh * GROUP + 2 + par]) * LOG2E
                mx = jnp.maximum(jnp.max(sc, axis=-1, keepdims=True), sink)
                p = jnp.exp2(sc - mx)
                den = jnp.sum(p, axis=-1, keepdims=True) + jnp.exp2(sink - mx)
                p2 = jnp.concatenate([jnp.where(own, 0.0, p), jnp.where(own, p, 0.0)], axis=1).astype(BF16)
                part = jnp.dot(p2, vv[keys], preferred_element_type=F32) / den
                o = part if o is None else o + part
            o_ref[rows, ob:ob + LANES] = o[:blk].astype(BF16)
            o_ref[rows, ob + LANES:ob + 2 * LANES] = o[blk:].astype(BF16)


def _mixer_kernel(sink_ref, nw_ref, hq_ref, g2_ref, lk_ref, hi_ref, gate_ref, tri_ref, code_ref, exp_ref,
                  aq_ref, kc_ref, kp_ref, vc_ref, vp_ref, o_ref, st_ref):
    n = pl.program_id(1)

    @pl.when(n == 0)
    def _():
        st_ref[...] = jnp.zeros(st_ref.shape, F32)

    for ch in range(MIX_TM // HGRN_CHUNK):
        rs = slice(ch * HGRN_CHUNK, (ch + 1) * HGRN_CHUNK)
        _hgrn_chunk(nw_ref, hq_ref, g2_ref, lk_ref, hi_ref, gate_ref, tri_ref, code_ref, exp_ref, o_ref, st_ref, rs)
    _attn_blocks(sink_ref, aq_ref, kc_ref, kp_ref, vc_ref, vp_ref, o_ref, n)


def _mixer(sinks_l, norm_w, hq, g2, lk, hi, gate, aq, ak, av):
    bsz, t, w = hq.shape
    c = HGRN_CHUNK
    tm = MIX_TM
    blk = WINDOW
    assert c == 2 * HGRN_HALF and HGRN_HALF == LANES and tm % c == 0 and tm % blk == 0
    tri = jnp.asarray(np.tril(np.ones((c, c))), BF16)
    code = jnp.asarray(_hgrn_code_table())
    expand_np = (np.arange(HGRN_DIM)[:, None] == (np.arange(HGRN_HALF)[None, :] % HGRN_DIAG)).astype(np.float32)
    expand = jnp.asarray(expand_np, BF16)
    cur = lambda width: pl.BlockSpec((None, tm, width), lambda b, i: (b, i, 0))
    prev = lambda width: pl.BlockSpec((None, blk, width), lambda b, i: (b, jnp.maximum(i * (tm // blk) - 1, 0), 0))
    return pl.pallas_call(
        _mixer_kernel,
        grid=(bsz, t // tm),
        in_specs=[
            pl.BlockSpec(memory_space=pltpu.SMEM),
            _resident((1, HGRN_DIM)),
            cur(w), cur(w), cur(w), cur(w), cur(w),
            _resident((c, c)),
            _resident((c, HGRN_HALF)),
            _resident((HGRN_DIM, HGRN_HALF)),
            cur(ATTN_WIDTH), cur(KV_WIDTH), prev(KV_WIDTH), cur(KV_WIDTH), prev(KV_WIDTH),
        ],
        out_specs=cur(MIX_WIDTH),
        out_shape=jax.ShapeDtypeStruct((bsz, t, MIX_WIDTH), BF16),
        scratch_shapes=[pltpu.VMEM((HGRN_HEADS, HGRN_DIM, HGRN_DIM), F32)],
        compiler_params=_params(("arbitrary", "arbitrary")),
        name="mixer",
    )(sinks_l, norm_w.reshape(1, HGRN_DIM), hq, g2, lk, hi, gate, tri, code, expand, aq, ak, ak, av, av)


def _ffn_kernel(*refs, rows, tm, sub_rows, with_mix, n_cast):
    refs = list(refs)
    a_scr = refs.pop()
    cast_dst = [refs.pop() for _ in range(n_cast)][::-1]
    o_ref = refs.pop()
    cast_src = [refs.pop() for _ in range(n_cast)][::-1]
    if with_mix:
        x_ref, mod_ref, nw_ref, wg_ref, wu_ref, wd_ref, om_ref, wo_ref = refs
    else:
        x_ref, mod_ref, nw_ref, wg_ref, wu_ref, wd_ref = refs
    m = mod_ref[...]
    sh, sc, gate = (m[r:r + 1] for r in rows)
    for sub in range(tm // sub_rows):
        rs = slice(sub * sub_rows, (sub + 1) * sub_rows)
        x = x_ref[rs, :]
        if with_mix:
            x = x + m[5:6] * jnp.dot(om_ref[rs, :], wo_ref[...], preferred_element_type=F32)
        h = _rms_mod(x, nw_ref[...], sc, sh).astype(BF16)
        lo = 0
        for width in FFN_CHUNKS:
            gt = jnp.dot(h, wg_ref[:, lo:lo + width], preferred_element_type=F32)
            ut = jnp.dot(h, wu_ref[:, lo:lo + width], preferred_element_type=F32)
            a_scr[rs, lo:lo + width] = (_silu(gt) * ut).astype(BF16)
            lo += width
        y = jnp.dot(a_scr[rs, :], wd_ref[...], preferred_element_type=F32)
        o_ref[rs, :] = x + (0.5 * gate) * y
    for src_ref, dst_ref in zip(cast_src, cast_dst):
        dst_ref[...] = src_ref[...].astype(BF16)


def _ffn(x, mod, layer, nw, wg, wu, wd, rows, tm, sub_rows, mix=None, casts=()):
    bsz, t, d = x.shape
    dff = wg.shape[-1]
    assert sum(FFN_CHUNKS) == dff and tm % sub_rows == 0
    steps = t // tm
    tok = lambda w: pl.BlockSpec((None, tm, w), lambda b, i: (b, i, 0))
    in_specs = [
        tok(d),
        _mod_spec(mod, layer),
        _resident((1, d)),
        _resident(wg.shape),
        _resident(wu.shape),
        _resident(wd.shape),
    ]
    args = [x, mod, nw.reshape(1, d), wg, wu, wd]
    if mix is not None:
        o_mix, w_out = mix
        in_specs += [tok(o_mix.shape[-1]), _resident(w_out.shape)]
        args += [o_mix, w_out]
    cast_specs = [_cast_specs(arr, l, steps, bsz * steps) for arr, l in casts]
    in_specs += [s[0] for s in cast_specs]
    args += [arr for arr, _ in casts]
    return pl.pallas_call(
        functools.partial(_ffn_kernel, rows=rows, tm=tm, sub_rows=sub_rows, with_mix=mix is not None,
                          n_cast=len(casts)),
        grid=(bsz, steps),
        in_specs=in_specs,
        out_specs=[tok(d)] + [s[1] for s in cast_specs],
        out_shape=[jax.ShapeDtypeStruct((bsz, t, d), F32)] + [s[2] for s in cast_specs],
        scratch_shapes=[pltpu.VMEM((tm, dff), BF16)],
        compiler_params=_params(("arbitrary", "arbitrary")),
        name="ffn_mix" if mix is not None else "ffn",
    )(*args)


def kernel(x, c, positions, ada_w, ada_b, norm_ffn1, ffn1_w_gate, ffn1_w_up, ffn1_w_down, norm_mix, w_in,
           lb_logits, hgrn_norm, q_norm, k_norm, sinks, w_out, norm_ffn2, ffn2_w_gate, ffn2_w_up, ffn2_w_down):
    depth = ada_w.shape[0]
    mod = _modulation(c, ada_w, ada_b)
    wg1, wu1, wd1, cos_t, sin_t = _cast_layer_and_rope((ffn1_w_gate, ffn1_w_up, ffn1_w_down), 0, positions)
    for l in range(depth):
        x, w_in_l, wg2, wu2, wd2, w_out_l = _ffn(
            x, mod, l, norm_ffn1[l], wg1, wu1, wd1, rows=(0, 1, 2), tm=FFN_TM, sub_rows=FFN_SUB,
            casts=((w_in, l), (ffn2_w_gate, l), (ffn2_w_up, l), (ffn2_w_down, l), (w_out, l)))
        hq, g2, lk, hi, gate, aq, ak, av = _project(x, mod, l, norm_mix[l], w_in_l, q_norm[l], k_norm[l], cos_t, sin_t,
                                                    lb_logits, tm=PROJ_TM, sub_rows=PROJ_SUB)
        o_mix = _mixer(sinks[l], hgrn_norm[l], hq, g2, lk, hi, gate, aq, ak, av)
        nxt = ((ffn1_w_gate, l + 1), (ffn1_w_up, l + 1), (ffn1_w_down, l + 1)) if l + 1 < depth else ()
        x, *nxt_w = _ffn(x, mod, l, norm_ffn2[l], wg2, wu2, wd2, rows=(6, 7, 8), tm=FFN_TM, sub_rows=FFN_MIX_SUB,
                         mix=(o_mix, w_out_l), casts=nxt)
        if nxt_w:
            wg1, wu1, wd1 = nxt_w
    return x
```

```python
import functools

import numpy as np
import jax
import jax.numpy as jnp
from jax import lax
from jax.experimental import pallas as pl
from jax.experimental.pallas import tpu as pltpu

F32 = jnp.float32
BF16 = jnp.bfloat16

HGRN_HEADS = 4
HGRN_DIM = 128
HGRN_WIDTH = HGRN_HEADS * HGRN_DIM
ATTN_HEADS = 8
KV_HEADS = 2
HEAD_DIM = 64
ATTN_WIDTH = ATTN_HEADS * HEAD_DIM
KV_WIDTH = KV_HEADS * HEAD_DIM
MIX_WIDTH = HGRN_WIDTH + ATTN_WIDTH
GROUP = ATTN_HEADS // KV_HEADS
WINDOW = 128
ROPE_THETA = 500000.0
ROPE_DIM = HEAD_DIM // 4
N_MOD = 9
EPS = 1e-6

LANES = 128
VMEM_LIMIT_BYTES = 56 * 1024 * 1024

MOD_TK = 128
ROPE_TM = 1024
FFN_TM = 1024
FFN_SUB = 256
FFN_MIX_SUB = 512
PROJ_TM = 1024
PROJ_SUB = 256
FFN_CHUNKS = (1536, 1280)
HGRN_CHUNK = 256
HGRN_DIAG = 8
MIX_TM = 1024
HGRN_HALF = 128
LOG2E = 1.4426950408889634


def _params(sem):
    return pltpu.CompilerParams(dimension_semantics=sem, vmem_limit_bytes=VMEM_LIMIT_BYTES)


def _resident(shape):
    nd = len(shape)
    return pl.BlockSpec(shape, lambda *_: (0,) * nd, pipeline_mode=pl.Buffered(1))


def _cast_specs(arr, l, steps_inner, bands):
    _, r, c = arr.shape
    assert r % bands == 0 and (r // bands) % 16 == 0
    rb = r // bands
    return (pl.BlockSpec((None, rb, c), lambda b, i: (l, b * steps_inner + i, 0)),
            pl.BlockSpec((rb, c), lambda b, i: (b * steps_inner + i, 0)),
            jax.ShapeDtypeStruct((r, c), BF16))


def _mod_spec(mod, l):
    return pl.BlockSpec((None, None) + mod.shape[2:], lambda b, i: (l, b, 0, 0))


def _rms_mod(x, nw, sc, sh):
    ms = jnp.mean(x * x, axis=-1, keepdims=True)
    return (x * lax.rsqrt(ms + EPS)) * (nw * (1.0 + sc)) + sh


def _silu(x):
    return x * jax.nn.sigmoid(x)


def _cast_rope_kernel(pos_ref, invf_ref, ec_ref, es_ref, base_ref, *refs):
    n = (len(refs) - 2) // 2
    for src_ref, dst_ref in zip(refs[:n], refs[n:2 * n]):
        dst_ref[...] = src_ref[...].astype(BF16)
    _rope_rows(pos_ref, invf_ref, ec_ref, es_ref, base_ref, refs[2 * n], refs[2 * n + 1])


def _cast_layer_and_rope(arrs, l, positions):
    bsz, t = positions.shape
    steps = t // ROPE_TM
    bands = bsz * steps
    specs = [_cast_specs(a, l, steps, bands) for a in arrs]
    rope_in, rope_args, rope_out, rope_shape = _rope_specs(positions)
    return pl.pallas_call(
        _cast_rope_kernel,
        grid=(bsz, steps),
        in_specs=rope_in + [s[0] for s in specs],
        out_specs=[s[1] for s in specs] + rope_out,
        out_shape=[s[2] for s in specs] + rope_shape,
        compiler_params=_params(("arbitrary", "arbitrary")),
        name="cast_rope",
    )(*rope_args, *arrs)


def _mod_kernel(ct_ref, w_ref, b_ref, o_ref):
    @pl.when(pl.program_id(1) == 0)
    def _():
        o_ref[...] = jnp.broadcast_to(b_ref[...], o_ref.shape)

    ca = _silu(ct_ref[...])
    w = w_ref[...]
    rows = [jnp.sum(w * ca[:, b:b + 1], axis=0, keepdims=True) for b in range(ca.shape[1])]
    o_ref[...] += jnp.concatenate(rows, axis=0)


def _modulation(c, ada_w, ada_b):
    depth, d, n = ada_w.shape
    bsz = c.shape[0]
    out = pl.pallas_call(
        _mod_kernel,
        grid=(depth, d // MOD_TK),
        in_specs=[
            pl.BlockSpec((MOD_TK, bsz), lambda l, k: (k, 0)),
            pl.BlockSpec((None, MOD_TK, n), lambda l, k: (l, k, 0)),
            pl.BlockSpec((None, 1, n), lambda l, k: (l, 0, 0)),
        ],
        out_specs=pl.BlockSpec((None, bsz, n), lambda l, k: (l, 0, 0)),
        out_shape=jax.ShapeDtypeStruct((depth, bsz, n), F32),
        compiler_params=_params(("arbitrary", "arbitrary")),
        name="adaln_mod",
    )(c.T, ada_w, ada_b.reshape(depth, 1, n))
    return out.reshape(depth, bsz, N_MOD, d)


def _rope_rows(pos_ref, invf_ref, ec_ref, es_ref, base_ref, cos_ref, sin_ref):
    tn_dims = (((0,), (0,)), ((), ()))
    ang = invf_ref[...] * pos_ref[...].astype(F32)

    def expand(vals, e_ref):
        pad = jnp.zeros(vals.shape, BF16)
        stacked = jnp.concatenate(list(_split3(vals)) + [pad], axis=0)
        return lax.dot_general(stacked, e_ref[...], tn_dims, preferred_element_type=F32)

    cos_ref[...] = expand(jnp.cos(ang), ec_ref) + base_ref[...]
    sin_ref[...] = expand(jnp.sin(ang), es_ref)


def _rope_specs(positions):
    bsz, t = positions.shape
    nf = ROPE_DIM // 2
    inv_freq = ROPE_THETA ** (-jnp.arange(0, ROPE_DIM, 2, dtype=F32) / ROPE_DIM)
    lane = np.arange(2 * HEAD_DIM) % HEAD_DIM
    rot = lane < ROPE_DIM
    pick = (np.arange(nf)[:, None] == (lane % nf)[None, :]) & rot[None, :]
    e_cos = np.tile(pick.astype(np.float32), (4, 1))
    e_cos[3 * nf:] = 0.0
    e_sin = e_cos * np.where(lane < nf, -1.0, 1.0)[None, :]
    base = (~rot).astype(np.float32).reshape(1, 2 * HEAD_DIM)
    shp = jax.ShapeDtypeStruct((bsz, t, 2 * HEAD_DIM), F32)
    const = lambda shape: pl.BlockSpec(shape, lambda b, i: (0, 0))
    in_specs = [
        pl.BlockSpec((None, 1, ROPE_TM), lambda b, i: (b, 0, i)),
        const((nf, 1)),
        const((4 * nf, 2 * HEAD_DIM)),
        const((4 * nf, 2 * HEAD_DIM)),
        const((1, 2 * HEAD_DIM)),
    ]
    args = [positions.reshape(bsz, 1, t), inv_freq.reshape(nf, 1), jnp.asarray(e_cos, BF16), jnp.asarray(e_sin, BF16),
            jnp.asarray(base)]
    out_specs = [pl.BlockSpec((None, ROPE_TM, 2 * HEAD_DIM), lambda b, i: (b, i, 0))] * 2
    return in_specs, args, out_specs, [shp, shp]


def _head_norm_rope(t, nw, cos, sin, swap_lo):
    sq = t * t
    first = lax.broadcasted_iota(jnp.int32, t.shape, 1) < HEAD_DIM
    s_lo = jnp.sum(jnp.where(first, sq, 0.0), axis=-1, keepdims=True)
    s_all = jnp.sum(sq, axis=-1, keepdims=True)
    ms = jnp.where(first, s_lo, s_all - s_lo) * (1.0 / HEAD_DIM)
    y = (t * lax.rsqrt(ms + EPS)) * nw
    swapped = jnp.where(swap_lo, pltpu.roll(y, LANES - ROPE_DIM // 2, 1), pltpu.roll(y, ROPE_DIM // 2, 1))
    return y * cos + swapped * sin


def _lower_bound_logs(lbl, layer):
    e = jnp.exp(lbl - jnp.max(lbl, axis=0, keepdims=True))
    sm = e / jnp.sum(e, axis=0, keepdims=True)
    cum0 = sm[0:1]
    cum = cum0
    for j in range(1, layer + 1):
        cum = cum + sm[j:j + 1]
    lb = cum - cum0
    return jnp.log(lb), jnp.log1p(-lb)


def _proj_rows(x, m, nw, w_ref, qn, kn, cos, sin, lbl_ref, outs, rs, layer):
    hq_ref, g2_ref, lk_ref, hi_ref, hg_ref, aq_ref, ak_ref, av_ref = outs
    h = _rms_mod(x, nw, m[4:5], m[3:4]).astype(BF16)

    proj = jnp.dot(h, w_ref[...], preferred_element_type=F32)

    def seg_dot(lo, width):
        return proj[:, lo:lo + width]

    hq_ref[rs, :] = (seg_dot(0, HGRN_WIDTH) * (HGRN_DIM ** -0.5)).astype(BF16)

    fl2 = seg_dot(HGRN_WIDTH, HGRN_WIDTH) * LOG2E
    sp2 = jnp.log2(1.0 + jnp.exp2(-jnp.abs(fl2)))
    ls2 = jnp.minimum(fl2, 0.0) - sp2
    lsn2 = -(jnp.maximum(fl2, 0.0) + sp2)
    if layer == 0:
        g2_ref[rs, :] = ls2
        lk_ref[rs, :] = lsn2
    else:
        log_lb, l1m = _lower_bound_logs(lbl_ref[...], layer)
        log_lb2 = log_lb * LOG2E
        l1m2 = l1m * LOG2E
        cc2 = l1m2 + ls2
        lk_ref[rs, :] = l1m2 + lsn2
        dlt = log_lb2 - cc2
        g2_ref[rs, :] = jnp.maximum(log_lb2, cc2) + jnp.log2(1.0 + jnp.exp2(-jnp.abs(dlt)))

    hi_ref[rs, :] = seg_dot(2 * HGRN_WIDTH, HGRN_WIDTH).astype(BF16)
    hg_ref[rs, :] = _silu(seg_dot(3 * HGRN_WIDTH, HGRN_WIDTH)).astype(BF16)
    base = 4 * HGRN_WIDTH
    aq = seg_dot(base, ATTN_WIDTH)
    ak = seg_dot(base + ATTN_WIDTH, KV_WIDTH)
    av_ref[rs, :] = seg_dot(base + ATTN_WIDTH + KV_WIDTH, KV_WIDTH).astype(BF16)

    lane = lax.broadcasted_iota(jnp.int32, cos.shape, 1)
    swap_lo = (lane & (HEAD_DIM - 1)) < ROPE_DIM // 2
    for j in range(ATTN_WIDTH // LANES):
        sl = slice(j * LANES, (j + 1) * LANES)
        aq_ref[rs, sl] = _head_norm_rope(aq[:, sl], qn, cos, sin, swap_lo).astype(BF16)
    for j in range(KV_WIDTH // LANES):
        sl = slice(j * LANES, (j + 1) * LANES)
        ak_ref[rs, sl] = _head_norm_rope(ak[:, sl], kn, cos, sin, swap_lo).astype(BF16)


def _proj_kernel(x_ref, mod_ref, pnw_ref, win_ref, qn_ref, kn_ref, cos_ref, sin_ref, lbl_ref, *outs,
                 tm, sub_rows, layer):
    m = mod_ref[...]
    for sub in range(tm // sub_rows):
        rs = slice(sub * sub_rows, (sub + 1) * sub_rows)
        _proj_rows(x_ref[rs, :], m, pnw_ref[...], win_ref, qn_ref[...], kn_ref[...], cos_ref[rs, :], sin_ref[rs, :],
                   lbl_ref, outs, rs, layer)


def _project(x, mod, layer, pnw, w_in, q_norm, k_norm, cos_t, sin_t, lb_logits, tm, sub_rows):
    bsz, t, d = x.shape
    tok = lambda w: pl.BlockSpec((None, tm, w), lambda b, i: (b, i, 0))
    sds = lambda w, dt: jax.ShapeDtypeStruct((bsz, t, w), dt)
    qn = jnp.tile(q_norm * (LOG2E * HEAD_DIM ** -0.5), LANES // HEAD_DIM).reshape(1, LANES)
    kn = jnp.tile(k_norm, LANES // HEAD_DIM).reshape(1, LANES)
    return pl.pallas_call(
        functools.partial(_proj_kernel, tm=tm, sub_rows=sub_rows, layer=layer),
        grid=(bsz, t // tm),
        in_specs=[tok(d), _mod_spec(mod, layer), _resident((1, d)), _resident(w_in.shape),
                  _resident((1, LANES)), _resident((1, LANES)), tok(LANES), tok(LANES), _resident(lb_logits.shape)],
        out_specs=[tok(HGRN_WIDTH)] * 5 + [tok(ATTN_WIDTH), tok(KV_WIDTH), tok(KV_WIDTH)],
        out_shape=[sds(HGRN_WIDTH, BF16), sds(HGRN_WIDTH, F32), sds(HGRN_WIDTH, F32), sds(HGRN_WIDTH, BF16),
                   sds(HGRN_WIDTH, BF16), sds(ATTN_WIDTH, BF16), sds(KV_WIDTH, BF16), sds(KV_WIDTH, BF16)],
        compiler_params=_params(("arbitrary", "arbitrary")),
        name="in_proj",
    )(x, mod, pnw.reshape(1, d), w_in, qn, kn, cos_t, sin_t, lb_logits)


def _hgrn_inner_levels():
    lv = []
    m = HGRN_DIAG
    while m < HGRN_HALF:
        lv.append(m)
        m *= 2
    return tuple(lv)


def _hgrn_code_table():
    n = HGRN_HALF
    t = np.arange(n)[:, None]
    s = np.arange(n)[None, :]
    x = t ^ s
    code = np.full((n, n), -2, np.int32)
    for idx, m in enumerate(_hgrn_inner_levels()):
        code[(x >= m) & (x < 2 * m) & (t > s)] = idx
    code[(t // HGRN_DIAG == s // HGRN_DIAG) & (t >= s)] = -1
    return np.tile(code, (HGRN_CHUNK // HGRN_HALF, 1))


def _split3(g):
    hi = g.astype(BF16)
    r1 = g - hi.astype(F32)
    mid = r1.astype(BF16)
    lo = (r1 - mid.astype(F32)).astype(BF16)
    return hi, mid, lo


def _hgrn_chunk(nw_ref, q_ref, g2_ref, lk_ref, i_ref, gate_ref, tri_ref, code_ref, exp_ref, o_ref, st_ref, rs):
    c = HGRN_CHUNK
    dg = HGRN_DIAG
    hb = HGRN_HALF
    dh = HGRN_DIM
    nt_dims = (((1,), (1,)), ((), ()))
    tn_dims = (((0,), (0,)), ((), ()))

    tri = tri_ref[...]
    b_all = sum(jnp.dot(tri, p, preferred_element_type=F32) for p in _split3(g2_ref[rs, :]))

    code = code_ref[...]
    expand = exp_ref[...]
    nw = nw_ref[...]
    nb = c // dg
    lane3 = lax.broadcasted_iota(jnp.int32, (1, dg, dh), 2)
    row3 = lax.broadcasted_iota(jnp.int32, (1, dg, dh), 1)
    diag_code = jnp.where(lane3 <= row3, lane3, -1)

    for hd in range(HGRN_HEADS):
        sl = slice(hd * dh, (hd + 1) * dh)
        q = q_ref[rs, sl].astype(F32)
        v = i_ref[rs, sl]
        b = b_all[:, sl]
        lk = lk_ref[rs, sl]
        b_last = b[c - 1:c, :]

        st = st_ref[hd]
        qq0 = (q * jnp.exp2(b)).astype(BF16)
        o = lax.dot_general(qq0, st.astype(BF16), nt_dims, preferred_element_type=F32)
        kk_end = jnp.exp2(lk + (b_last - b)).astype(BF16)
        st_ref[hd] = st * jnp.exp2(b_last) + lax.dot_general(v, kk_end, tn_dims, preferred_element_type=F32)

        att_d = None
        for idx, m in reversed(list(enumerate(_hgrn_inner_levels()))):
            shp = (c // (2 * m), 2 * m, dh)
            b3 = b.reshape(shp)
            mid = b3[:, m - 1:m, :]
            zero = jnp.zeros((shp[0], m, dh), F32)
            q_hi = q.reshape(shp)[:, m:, :] * jnp.exp2(b3[:, m:, :] - mid)
            k_lo = jnp.exp2(lk.reshape(shp)[:, :m, :] + (mid - b3[:, :m, :]))
            qm = jnp.concatenate([zero, q_hi], axis=1).reshape(c, dh).astype(BF16)
            km = jnp.concatenate([k_lo, zero], axis=1).reshape(c, dh).astype(BF16)
            a = lax.dot_general(qm, km, nt_dims, preferred_element_type=F32)
            a_d = jnp.concatenate([a[:hb, :hb], a[hb:, hb:]], axis=0)
            att_d = a_d if att_d is None else jnp.where(code == idx, a_d, att_d)

        mid = b[hb - 1:hb, :]
        q_top = (q[hb:] * jnp.exp2(b[hb:] - mid)).astype(BF16)
        k_top = jnp.exp2(lk[:hb] + (mid - b[:hb])).astype(BF16)
        a_top = lax.dot_general(q_top, k_top, nt_dims, preferred_element_type=F32)

        q3 = q.reshape(nb, dg, dh)
        b3 = b.reshape(nb, dg, dh)
        bk3 = (b - lk).reshape(nb, dg, dh)
        compact = jnp.zeros((nb, dg, dh), F32)
        for s_loc in range(dg):
            z = q3 * jnp.exp2(b3 - bk3[:, s_loc:s_loc + 1, :])
            col = jnp.sum(z, axis=-1, keepdims=True)
            compact = jnp.where(diag_code == s_loc, col, compact)
        compact = compact.reshape(c, dh)
        a_diag = jnp.dot(compact.astype(BF16), expand, preferred_element_type=F32)
        att_d = jnp.where(code == -1, a_diag, att_d)

        att = jnp.concatenate([
            jnp.concatenate([att_d[:hb], jnp.zeros((hb, hb), F32)], axis=1),
            jnp.concatenate([a_top, att_d[hb:]], axis=1)], axis=0)
        o = o + jnp.dot(att.astype(BF16), v, preferred_element_type=F32)
        ms = jnp.mean(o * o, axis=-1, keepdims=True)
        y = (o * lax.rsqrt(ms + EPS)) * nw
        o_ref[rs, sl] = (y * gate_ref[rs, sl].astype(F32)).astype(BF16)


def _attn_blocks(sink_ref, q_ref, kc_ref, kp_ref, vc_ref, vp_ref, o_ref, n):
    blk = WINDOW
    nt_dims = (((1,), (1,)), ((), ()))
    k_all = jnp.concatenate([kp_ref[...], kc_ref[...]], axis=0)
    v_all = jnp.concatenate([vp_ref[...], vc_ref[...]], axis=0)
    k_sw = pltpu.roll(k_all.astype(F32), HEAD_DIM, 1).astype(BF16)
    v_sw = pltpu.roll(v_all.astype(F32), HEAD_DIM, 1).astype(BF16)
    lane = lax.broadcasted_iota(jnp.int32, (1, LANES), 1)
    m_lo = jnp.where(lane < HEAD_DIM, 1.0, 0.0).astype(BF16)
    m_hi = jnp.where(lane < HEAD_DIM, 0.0, 1.0).astype(BF16)

    qi = lax.broadcasted_iota(jnp.int32, (2 * blk, blk), 0) & (blk - 1)
    kj = lax.broadcasted_iota(jnp.int32, (2 * blk, blk), 1)
    own = kj <= qi
    top = lax.broadcasted_iota(jnp.int32, (2 * blk, 1), 0) < blk
    first_bias = jnp.where(n > 0, 0.0, -jnp.inf).astype(F32)

    for kh in range(KV_HEADS):
        own_k, other_k = (k_all, k_sw) if kh == 0 else (k_sw, k_all)
        own_v, other_v = (v_all, v_sw) if kh == 0 else (v_sw, v_all)
        kv_par = ((own_k * m_lo, own_v * m_lo), (other_k * m_hi, other_v * m_hi))
        base = kh * GROUP * HEAD_DIM
        ob = HGRN_WIDTH + base
        for j in range(MIX_TM // WINDOW):
            rows = slice(j * blk, (j + 1) * blk)
            keys = slice(j * blk, (j + 2) * blk)
            qs = jnp.concatenate([q_ref[rows, base:base + LANES], q_ref[rows, base + LANES:base + 2 * LANES]], axis=0)
            o = None
            for par, (kk, vv) in enumerate(kv_par):
                s = lax.dot_general(qs, kk[keys], nt_dims, preferred_element_type=F32)
                s_prev = s[:, :blk] + first_bias if j == 0 else s[:, :blk]
                sc = jnp.where(own, s[:, blk:], s_prev)
                sink = jnp.where(top, sink_ref[kh * GROUP + par], sink_ref[kh * GROUP + 2 + par]) * LOG2E
                mx = jnp.maximum(jnp.max(sc, axis=-1, keepdims=True), sink)
                p = jnp.exp2(sc - mx)
                den = jnp.sum(p, axis=-1, keepdims=True) + jnp.exp2(sink - mx)
                p2 = jnp.concatenate([jnp.where(own, 0.0, p), jnp.where(own, p, 0.0)], axis=1).astype(BF16)
                part = jnp.dot(p2, vv[keys], preferred_element_type=F32) / den
                o = part if o is None else o + part
            o_ref[rows, ob:ob + LANES] = o[:blk].astype(BF16)
            o_ref[rows, ob + LANES:ob + 2 * LANES] = o[blk:].astype(BF16)


def _mixer_kernel(sink_ref, nw_ref, hq_ref, g2_ref, lk_ref, hi_ref, gate_ref, tri_ref, code_ref, exp_ref,
                  aq_ref, kc_ref, kp_ref, vc_ref, vp_ref, o_ref, st_ref):
    n = pl.program_id(1)

    @pl.when(n == 0)
    def _():
        st_ref[...] = jnp.zeros(st_ref.shape, F32)

    for ch in range(MIX_TM // HGRN_CHUNK):
        rs = slice(ch * HGRN_CHUNK, (ch + 1) * HGRN_CHUNK)
        _hgrn_chunk(nw_ref, hq_ref, g2_ref, lk_ref, hi_ref, gate_ref, tri_ref, code_ref, exp_ref, o_ref, st_ref, rs)
    _attn_blocks(sink_ref, aq_ref, kc_ref, kp_ref, vc_ref, vp_ref, o_ref, n)


def _mixer(sinks_l, norm_w, hq, g2, lk, hi, gate, aq, ak, av):
    bsz, t, w = hq.shape
    c = HGRN_CHUNK
    tm = MIX_TM
    blk = WINDOW
    assert c == 2 * HGRN_HALF and HGRN_HALF == LANES and tm % c == 0 and tm % blk == 0
    tri = jnp.asarray(np.tril(np.ones((c, c))), BF16)
    code = jnp.asarray(_hgrn_code_table())
    expand_np = (np.arange(HGRN_DIM)[:, None] == (np.arange(HGRN_HALF)[None, :] % HGRN_DIAG)).astype(np.float32)
    expand = jnp.asarray(expand_np, BF16)
    cur = lambda width: pl.BlockSpec((None, tm, width), lambda b, i: (b, i, 0))
    prev = lambda width: pl.BlockSpec((None, blk, width), lambda b, i: (b, jnp.maximum(i * (tm // blk) - 1, 0), 0))
    return pl.pallas_call(
        _mixer_kernel,
        grid=(bsz, t // tm),
        in_specs=[
            pl.BlockSpec(memory_space=pltpu.SMEM),
            _resident((1, HGRN_DIM)),
            cur(w), cur(w), cur(w), cur(w), cur(w),
            _resident((c, c)),
            _resident((c, HGRN_HALF)),
            _resident((HGRN_DIM, HGRN_HALF)),
            cur(ATTN_WIDTH), cur(KV_WIDTH), prev(KV_WIDTH), cur(KV_WIDTH), prev(KV_WIDTH),
        ],
        out_specs=cur(MIX_WIDTH),
        out_shape=jax.ShapeDtypeStruct((bsz, t, MIX_WIDTH), BF16),
        scratch_shapes=[pltpu.VMEM((HGRN_HEADS, HGRN_DIM, HGRN_DIM), F32)],
        compiler_params=_params(("arbitrary", "arbitrary")),
        name="mixer",
    )(sinks_l, norm_w.reshape(1, HGRN_DIM), hq, g2, lk, hi, gate, tri, code, expand, aq, ak, ak, av, av)


def _ffn_kernel(*refs, rows, tm, sub_rows, with_mix, n_cast):
    refs = list(refs)
    a_scr = refs.pop()
    cast_dst = [refs.pop() for _ in range(n_cast)][::-1]
    o_ref = refs.pop()
    cast_src = [refs.pop() for _ in range(n_cast)][::-1]
    if with_mix:
        x_ref, mod_ref, nw_ref, wg_ref, wu_ref, wd_ref, om_ref, wo_ref = refs
    else:
        x_ref, mod_ref, nw_ref, wg_ref, wu_ref, wd_ref = refs
    m = mod_ref[...]
    sh, sc, gate = (m[r:r + 1] for r in rows)
    for sub in range(tm // sub_rows):
        rs = slice(sub * sub_rows, (sub + 1) * sub_rows)
        x = x_ref[rs, :]
        if with_mix:
            x = x + m[5:6] * jnp.dot(om_ref[rs, :], wo_ref[...], preferred_element_type=F32)
        h = _rms_mod(x, nw_ref[...], sc, sh).astype(BF16)
        lo = 0
        for width in FFN_CHUNKS:
            gt = jnp.dot(h, wg_ref[:, lo:lo + width], preferred_element_type=F32)
            ut = jnp.dot(h, wu_ref[:, lo:lo + width], preferred_element_type=F32)
            a_scr[rs, lo:lo + width] = (_silu(gt) * ut).astype(BF16)
            lo += width
        y = jnp.dot(a_scr[rs, :], wd_ref[...], preferred_element_type=F32)
        o_ref[rs, :] = x + (0.5 * gate) * y
    for src_ref, dst_ref in zip(cast_src, cast_dst):
        dst_ref[...] = src_ref[...].astype(BF16)


def _ffn(x, mod, layer, nw, wg, wu, wd, rows, tm, sub_rows, mix=None, casts=()):
    bsz, t, d = x.shape
    dff = wg.shape[-1]
    assert sum(FFN_CHUNKS) == dff and tm % sub_rows == 0
    steps = t // tm
    tok = lambda w: pl.BlockSpec((None, tm, w), lambda b, i: (b, i, 0))
    in_specs = [
        tok(d),
        _mod_spec(mod, layer),
        _resident((1, d)),
        _resident(wg.shape),
        _resident(wu.shape),
        _resident(wd.shape),
    ]
    args = [x, mod, nw.reshape(1, d), wg, wu, wd]
    if mix is not None:
        o_mix, w_out = mix
        in_specs += [tok(o_mix.shape[-1]), _resident(w_out.shape)]
        args += [o_mix, w_out]
    cast_specs = [_cast_specs(arr, l, steps, bsz * steps) for arr, l in casts]
    in_specs += [s[0] for s in cast_specs]
    args += [arr for arr, _ in casts]
    return pl.pallas_call(
        functools.partial(_ffn_kernel, rows=rows, tm=tm, sub_rows=sub_rows, with_mix=mix is not None,
                          n_cast=len(casts)),
        grid=(bsz, steps),
        in_specs=in_specs,
        out_specs=[tok(d)] + [s[1] for s in cast_specs],
        out_shape=[jax.ShapeDtypeStruct((bsz, t, d), F32)] + [s[2] for s in cast_specs],
        scratch_shapes=[pltpu.VMEM((tm, dff), BF16)],
        compiler_params=_params(("arbitrary", "arbitrary")),
        name="ffn_mix" if mix is not None else "ffn",
    )(*args)


def kernel(x, c, positions, ada_w, ada_b, norm_ffn1, ffn1_w_gate, ffn1_w_up, ffn1_w_down, norm_mix, w_in,
           lb_logits, hgrn_norm, q_norm, k_norm, sinks, w_out, norm_ffn2, ffn2_w_gate, ffn2_w_up, ffn2_w_down):
    depth = ada_w.shape[0]
    mod = _modulation(c, ada_w, ada_b)
    wg1, wu1, wd1, cos_t, sin_t = _cast_layer_and_rope((ffn1_w_gate, ffn1_w_up, ffn1_w_down), 0, positions)
    for l in range(depth):
        x, w_in_l, wg2, wu2, wd2, w_out_l = _ffn(
            x, mod, l, norm_ffn1[l], wg1, wu1, wd1, rows=(0, 1, 2), tm=FFN_TM, sub_rows=FFN_SUB,
            casts=((w_in, l), (ffn2_w_gate, l), (ffn2_w_up, l), (ffn2_w_down, l), (w_out, l)))
        hq, g2, lk, hi, gate, aq, ak, av = _project(x, mod, l, norm_mix[l], w_in_l, q_norm[l], k_norm[l], cos_t, sin_t,
                                                    lb_logits, tm=PROJ_TM, sub_rows=PROJ_SUB)
        o_mix = _mixer(sinks[l], hgrn_norm[l], hq, g2, lk, hi, gate, aq, ak, av)
        nxt = ((ffn1_w_gate, l + 1), (ffn1_w_up, l + 1), (ffn1_w_down, l + 1)) if l + 1 < depth else ()
        x, *nxt_w = _ffn(x, mod, l, norm_ffn2[l], wg2, wu2, wd2, rows=(6, 7, 8), tm=FFN_TM, sub_rows=FFN_MIX_SUB,
                         mix=(o_mix, w_out_l), casts=nxt)
        if nxt_w:
            wg1, wu1, wd1 = nxt_w
    return x
```

```python
import functools

import numpy as np
import jax
import jax.numpy as jnp
from jax import lax
from jax.experimental import pallas as pl
from jax.experimental.pallas import tpu as pltpu

F32 = jnp.float32
BF16 = jnp.bfloat16

HGRN_HEADS = 4
HGRN_DIM = 128
HGRN_WIDTH = HGRN_HEADS * HGRN_DIM
ATTN_HEADS = 8
KV_HEADS = 2
HEAD_DIM = 64
ATTN_WIDTH = ATTN_HEADS * HEAD_DIM
KV_WIDTH = KV_HEADS * HEAD_DIM
MIX_WIDTH = HGRN_WIDTH + ATTN_WIDTH
GROUP = ATTN_HEADS // KV_HEADS
WINDOW = 128
ROPE_THETA = 500000.0
ROPE_DIM = HEAD_DIM // 4
N_MOD = 9
EPS = 1e-6

LANES = 128
VMEM_LIMIT_BYTES = 56 * 1024 * 1024

MOD_TN = 2304
ROPE_TM = 1024
FFN_TM = 1024
FFN_SUB = 256
FFN_MIX_SUB = 512
PROJ_TM = 1024
PROJ_SUB = 128
FFN_CHUNKS = (1536, 1280)
HGRN_CHUNK = 256
HGRN_DIAG = 8
MIX_TM = 2048
HGRN_HALF = 128
LOG2E = 1.4426950408889634


def _params(sem):
    return pltpu.CompilerParams(dimension_semantics=sem, vmem_limit_bytes=VMEM_LIMIT_BYTES)


def _resident(shape):
    nd = len(shape)
    return pl.BlockSpec(shape, lambda *_: (0,) * nd, pipeline_mode=pl.Buffered(1))


def _cast_specs(arr, l, steps_inner, bands):
    _, r, c = arr.shape
    assert r % bands == 0 and (r // bands) % 16 == 0
    rb = r // bands
    return (pl.BlockSpec((None, rb, c), lambda b, i: (l, b * steps_inner + i, 0)),
            pl.BlockSpec((rb, c), lambda b, i: (b * steps_inner + i, 0)),
            jax.ShapeDtypeStruct((r, c), BF16))


def _mod_spec(mod, l):
    return pl.BlockSpec((None, None) + mod.shape[2:], lambda b, i: (l, b, 0, 0))


def _rms_mod(x, nw, sc, sh):
    ms = jnp.mean(x * x, axis=-1, keepdims=True)
    return (x * lax.rsqrt(ms + EPS)) * (nw * (1.0 + sc)) + sh


def _silu(x):
    return x * jax.nn.sigmoid(x)


def _cast_rope_kernel(pos_ref, invf_ref, ec_ref, es_ref, base_ref, *refs):
    n = (len(refs) - 2) // 2
    for src_ref, dst_ref in zip(refs[:n], refs[n:2 * n]):
        dst_ref[...] = src_ref[...].astype(BF16)
    _rope_rows(pos_ref, invf_ref, ec_ref, es_ref, base_ref, refs[2 * n], refs[2 * n + 1])


def _cast_layer_and_rope(arrs, l, positions):
    bsz, t = positions.shape
    steps = t // ROPE_TM
    bands = bsz * steps
    specs = [_cast_specs(a, l, steps, bands) for a in arrs]
    rope_in, rope_args, rope_out, rope_shape = _rope_specs(positions)
    return pl.pallas_call(
        _cast_rope_kernel,
        grid=(bsz, steps),
        in_specs=rope_in + [s[0] for s in specs],
        out_specs=[s[1] for s in specs] + rope_out,
        out_shape=[s[2] for s in specs] + rope_shape,
        compiler_params=_params(("arbitrary", "arbitrary")),
        name="cast_rope",
    )(*rope_args, *arrs)


def _mod_kernel(ct_ref, w_ref, b_ref, o_ref):
    ca = _silu(ct_ref[...])
    w = w_ref[...]
    rows = [jnp.sum(w * ca[:, b:b + 1], axis=0, keepdims=True) for b in range(ca.shape[1])]
    o_ref[...] = jnp.concatenate(rows, axis=0) + b_ref[...]


def _modulation(c, ada_w, ada_b):
    depth, d, n = ada_w.shape
    bsz = c.shape[0]
    out = pl.pallas_call(
        _mod_kernel,
        grid=(depth, n // MOD_TN),
        in_specs=[
            pl.BlockSpec((d, bsz), lambda l, j: (0, 0)),
            pl.BlockSpec((None, d, MOD_TN), lambda l, j: (l, 0, j)),
            pl.BlockSpec((None, 1, MOD_TN), lambda l, j: (l, 0, j)),
        ],
        out_specs=pl.BlockSpec((None, bsz, MOD_TN), lambda l, j: (l, 0, j)),
        out_shape=jax.ShapeDtypeStruct((depth, bsz, n), F32),
        compiler_params=_params(("arbitrary", "arbitrary")),
        name="adaln_mod",
    )(c.T, ada_w, ada_b.reshape(depth, 1, n))
    return out.reshape(depth, bsz, N_MOD, d)


def _rope_rows(pos_ref, invf_ref, ec_ref, es_ref, base_ref, cos_ref, sin_ref):
    tn_dims = (((0,), (0,)), ((), ()))
    ang = invf_ref[...] * pos_ref[...].astype(F32)

    def expand(vals, e_ref):
        pad = jnp.zeros(vals.shape, BF16)
        stacked = jnp.concatenate(list(_split3(vals)) + [pad], axis=0)
        return lax.dot_general(stacked, e_ref[...], tn_dims, preferred_element_type=F32)

    cos_ref[...] = expand(jnp.cos(ang), ec_ref) + base_ref[...]
    sin_ref[...] = expand(jnp.sin(ang), es_ref)


def _rope_specs(positions):
    bsz, t = positions.shape
    nf = ROPE_DIM // 2
    inv_freq = ROPE_THETA ** (-jnp.arange(0, ROPE_DIM, 2, dtype=F32) / ROPE_DIM)
    lane = np.arange(2 * HEAD_DIM) % HEAD_DIM
    rot = lane < ROPE_DIM
    pick = (np.arange(nf)[:, None] == (lane % nf)[None, :]) & rot[None, :]
    e_cos = np.tile(pick.astype(np.float32), (4, 1))
    e_cos[3 * nf:] = 0.0
    e_sin = e_cos * np.where(lane < nf, -1.0, 1.0)[None, :]
    base = (~rot).astype(np.float32).reshape(1, 2 * HEAD_DIM)
    shp = jax.ShapeDtypeStruct((bsz, t, 2 * HEAD_DIM), F32)
    const = lambda shape: pl.BlockSpec(shape, lambda b, i: (0, 0))
    in_specs = [
        pl.BlockSpec((None, 1, ROPE_TM), lambda b, i: (b, 0, i)),
        const((nf, 1)),
        const((4 * nf, 2 * HEAD_DIM)),
        const((4 * nf, 2 * HEAD_DIM)),
        const((1, 2 * HEAD_DIM)),
    ]
    args = [positions.reshape(bsz, 1, t), inv_freq.reshape(nf, 1), jnp.asarray(e_cos, BF16), jnp.asarray(e_sin, BF16),
            jnp.asarray(base)]
    out_specs = [pl.BlockSpec((None, ROPE_TM, 2 * HEAD_DIM), lambda b, i: (b, i, 0))] * 2
    return in_specs, args, out_specs, [shp, shp]


def _head_norm_rope(t, nw, cos, sin, swap_lo):
    sq = t * t
    first = lax.broadcasted_iota(jnp.int32, t.shape, 1) < HEAD_DIM
    s_lo = jnp.sum(jnp.where(first, sq, 0.0), axis=-1, keepdims=True)
    s_all = jnp.sum(sq, axis=-1, keepdims=True)
    ms = jnp.where(first, s_lo, s_all - s_lo) * (1.0 / HEAD_DIM)
    y = (t * lax.rsqrt(ms + EPS)) * nw
    swapped = jnp.where(swap_lo, pltpu.roll(y, LANES - ROPE_DIM // 2, 1), pltpu.roll(y, ROPE_DIM // 2, 1))
    return y * cos + swapped * sin


def _lower_bound_logs(lbl, layer):
    e = jnp.exp(lbl - jnp.max(lbl, axis=0, keepdims=True))
    sm = e / jnp.sum(e, axis=0, keepdims=True)
    cum0 = sm[0:1]
    cum = cum0
    for j in range(1, layer + 1):
        cum = cum + sm[j:j + 1]
    lb = cum - cum0
    return jnp.log(lb), jnp.log1p(-lb)


def _proj_rows(x, m, nw, w_ref, qn, kn, cos, sin, lbl_ref, outs, rs, layer):
    hq_ref, g2_ref, lk_ref, hi_ref, hg_ref, aq_ref, ak_ref, av_ref = outs
    h = _rms_mod(x, nw, m[4:5], m[3:4]).astype(BF16)

    proj = jnp.dot(h, w_ref[...], preferred_element_type=F32)

    def seg_dot(lo, width):
        return proj[:, lo:lo + width]

    hq_ref[rs, :] = (seg_dot(0, HGRN_WIDTH) * (HGRN_DIM ** -0.5)).astype(BF16)

    fl2 = seg_dot(HGRN_WIDTH, HGRN_WIDTH) * LOG2E
    sp2 = jnp.log2(1.0 + jnp.exp2(-jnp.abs(fl2)))
    ls2 = jnp.minimum(fl2, 0.0) - sp2
    lsn2 = -(jnp.maximum(fl2, 0.0) + sp2)
    if layer == 0:
        g2_ref[rs, :] = ls2
        lk_ref[rs, :] = lsn2
    else:
        log_lb, l1m = _lower_bound_logs(lbl_ref[...], layer)
        log_lb2 = log_lb * LOG2E
        l1m2 = l1m * LOG2E
        cc2 = l1m2 + ls2
        lk_ref[rs, :] = l1m2 + lsn2
        dlt = log_lb2 - cc2
        g2_ref[rs, :] = jnp.maximum(log_lb2, cc2) + jnp.log2(1.0 + jnp.exp2(-jnp.abs(dlt)))

    hi_ref[rs, :] = seg_dot(2 * HGRN_WIDTH, HGRN_WIDTH).astype(BF16)
    hg_ref[rs, :] = _silu(seg_dot(3 * HGRN_WIDTH, HGRN_WIDTH)).astype(BF16)
    base = 4 * HGRN_WIDTH
    aq = seg_dot(base, ATTN_WIDTH)
    ak = seg_dot(base + ATTN_WIDTH, KV_WIDTH)
    av_ref[rs, :] = seg_dot(base + ATTN_WIDTH + KV_WIDTH, KV_WIDTH).astype(BF16)

    lane = lax.broadcasted_iota(jnp.int32, cos.shape, 1)
    swap_lo = (lane & (HEAD_DIM - 1)) < ROPE_DIM // 2
    for j in range(ATTN_WIDTH // LANES):
        sl = slice(j * LANES, (j + 1) * LANES)
        aq_ref[rs, sl] = _head_norm_rope(aq[:, sl], qn, cos, sin, swap_lo).astype(BF16)
    for j in range(KV_WIDTH // LANES):
        sl = slice(j * LANES, (j + 1) * LANES)
        ak_ref[rs, sl] = _head_norm_rope(ak[:, sl], kn, cos, sin, swap_lo).astype(BF16)


def _proj_kernel(x_ref, mod_ref, pnw_ref, win_ref, qn_ref, kn_ref, cos_ref, sin_ref, lbl_ref, *outs,
                 tm, sub_rows, layer):
    m = mod_ref[...]
    for sub in range(tm // sub_rows):
        rs = slice(sub * sub_rows, (sub + 1) * sub_rows)
        _proj_rows(x_ref[rs, :], m, pnw_ref[...], win_ref, qn_ref[...], kn_ref[...], cos_ref[rs, :], sin_ref[rs, :],
                   lbl_ref, outs, rs, layer)


def _project(x, mod, layer, pnw, w_in, q_norm, k_norm, cos_t, sin_t, lb_logits, tm, sub_rows):
    bsz, t, d = x.shape
    tok = lambda w: pl.BlockSpec((None, tm, w), lambda b, i: (b, i, 0))
    sds = lambda w, dt: jax.ShapeDtypeStruct((bsz, t, w), dt)
    qn = jnp.tile(q_norm * (LOG2E * HEAD_DIM ** -0.5), LANES // HEAD_DIM).reshape(1, LANES)
    kn = jnp.tile(k_norm, LANES // HEAD_DIM).reshape(1, LANES)
    return pl.pallas_call(
        functools.partial(_proj_kernel, tm=tm, sub_rows=sub_rows, layer=layer),
        grid=(bsz, t // tm),
        in_specs=[tok(d), _mod_spec(mod, layer), _resident((1, d)), _resident(w_in.shape),
                  _resident((1, LANES)), _resident((1, LANES)), tok(LANES), tok(LANES), _resident(lb_logits.shape)],
        out_specs=[tok(HGRN_WIDTH)] * 5 + [tok(ATTN_WIDTH), tok(KV_WIDTH), tok(KV_WIDTH)],
        out_shape=[sds(HGRN_WIDTH, BF16), sds(HGRN_WIDTH, F32), sds(HGRN_WIDTH, F32), sds(HGRN_WIDTH, BF16),
                   sds(HGRN_WIDTH, BF16), sds(ATTN_WIDTH, BF16), sds(KV_WIDTH, BF16), sds(KV_WIDTH, BF16)],
        compiler_params=_params(("arbitrary", "arbitrary")),
        name="in_proj",
    )(x, mod, pnw.reshape(1, d), w_in, qn, kn, cos_t, sin_t, lb_logits)


def _hgrn_inner_levels():
    lv = []
    m = HGRN_DIAG
    while m < HGRN_HALF:
        lv.append(m)
        m *= 2
    return tuple(lv)


def _hgrn_code_table():
    n = HGRN_HALF
    t = np.arange(n)[:, None]
    s = np.arange(n)[None, :]
    x = t ^ s
    code = np.full((n, n), -2, np.int32)
    for idx, m in enumerate(_hgrn_inner_levels()):
        code[(x >= m) & (x < 2 * m) & (t > s)] = idx
    code[(t // HGRN_DIAG == s // HGRN_DIAG) & (t >= s)] = -1
    return np.tile(code, (HGRN_CHUNK // HGRN_HALF, 1))


def _split3(g):
    hi = g.astype(BF16)
    r1 = g - hi.astype(F32)
    mid = r1.astype(BF16)
    lo = (r1 - mid.astype(F32)).astype(BF16)
    return hi, mid, lo


def _hgrn_chunk(nw_ref, q_ref, g2_ref, lk_ref, i_ref, gate_ref, tri_ref, code_ref, exp_ref, o_ref, st_ref, rs):
    c = HGRN_CHUNK
    dg = HGRN_DIAG
    hb = HGRN_HALF
    dh = HGRN_DIM
    nt_dims = (((1,), (1,)), ((), ()))
    tn_dims = (((0,), (0,)), ((), ()))

    tri = tri_ref[...]
    b_all = sum(jnp.dot(tri, p, preferred_element_type=F32) for p in _split3(g2_ref[rs, :]))

    code = code_ref[...]
    expand = exp_ref[...]
    nw = nw_ref[...]
    nb = c // dg
    lane3 = lax.broadcasted_iota(jnp.int32, (1, dg, dh), 2)
    row3 = lax.broadcasted_iota(jnp.int32, (1, dg, dh), 1)
    diag_code = jnp.where(lane3 <= row3, lane3, -1)

    for hd in range(HGRN_HEADS):
        sl = slice(hd * dh, (hd + 1) * dh)
        q = q_ref[rs, sl].astype(F32)
        v = i_ref[rs, sl]
        b = b_all[:, sl]
        lk = lk_ref[rs, sl]
        b_last = b[c - 1:c, :]

        st = st_ref[hd]
        qq0 = (q * jnp.exp2(b)).astype(BF16)
        o = lax.dot_general(qq0, st.astype(BF16), nt_dims, preferred_element_type=F32)
        kk_end = jnp.exp2(lk + (b_last - b)).astype(BF16)
        st_ref[hd] = st * jnp.exp2(b_last) + lax.dot_general(v, kk_end, tn_dims, preferred_element_type=F32)

        att_d = None
        for idx, m in reversed(list(enumerate(_hgrn_inner_levels()))):
            shp = (c // (2 * m), 2 * m, dh)
            b3 = b.reshape(shp)
            mid = b3[:, m - 1:m, :]
            zero = jnp.zeros((shp[0], m, dh), F32)
            q_hi = q.reshape(shp)[:, m:, :] * jnp.exp2(b3[:, m:, :] - mid)
            k_lo = jnp.exp2(lk.reshape(shp)[:, :m, :] + (mid - b3[:, :m, :]))
            qm = jnp.concatenate([zero, q_hi], axis=1).reshape(c, dh).astype(BF16)
            km = jnp.concatenate([k_lo, zero], axis=1).reshape(c, dh).astype(BF16)
            a = lax.dot_general(qm, km, nt_dims, preferred_element_type=F32)
            a_d = jnp.concatenate([a[:hb, :hb], a[hb:, hb:]], axis=0)
            att_d = a_d if att_d is None else jnp.where(code == idx, a_d, att_d)

        mid = b[hb - 1:hb, :]
        q_top = (q[hb:] * jnp.exp2(b[hb:] - mid)).astype(BF16)
        k_top = jnp.exp2(lk[:hb] + (mid - b[:hb])).astype(BF16)
        a_top = lax.dot_general(q_top, k_top, nt_dims, preferred_element_type=F32)

        q3 = q.reshape(nb, dg, dh)
        b3 = b.reshape(nb, dg, dh)
        bk3 = (b - lk).reshape(nb, dg, dh)
        compact = jnp.zeros((nb, dg, dh), F32)
        for s_loc in range(dg):
            z = q3 * jnp.exp2(b3 - bk3[:, s_loc:s_loc + 1, :])
            col = jnp.sum(z, axis=-1, keepdims=True)
            compact = jnp.where(diag_code == s_loc, col, compact)
        compact = compact.reshape(c, dh)
        a_diag = jnp.dot(compact.astype(BF16), expand, preferred_element_type=F32)
        att_d = jnp.where(code == -1, a_diag, att_d)

        att = jnp.concatenate([
            jnp.concatenate([att_d[:hb], jnp.zeros((hb, hb), F32)], axis=1),
            jnp.concatenate([a_top, att_d[hb:]], axis=1)], axis=0)
        o = o + jnp.dot(att.astype(BF16), v, preferred_element_type=F32)
        ms = jnp.mean(o * o, axis=-1, keepdims=True)
        y = (o * lax.rsqrt(ms + EPS)) * nw
        o_ref[rs, sl] = (y * gate_ref[rs, sl].astype(F32)).astype(BF16)


def _attn_blocks(sink_ref, q_ref, kc_ref, kp_ref, vc_ref, vp_ref, o_ref, n):
    blk = WINDOW
    nt_dims = (((1,), (1,)), ((), ()))
    k_all = jnp.concatenate([kp_ref[...], kc_ref[...]], axis=0)
    v_all = jnp.concatenate([vp_ref[...], vc_ref[...]], axis=0)
    k_sw = pltpu.roll(k_all.astype(F32), HEAD_DIM, 1).astype(BF16)
    v_sw = pltpu.roll(v_all.astype(F32), HEAD_DIM, 1).astype(BF16)
    lane = lax.broadcasted_iota(jnp.int32, (1, LANES), 1)
    m_lo = jnp.where(lane < HEAD_DIM, 1.0, 0.0).astype(BF16)
    m_hi = jnp.where(lane < HEAD_DIM, 0.0, 1.0).astype(BF16)

    qi = lax.broadcasted_iota(jnp.int32, (2 * blk, blk), 0) & (blk - 1)
    kj = lax.broadcasted_iota(jnp.int32, (2 * blk, blk), 1)
    own = kj <= qi
    top = lax.broadcasted_iota(jnp.int32, (2 * blk, 1), 0) < blk
    first_bias = jnp.where(n > 0, 0.0, -jnp.inf).astype(F32)

    for kh in range(KV_HEADS):
        own_k, other_k = (k_all, k_sw) if kh == 0 else (k_sw, k_all)
        own_v, other_v = (v_all, v_sw) if kh == 0 else (v_sw, v_all)
        kv_par = ((own_k * m_lo, own_v * m_lo), (other_k * m_hi, other_v * m_hi))
        base = kh * GROUP * HEAD_DIM
        ob = HGRN_WIDTH + base
        for j in range(MIX_TM // WINDOW):
            rows = slice(j * blk, (j + 1) * blk)
            keys = slice(j * blk, (j + 2) * blk)
            qs = jnp.concatenate([q_ref[rows, base:base + LANES], q_ref[rows, base + LANES:base + 2 * LANES]], axis=0)
            o = None
            for par, (kk, vv) in enumerate(kv_par):
                s = lax.dot_general(qs, kk[keys], nt_dims, preferred_element_type=F32)
                s_prev = s[:, :blk] + first_bias if j == 0 else s[:, :blk]
                sc = jnp.where(own, s[:, blk:], s_prev)
                sink = jnp.where(top, sink_ref[kh * GROUP + par], sink_ref[kh * GROUP + 2 + par]) * LOG2E
                mx = jnp.maximum(jnp.max(sc, axis=-1, keepdims=True), sink)
                p = jnp.exp2(sc - mx)
                den = jnp.sum(p, axis=-1, keepdims=True) + jnp.exp2(sink - mx)
                p2 = jnp.concatenate([jnp.where(own, 0.0, p), jnp.where(own, p, 0.0)], axis=1).astype(BF16)
                part = jnp.dot(p2, vv[keys], preferred_element_type=F32) / den
                o = part if o is None else o + part
            o_ref[rows, ob:ob + LANES] = o[:blk].astype(BF16)
            o_ref[rows, ob + LANES:ob + 2 * LANES] = o[blk:].astype(BF16)


def _mixer_kernel(sink_ref, nw_ref, hq_ref, g2_ref, lk_ref, hi_ref, gate_ref, tri_ref, code_ref, exp_ref,
                  aq_ref, kc_ref, kp_ref, vc_ref, vp_ref, o_ref, st_ref):
    n = pl.program_id(1)

    @pl.when(n == 0)
    def _():
        st_ref[...] = jnp.zeros(st_ref.shape, F32)

    for ch in range(MIX_TM // HGRN_CHUNK):
        rs = slice(ch * HGRN_CHUNK, (ch + 1) * HGRN_CHUNK)
        _hgrn_chunk(nw_ref, hq_ref, g2_ref, lk_ref, hi_ref, gate_ref, tri_ref, code_ref, exp_ref, o_ref, st_ref, rs)
    _attn_blocks(sink_ref, aq_ref, kc_ref, kp_ref, vc_ref, vp_ref, o_ref, n)


def _mixer(sinks_l, norm_w, hq, g2, lk, hi, gate, aq, ak, av):
    bsz, t, w = hq.shape
    c = HGRN_CHUNK
    tm = MIX_TM
    blk = WINDOW
    assert c == 2 * HGRN_HALF and HGRN_HALF == LANES and tm % c == 0 and tm % blk == 0
    tri = jnp.asarray(np.tril(np.ones((c, c))), BF16)
    code = jnp.asarray(_hgrn_code_table())
    expand_np = (np.arange(HGRN_DIM)[:, None] == (np.arange(HGRN_HALF)[None, :] % HGRN_DIAG)).astype(np.float32)
    expand = jnp.asarray(expand_np, BF16)
    cur = lambda width: pl.BlockSpec((None, tm, width), lambda b, i: (b, i, 0))
    prev = lambda width: pl.BlockSpec((None, blk, width), lambda b, i: (b, jnp.maximum(i * (tm // blk) - 1, 0), 0))
    return pl.pallas_call(
        _mixer_kernel,
        grid=(bsz, t // tm),
        in_specs=[
            pl.BlockSpec(memory_space=pltpu.SMEM),
            _resident((1, HGRN_DIM)),
            cur(w), cur(w), cur(w), cur(w), cur(w),
            _resident((c, c)),
            _resident((c, HGRN_HALF)),
            _resident((HGRN_DIM, HGRN_HALF)),
            cur(ATTN_WIDTH), cur(KV_WIDTH), prev(KV_WIDTH), cur(KV_WIDTH), prev(KV_WIDTH),
        ],
        out_specs=cur(MIX_WIDTH),
        out_shape=jax.ShapeDtypeStruct((bsz, t, MIX_WIDTH), BF16),
        scratch_shapes=[pltpu.VMEM((HGRN_HEADS, HGRN_DIM, HGRN_DIM), F32)],
        compiler_params=_params(("arbitrary", "arbitrary")),
        name="mixer",
    )(sinks_l, norm_w.reshape(1, HGRN_DIM), hq, g2, lk, hi, gate, tri, code, expand, aq, ak, ak, av, av)


def _ffn_kernel(*refs, rows, tm, sub_rows, with_mix, n_cast):
    refs = list(refs)
    a_scr = refs.pop()
    cast_dst = [refs.pop() for _ in range(n_cast)][::-1]
    o_ref = refs.pop()
    cast_src = [refs.pop() for _ in range(n_cast)][::-1]
    if with_mix:
        x_ref, mod_ref, nw_ref, wg_ref, wu_ref, wd_ref, om_ref, wo_ref = refs
    else:
        x_ref, mod_ref, nw_ref, wg_ref, wu_ref, wd_ref = refs
    m = mod_ref[...]
    sh, sc, gate = (m[r:r + 1] for r in rows)
    for sub in range(tm // sub_rows):
        rs = slice(sub * sub_rows, (sub + 1) * sub_rows)
        x = x_ref[rs, :]
        if with_mix:
            x = x + m[5:6] * jnp.dot(om_ref[rs, :], wo_ref[...], preferred_element_type=F32)
        h = _rms_mod(x, nw_ref[...], sc, sh).astype(BF16)
        lo = 0
        for width in FFN_CHUNKS:
            gt = jnp.dot(h, wg_ref[:, lo:lo + width], preferred_element_type=F32)
            ut = jnp.dot(h, wu_ref[:, lo:lo + width], preferred_element_type=F32)
            a_scr[rs, lo:lo + width] = (_silu(gt) * ut).astype(BF16)
            lo += width
        y = jnp.dot(a_scr[rs, :], wd_ref[...], preferred_element_type=F32)
        o_ref[rs, :] = x + (0.5 * gate) * y
    for src_ref, dst_ref in zip(cast_src, cast_dst):
        dst_ref[...] = src_ref[...].astype(BF16)


def _ffn(x, mod, layer, nw, wg, wu, wd, rows, tm, sub_rows, mix=None, casts=()):
    bsz, t, d = x.shape
    dff = wg.shape[-1]
    assert sum(FFN_CHUNKS) == dff and tm % sub_rows == 0
    steps = t // tm
    tok = lambda w: pl.BlockSpec((None, tm, w), lambda b, i: (b, i, 0))
    in_specs = [
        tok(d),
        _mod_spec(mod, layer),
        _resident((1, d)),
        _resident(wg.shape),
        _resident(wu.shape),
        _resident(wd.shape),
    ]
    args = [x, mod, nw.reshape(1, d), wg, wu, wd]
    if mix is not None:
        o_mix, w_out = mix
        in_specs += [tok(o_mix.shape[-1]), _resident(w_out.shape)]
        args += [o_mix, w_out]
    cast_specs = [_cast_specs(arr, l, steps, bsz * steps) for arr, l in casts]
    in_specs += [s[0] for s in cast_specs]
    args += [arr for arr, _ in casts]
    return pl.pallas_call(
        functools.partial(_ffn_kernel, rows=rows, tm=tm, sub_rows=sub_rows, with_mix=mix is not None,
                          n_cast=len(casts)),
        grid=(bsz, steps),
        in_specs=in_specs,
        out_specs=[tok(d)] + [s[1] for s in cast_specs],
        out_shape=[jax.ShapeDtypeStruct((bsz, t, d), F32)] + [s[2] for s in cast_specs],
        scratch_shapes=[pltpu.VMEM((tm, dff), BF16)],
        compiler_params=_params(("arbitrary", "arbitrary")),
        name="ffn_mix" if mix is not None else "ffn",
    )(*args)


def kernel(x, c, positions, ada_w, ada_b, norm_ffn1, ffn1_w_gate, ffn1_w_up, ffn1_w_down, norm_mix, w_in,
           lb_logits, hgrn_norm, q_norm, k_norm, sinks, w_out, norm_ffn2, ffn2_w_gate, ffn2_w_up, ffn2_w_down):
    depth = ada_w.shape[0]
    mod = _modulation(c, ada_w, ada_b)
    wg1, wu1, wd1, cos_t, sin_t = _cast_layer_and_rope((ffn1_w_gate, ffn1_w_up, ffn1_w_down), 0, positions)
    for l in range(depth):
        x, w_in_l, wg2, wu2, wd2, w_out_l = _ffn(
            x, mod, l, norm_ffn1[l], wg1, wu1, wd1, rows=(0, 1, 2), tm=FFN_TM, sub_rows=FFN_SUB,
            casts=((w_in, l), (ffn2_w_gate, l), (ffn2_w_up, l), (ffn2_w_down, l), (w_out, l)))
        hq, g2, lk, hi, gate, aq, ak, av = _project(x, mod, l, norm_mix[l], w_in_l, q_norm[l], k_norm[l], cos_t, sin_t,
                                                    lb_logits, tm=PROJ_TM, sub_rows=PROJ_SUB)
        o_mix = _mixer(sinks[l], hgrn_norm[l], hq, g2, lk, hi, gate, aq, ak, av)
        nxt = ((ffn1_w_gate, l + 1), (ffn1_w_up, l + 1), (ffn1_w_down, l + 1)) if l + 1 < depth else ()
        x, *nxt_w = _ffn(x, mod, l, norm_ffn2[l], wg2, wu2, wd2, rows=(6, 7, 8), tm=FFN_TM, sub_rows=FFN_MIX_SUB,
                         mix=(o_mix, w_out_l), casts=nxt)
        if nxt_w:
            wg1, wu1, wd1 = nxt_w
    return x
```

```python
import functools

import numpy as np
import jax
import jax.numpy as jnp
from jax import lax
from jax.experimental import pallas as pl
from jax.experimental.pallas import tpu as pltpu

F32 = jnp.float32
BF16 = jnp.bfloat16

HGRN_HEADS = 4
HGRN_DIM = 128
HGRN_WIDTH = HGRN_HEADS * HGRN_DIM
ATTN_HEADS = 8
KV_HEADS = 2
HEAD_DIM = 64
ATTN_WIDTH = ATTN_HEADS * HEAD_DIM
KV_WIDTH = KV_HEADS * HEAD_DIM
MIX_WIDTH = HGRN_WIDTH + ATTN_WIDTH
PK_Q, PK_I, PK_G = 0, HGRN_WIDTH, 2 * HGRN_WIDTH
PK_AQ = 3 * HGRN_WIDTH
PK_AK = PK_AQ + ATTN_WIDTH
PK_AV = PK_AK + KV_WIDTH
PK_WIDTH = PK_AV + KV_WIDTH
GROUP = ATTN_HEADS // KV_HEADS
WINDOW = 128
ROPE_THETA = 500000.0
ROPE_DIM = HEAD_DIM // 4
N_MOD = 9
EPS = 1e-6

LANES = 128
VMEM_LIMIT_BYTES = 56 * 1024 * 1024

MOD_TN = 2304
ROPE_TM = 1024
FFN_TM = 1024
FFN_SUB = 256
FFN_MIX_SUB = 512
PROJ_TM = 1024
PROJ_SUB = 256
FFN_CHUNKS = (1536, 1280)
HGRN_CHUNK = 256
HGRN_DIAG = 8
MIX_TM = 1024
HGRN_HALF = 128
LOG2E = 1.4426950408889634


def _params(sem):
    return pltpu.CompilerParams(dimension_semantics=sem, vmem_limit_bytes=VMEM_LIMIT_BYTES)


def _resident(shape):
    nd = len(shape)
    return pl.BlockSpec(shape, lambda *_: (0,) * nd, pipeline_mode=pl.Buffered(1))


def _cast_specs(arr, l, steps_inner, bands):
    _, r, c = arr.shape
    assert r % bands == 0 and (r // bands) % 16 == 0
    rb = r // bands
    return (pl.BlockSpec((None, rb, c), lambda b, i: (l, b * steps_inner + i, 0)),
            pl.BlockSpec((rb, c), lambda b, i: (b * steps_inner + i, 0)),
            jax.ShapeDtypeStruct((r, c), BF16))


def _mod_spec(mod, l):
    return pl.BlockSpec((None, None) + mod.shape[2:], lambda b, i: (l, b, 0, 0))


def _rms_mod(x, nw, sc, sh):
    ms = jnp.mean(x * x, axis=-1, keepdims=True)
    return (x * lax.rsqrt(ms + EPS)) * (nw * (1.0 + sc)) + sh


def _silu(x):
    return x * jax.nn.sigmoid(x)


def _cast_rope_kernel(pos_ref, invf_ref, ec_ref, es_ref, base_ref, *refs):
    n = (len(refs) - 2) // 2
    for src_ref, dst_ref in zip(refs[:n], refs[n:2 * n]):
        dst_ref[...] = src_ref[...].astype(BF16)
    _rope_rows(pos_ref, invf_ref, ec_ref, es_ref, base_ref, refs[2 * n], refs[2 * n + 1])


def _cast_layer_and_rope(arrs, l, positions):
    bsz, t = positions.shape
    steps = t // ROPE_TM
    bands = bsz * steps
    specs = [_cast_specs(a, l, steps, bands) for a in arrs]
    rope_in, rope_args, rope_out, rope_shape = _rope_specs(positions)
    return pl.pallas_call(
        _cast_rope_kernel,
        grid=(bsz, steps),
        in_specs=rope_in + [s[0] for s in specs],
        out_specs=[s[1] for s in specs] + rope_out,
        out_shape=[s[2] for s in specs] + rope_shape,
        compiler_params=_params(("arbitrary", "arbitrary")),
        name="cast_rope",
    )(*rope_args, *arrs)


def _mod_kernel(ct_ref, w_ref, b_ref, o_ref):
    ca = _silu(ct_ref[...])
    w = w_ref[...]
    rows = [jnp.sum(w * ca[:, b:b + 1], axis=0, keepdims=True) for b in range(ca.shape[1])]
    o_ref[...] = jnp.concatenate(rows, axis=0) + b_ref[...]


def _modulation(c, ada_w, ada_b):
    depth, d, n = ada_w.shape
    bsz = c.shape[0]
    out = pl.pallas_call(
        _mod_kernel,
        grid=(depth, n // MOD_TN),
        in_specs=[
            pl.BlockSpec((d, bsz), lambda l, j: (0, 0)),
            pl.BlockSpec((None, d, MOD_TN), lambda l, j: (l, 0, j)),
            pl.BlockSpec((None, 1, MOD_TN), lambda l, j: (l, 0, j)),
        ],
        out_specs=pl.BlockSpec((None, bsz, MOD_TN), lambda l, j: (l, 0, j)),
        out_shape=jax.ShapeDtypeStruct((depth, bsz, n), F32),
        compiler_params=_params(("arbitrary", "arbitrary")),
        name="adaln_mod",
    )(c.T, ada_w, ada_b.reshape(depth, 1, n))
    return out.reshape(depth, bsz, N_MOD, d)


def _rope_rows(pos_ref, invf_ref, ec_ref, es_ref, base_ref, cos_ref, sin_ref):
    tn_dims = (((0,), (0,)), ((), ()))
    ang = invf_ref[...] * pos_ref[...].astype(F32)

    def expand(vals, e_ref):
        pad = jnp.zeros(vals.shape, BF16)
        stacked = jnp.concatenate(list(_split3(vals)) + [pad], axis=0)
        return lax.dot_general(stacked, e_ref[...], tn_dims, preferred_element_type=F32)

    cos_ref[...] = expand(jnp.cos(ang), ec_ref) + base_ref[...]
    sin_ref[...] = expand(jnp.sin(ang), es_ref)


def _rope_specs(positions):
    bsz, t = positions.shape
    nf = ROPE_DIM // 2
    inv_freq = ROPE_THETA ** (-jnp.arange(0, ROPE_DIM, 2, dtype=F32) / ROPE_DIM)
    lane = np.arange(2 * HEAD_DIM) % HEAD_DIM
    rot = lane < ROPE_DIM
    pick = (np.arange(nf)[:, None] == (lane % nf)[None, :]) & rot[None, :]
    e_cos = np.tile(pick.astype(np.float32), (4, 1))
    e_cos[3 * nf:] = 0.0
    e_sin = e_cos * np.where(lane < nf, -1.0, 1.0)[None, :]
    base = (~rot).astype(np.float32).reshape(1, 2 * HEAD_DIM)
    shp = jax.ShapeDtypeStruct((bsz, t, 2 * HEAD_DIM), F32)
    const = lambda shape: pl.BlockSpec(shape, lambda b, i: (0, 0))
    in_specs = [
        pl.BlockSpec((None, 1, ROPE_TM), lambda b, i: (b, 0, i)),
        const((nf, 1)),
        const((4 * nf, 2 * HEAD_DIM)),
        const((4 * nf, 2 * HEAD_DIM)),
        const((1, 2 * HEAD_DIM)),
    ]
    args = [positions.reshape(bsz, 1, t), inv_freq.reshape(nf, 1), jnp.asarray(e_cos, BF16), jnp.asarray(e_sin, BF16),
            jnp.asarray(base)]
    out_specs = [pl.BlockSpec((None, ROPE_TM, 2 * HEAD_DIM), lambda b, i: (b, i, 0))] * 2
    return in_specs, args, out_specs, [shp, shp]


def _head_norm_rope(t, nw, cos, sin, swap_lo):
    sq = t * t
    first = lax.broadcasted_iota(jnp.int32, t.shape, 1) < HEAD_DIM
    s_lo = jnp.sum(jnp.where(first, sq, 0.0), axis=-1, keepdims=True)
    s_all = jnp.sum(sq, axis=-1, keepdims=True)
    ms = jnp.where(first, s_lo, s_all - s_lo) * (1.0 / HEAD_DIM)
    y = (t * lax.rsqrt(ms + EPS)) * nw
    swapped = jnp.where(swap_lo, pltpu.roll(y, LANES - ROPE_DIM // 2, 1), pltpu.roll(y, ROPE_DIM // 2, 1))
    return y * cos + swapped * sin


def _lower_bound_logs(lbl, layer):
    e = jnp.exp(lbl - jnp.max(lbl, axis=0, keepdims=True))
    sm = e / jnp.sum(e, axis=0, keepdims=True)
    cum0 = sm[0:1]
    cum = cum0
    for j in range(1, layer + 1):
        cum = cum + sm[j:j + 1]
    lb = cum - cum0
    return jnp.log(lb), jnp.log1p(-lb)


def _proj_rows(x, m, nw, w_ref, qn, kn, cos, sin, lbl_ref, outs, rs, layer):
    pk_ref, f32_ref = outs
    h = _rms_mod(x, nw, m[4:5], m[3:4]).astype(BF16)

    proj = jnp.dot(h, w_ref[...], preferred_element_type=F32)

    def seg_dot(lo, width):
        return proj[:, lo:lo + width]

    pk_ref[rs, PK_Q:PK_Q + HGRN_WIDTH] = (seg_dot(0, HGRN_WIDTH) * (HGRN_DIM ** -0.5)).astype(BF16)

    fl2 = seg_dot(HGRN_WIDTH, HGRN_WIDTH) * LOG2E
    sp2 = jnp.log2(1.0 + jnp.exp2(-jnp.abs(fl2)))
    ls2 = jnp.minimum(fl2, 0.0) - sp2
    lsn2 = -(jnp.maximum(fl2, 0.0) + sp2)
    if layer == 0:
        f32_ref[rs, :HGRN_WIDTH] = ls2
        f32_ref[rs, HGRN_WIDTH:] = lsn2
    else:
        log_lb, l1m = _lower_bound_logs(lbl_ref[...], layer)
        log_lb2 = log_lb * LOG2E
        l1m2 = l1m * LOG2E
        cc2 = l1m2 + ls2
        f32_ref[rs, HGRN_WIDTH:] = l1m2 + lsn2
        dlt = log_lb2 - cc2
        f32_ref[rs, :HGRN_WIDTH] = jnp.maximum(log_lb2, cc2) + jnp.log2(1.0 + jnp.exp2(-jnp.abs(dlt)))

    pk_ref[rs, PK_I:PK_I + HGRN_WIDTH] = seg_dot(2 * HGRN_WIDTH, HGRN_WIDTH).astype(BF16)
    pk_ref[rs, PK_G:PK_G + HGRN_WIDTH] = _silu(seg_dot(3 * HGRN_WIDTH, HGRN_WIDTH)).astype(BF16)
    base = 4 * HGRN_WIDTH
    aq = seg_dot(base, ATTN_WIDTH)
    ak = seg_dot(base + ATTN_WIDTH, KV_WIDTH)
    pk_ref[rs, PK_AV:PK_AV + KV_WIDTH] = seg_dot(base + ATTN_WIDTH + KV_WIDTH, KV_WIDTH).astype(BF16)

    lane = lax.broadcasted_iota(jnp.int32, cos.shape, 1)
    swap_lo = (lane & (HEAD_DIM - 1)) < ROPE_DIM // 2
    for j in range(ATTN_WIDTH // LANES):
        sl = slice(j * LANES, (j + 1) * LANES)
        aq_ref_cols = slice(PK_AQ + j * LANES, PK_AQ + (j + 1) * LANES)
        pk_ref[rs, aq_ref_cols] = _head_norm_rope(aq[:, sl], qn, cos, sin, swap_lo).astype(BF16)
    for j in range(KV_WIDTH // LANES):
        sl = slice(j * LANES, (j + 1) * LANES)
        ak_ref_cols = slice(PK_AK + j * LANES, PK_AK + (j + 1) * LANES)
        pk_ref[rs, ak_ref_cols] = _head_norm_rope(ak[:, sl], kn, cos, sin, swap_lo).astype(BF16)


def _proj_kernel(x_ref, mod_ref, pnw_ref, win_ref, qn_ref, kn_ref, cos_ref, sin_ref, lbl_ref, *outs,
                 tm, sub_rows, layer):
    m = mod_ref[...]
    for sub in range(tm // sub_rows):
        rs = slice(sub * sub_rows, (sub + 1) * sub_rows)
        _proj_rows(x_ref[rs, :], m, pnw_ref[...], win_ref, qn_ref[...], kn_ref[...], cos_ref[rs, :], sin_ref[rs, :],
                   lbl_ref, outs, rs, layer)


def _project(x, mod, layer, pnw, w_in, q_norm, k_norm, cos_t, sin_t, lb_logits, tm, sub_rows):
    bsz, t, d = x.shape
    tok = lambda w: pl.BlockSpec((None, tm, w), lambda b, i: (b, i, 0))
    sds = lambda w, dt: jax.ShapeDtypeStruct((bsz, t, w), dt)
    qn = jnp.tile(q_norm * (LOG2E * HEAD_DIM ** -0.5), LANES // HEAD_DIM).reshape(1, LANES)
    kn = jnp.tile(k_norm, LANES // HEAD_DIM).reshape(1, LANES)
    return pl.pallas_call(
        functools.partial(_proj_kernel, tm=tm, sub_rows=sub_rows, layer=layer),
        grid=(bsz, t // tm),
        in_specs=[tok(d), _mod_spec(mod, layer), _resident((1, d)), _resident(w_in.shape),
                  _resident((1, LANES)), _resident((1, LANES)), tok(LANES), tok(LANES), _resident(lb_logits.shape)],
        out_specs=[tok(PK_WIDTH), tok(2 * HGRN_WIDTH)],
        out_shape=[sds(PK_WIDTH, BF16), sds(2 * HGRN_WIDTH, F32)],
        compiler_params=_params(("arbitrary", "arbitrary")),
        name="in_proj",
    )(x, mod, pnw.reshape(1, d), w_in, qn, kn, cos_t, sin_t, lb_logits)


def _hgrn_inner_levels():
    lv = []
    m = HGRN_DIAG
    while m < HGRN_HALF:
        lv.append(m)
        m *= 2
    return tuple(lv)


def _hgrn_code_table():
    n = HGRN_HALF
    t = np.arange(n)[:, None]
    s = np.arange(n)[None, :]
    x = t ^ s
    code = np.full((n, n), -2, np.int32)
    for idx, m in enumerate(_hgrn_inner_levels()):
        code[(x >= m) & (x < 2 * m) & (t > s)] = idx
    code[(t // HGRN_DIAG == s // HGRN_DIAG) & (t >= s)] = -1
    return np.tile(code, (HGRN_CHUNK // HGRN_HALF, 1))


def _split3(g):
    hi = g.astype(BF16)
    r1 = g - hi.astype(F32)
    mid = r1.astype(BF16)
    lo = (r1 - mid.astype(F32)).astype(BF16)
    return hi, mid, lo


def _hgrn_chunk(nw_ref, q_ref, g2_ref, lk_ref, i_ref, gate_ref, tri_ref, code_ref, exp_ref, o_ref, st_ref, rs):
    c = HGRN_CHUNK
    dg = HGRN_DIAG
    hb = HGRN_HALF
    dh = HGRN_DIM
    nt_dims = (((1,), (1,)), ((), ()))
    tn_dims = (((0,), (0,)), ((), ()))

    tri = tri_ref[...]
    b_all = sum(jnp.dot(tri, p, preferred_element_type=F32) for p in _split3(g2_ref[rs, :]))

    code = code_ref[...]
    expand = exp_ref[...]
    nw = nw_ref[...]
    nb = c // dg
    lane3 = lax.broadcasted_iota(jnp.int32, (1, dg, dh), 2)
    row3 = lax.broadcasted_iota(jnp.int32, (1, dg, dh), 1)
    diag_code = jnp.where(lane3 <= row3, lane3, -1)

    for hd in range(HGRN_HEADS):
        sl = slice(hd * dh, (hd + 1) * dh)
        q = q_ref[rs, sl].astype(F32)
        v = i_ref[rs, sl]
        b = b_all[:, sl]
        lk = lk_ref[rs, sl]
        b_last = b[c - 1:c, :]

        st = st_ref[hd]
        qq0 = (q * jnp.exp2(b)).astype(BF16)
        o = lax.dot_general(qq0, st.astype(BF16), nt_dims, preferred_element_type=F32)
        kk_end = jnp.exp2(lk + (b_last - b)).astype(BF16)
        st_ref[hd] = st * jnp.exp2(b_last) + lax.dot_general(v, kk_end, tn_dims, preferred_element_type=F32)

        att_d = None
        for idx, m in reversed(list(enumerate(_hgrn_inner_levels()))):
            shp = (c // (2 * m), 2 * m, dh)
            b3 = b.reshape(shp)
            mid = b3[:, m - 1:m, :]
            zero = jnp.zeros((shp[0], m, dh), F32)
            q_hi = q.reshape(shp)[:, m:, :] * jnp.exp2(b3[:, m:, :] - mid)
            k_lo = jnp.exp2(lk.reshape(shp)[:, :m, :] + (mid - b3[:, :m, :]))
            qm = jnp.concatenate([zero, q_hi], axis=1).reshape(c, dh).astype(BF16)
            km = jnp.concatenate([k_lo, zero], axis=1).reshape(c, dh).astype(BF16)
            a = lax.dot_general(qm, km, nt_dims, preferred_element_type=F32)
            a_d = jnp.concatenate([a[:hb, :hb], a[hb:, hb:]], axis=0)
            att_d = a_d if att_d is None else jnp.where(code == idx, a_d, att_d)

        mid = b[hb - 1:hb, :]
        q_top = (q[hb:] * jnp.exp2(b[hb:] - mid)).astype(BF16)
        k_top = jnp.exp2(lk[:hb] + (mid - b[:hb])).astype(BF16)
        a_top = lax.dot_general(q_top, k_top, nt_dims, preferred_element_type=F32)

        q3 = q.reshape(nb, dg, dh)
        b3 = b.reshape(nb, dg, dh)
        bk3 = (b - lk).reshape(nb, dg, dh)
        compact = jnp.zeros((nb, dg, dh), F32)
        for s_loc in range(dg):
            z = q3 * jnp.exp2(b3 - bk3[:, s_loc:s_loc + 1, :])
            col = jnp.sum(z, axis=-1, keepdims=True)
            compact = jnp.where(diag_code == s_loc, col, compact)
        compact = compact.reshape(c, dh)
        a_diag = jnp.dot(compact.astype(BF16), expand, preferred_element_type=F32)
        att_d = jnp.where(code == -1, a_diag, att_d)

        att = jnp.concatenate([
            jnp.concatenate([att_d[:hb], jnp.zeros((hb, hb), F32)], axis=1),
            jnp.concatenate([a_top, att_d[hb:]], axis=1)], axis=0)
        o = o + jnp.dot(att.astype(BF16), v, preferred_element_type=F32)
        ms = jnp.mean(o * o, axis=-1, keepdims=True)
        y = (o * lax.rsqrt(ms + EPS)) * nw
        o_ref[rs, sl] = (y * gate_ref[rs, sl].astype(F32)).astype(BF16)


def _attn_blocks(sink_ref, q_ref, kc_ref, kp_ref, vc_ref, vp_ref, o_ref, n):
    blk = WINDOW
    nt_dims = (((1,), (1,)), ((), ()))
    k_all = jnp.concatenate([kp_ref[...], kc_ref[...]], axis=0)
    v_all = jnp.concatenate([vp_ref[...], vc_ref[...]], axis=0)
    k_sw = pltpu.roll(k_all.astype(F32), HEAD_DIM, 1).astype(BF16)
    v_sw = pltpu.roll(v_all.astype(F32), HEAD_DIM, 1).astype(BF16)
    lane = lax.broadcasted_iota(jnp.int32, (1, LANES), 1)
    m_lo = jnp.where(lane < HEAD_DIM, 1.0, 0.0).astype(BF16)
    m_hi = jnp.where(lane < HEAD_DIM, 0.0, 1.0).astype(BF16)

    qi = lax.broadcasted_iota(jnp.int32, (2 * blk, blk), 0) & (blk - 1)
    kj = lax.broadcasted_iota(jnp.int32, (2 * blk, blk), 1)
    own = kj <= qi
    top = lax.broadcasted_iota(jnp.int32, (2 * blk, 1), 0) < blk
    first_bias = jnp.where(n > 0, 0.0, -jnp.inf).astype(F32)

    for kh in range(KV_HEADS):
        own_k, other_k = (k_all, k_sw) if kh == 0 else (k_sw, k_all)
        own_v, other_v = (v_all, v_sw) if kh == 0 else (v_sw, v_all)
        kv_par = ((own_k * m_lo, own_v * m_lo), (other_k * m_hi, other_v * m_hi))
        base = kh * GROUP * HEAD_DIM
        ob = HGRN_WIDTH + base
        for j in range(MIX_TM // WINDOW):
            rows = slice(j * blk, (j + 1) * blk)
            keys = slice(j * blk, (j + 2) * blk)
            qs = jnp.concatenate([q_ref[rows, base:base + LANES], q_ref[rows, base + LANES:base + 2 * LANES]], axis=0)
            o = None
            for par, (kk, vv) in enumerate(kv_par):
                s = lax.dot_general(qs, kk[keys], nt_dims, preferred_element_type=F32)
                s_prev = s[:, :blk] + first_bias if j == 0 else s[:, :blk]
                sc = jnp.where(own, s[:, blk:], s_prev)
                sink = jnp.where(top, sink_ref[kh * GROUP + par], sink_ref[kh * GROUP + 2 + par]) * LOG2E
                mx = jnp.maximum(jnp.max(sc, axis=-1, keepdims=True), sink)
                p = jnp.exp2(sc - mx)
                den = jnp.sum(p, axis=-1, keepdims=True) + jnp.exp2(sink - mx)
                p2 = jnp.concatenate([jnp.where(own, 0.0, p), jnp.where(own, p, 0.0)], axis=1).astype(BF16)
                part = jnp.dot(p2, vv[keys], preferred_element_type=F32) / den
                o = part if o is None else o + part
            o_ref[rows, ob:ob + LANES] = o[:blk].astype(BF16)
            o_ref[rows, ob + LANES:ob + 2 * LANES] = o[blk:].astype(BF16)


def _mixer_kernel(sink_ref, nw_ref, hq_ref, g2_ref, lk_ref, hi_ref, gate_ref, tri_ref, code_ref, exp_ref,
                  aq_ref, kc_ref, kp_ref, vc_ref, vp_ref, o_ref, st_ref):
    n = pl.program_id(1)

    @pl.when(n == 0)
    def _():
        st_ref[...] = jnp.zeros(st_ref.shape, F32)

    for ch in range(MIX_TM // HGRN_CHUNK):
        rs = slice(ch * HGRN_CHUNK, (ch + 1) * HGRN_CHUNK)
        _hgrn_chunk(nw_ref, hq_ref, g2_ref, lk_ref, hi_ref, gate_ref, tri_ref, code_ref, exp_ref, o_ref, st_ref, rs)
    _attn_blocks(sink_ref, aq_ref, kc_ref, kp_ref, vc_ref, vp_ref, o_ref, n)


def _mixer(sinks_l, norm_w, pk, gl):
    bsz, t, _ = pk.shape
    w = HGRN_WIDTH
    c = HGRN_CHUNK
    tm = MIX_TM
    blk = WINDOW
    assert c == 2 * HGRN_HALF and HGRN_HALF == LANES and tm % c == 0 and tm % blk == 0
    tri = jnp.asarray(np.tril(np.ones((c, c))), BF16)
    code = jnp.asarray(_hgrn_code_table())
    expand_np = (np.arange(HGRN_DIM)[:, None] == (np.arange(HGRN_HALF)[None, :] % HGRN_DIAG)).astype(np.float32)
    expand = jnp.asarray(expand_np, BF16)
    cur = lambda width, col: pl.BlockSpec((None, tm, width), lambda b, i: (b, i, col // width))
    prev = lambda width, col: pl.BlockSpec(
        (None, blk, width), lambda b, i: (b, jnp.maximum(i * (tm // blk) - 1, 0), col // width))
    out = lambda width: pl.BlockSpec((None, tm, width), lambda b, i: (b, i, 0))
    return pl.pallas_call(
        _mixer_kernel,
        grid=(bsz, t // tm),
        in_specs=[
            pl.BlockSpec(memory_space=pltpu.SMEM),
            _resident((1, HGRN_DIM)),
            cur(w, PK_Q), cur(w, 0), cur(w, w), cur(w, PK_I), cur(w, PK_G),
            _resident((c, c)),
            _resident((c, HGRN_HALF)),
            _resident((HGRN_DIM, HGRN_HALF)),
            cur(ATTN_WIDTH, PK_AQ), cur(KV_WIDTH, PK_AK), prev(KV_WIDTH, PK_AK), cur(KV_WIDTH, PK_AV),
            prev(KV_WIDTH, PK_AV),
        ],
        out_specs=out(MIX_WIDTH),
        out_shape=jax.ShapeDtypeStruct((bsz, t, MIX_WIDTH), BF16),
        scratch_shapes=[pltpu.VMEM((HGRN_HEADS, HGRN_DIM, HGRN_DIM), F32)],
        compiler_params=_params(("arbitrary", "arbitrary")),
        name="mixer",
    )(sinks_l, norm_w.reshape(1, HGRN_DIM), pk, gl, gl, pk, pk, tri, code, expand, pk, pk, pk, pk, pk)


def _ffn_kernel(*refs, rows, tm, sub_rows, with_mix, n_cast):
    refs = list(refs)
    a_scr = refs.pop()
    cast_dst = [refs.pop() for _ in range(n_cast)][::-1]
    o_ref = refs.pop()
    cast_src = [refs.pop() for _ in range(n_cast)][::-1]
    if with_mix:
        x_ref, mod_ref, nw_ref, wg_ref, wu_ref, wd_ref, om_ref, wo_ref = refs
    else:
        x_ref, mod_ref, nw_ref, wg_ref, wu_ref, wd_ref = refs
    m = mod_ref[...]
    sh, sc, gate = (m[r:r + 1] for r in rows)
    for sub in range(tm // sub_rows):
        rs = slice(sub * sub_rows, (sub + 1) * sub_rows)
        x = x_ref[rs, :]
        if with_mix:
            x = x + m[5:6] * jnp.dot(om_ref[rs, :], wo_ref[...], preferred_element_type=F32)
        h = _rms_mod(x, nw_ref[...], sc, sh).astype(BF16)
        lo = 0
        for width in FFN_CHUNKS:
            gt = jnp.dot(h, wg_ref[:, lo:lo + width], preferred_element_type=F32)
            ut = jnp.dot(h, wu_ref[:, lo:lo + width], preferred_element_type=F32)
            a_scr[rs, lo:lo + width] = (_silu(gt) * ut).astype(BF16)
            lo += width
        y = jnp.dot(a_scr[rs, :], wd_ref[...], preferred_element_type=F32)
        o_ref[rs, :] = x + (0.5 * gate) * y
    for src_ref, dst_ref in zip(cast_src, cast_dst):
        dst_ref[...] = src_ref[...].astype(BF16)


def _ffn(x, mod, layer, nw, wg, wu, wd, rows, tm, sub_rows, mix=None, casts=()):
    bsz, t, d = x.shape
    dff = wg.shape[-1]
    assert sum(FFN_CHUNKS) == dff and tm % sub_rows == 0
    steps = t // tm
    tok = lambda w: pl.BlockSpec((None, tm, w), lambda b, i: (b, i, 0))
    in_specs = [
        tok(d),
        _mod_spec(mod, layer),
        _resident((1, d)),
        _resident(wg.shape),
        _resident(wu.shape),
        _resident(wd.shape),
    ]
    args = [x, mod, nw.reshape(1, d), wg, wu, wd]
    if mix is not None:
        o_mix, w_out = mix
        in_specs += [tok(o_mix.shape[-1]), _resident(w_out.shape)]
        args += [o_mix, w_out]
    cast_specs = [_cast_specs(arr, l, steps, bsz * steps) for arr, l in casts]
    in_specs += [s[0] for s in cast_specs]
    args += [arr for arr, _ in casts]
    return pl.pallas_call(
        functools.partial(_ffn_kernel, rows=rows, tm=tm, sub_rows=sub_rows, with_mix=mix is not None,
                          n_cast=len(casts)),
        grid=(bsz, steps),
        in_specs=in_specs,
        out_specs=[tok(d)] + [s[1] for s in cast_specs],
        out_shape=[jax.ShapeDtypeStruct((bsz, t, d), F32)] + [s[2] for s in cast_specs],
        scratch_shapes=[pltpu.VMEM((tm, dff), BF16)],
        compiler_params=_params(("arbitrary", "arbitrary")),
        name="ffn_mix" if mix is not None else "ffn",
    )(*args)


def kernel(x, c, positions, ada_w, ada_b, norm_ffn1, ffn1_w_gate, ffn1_w_up, ffn1_w_down, norm_mix, w_in,
           lb_logits, hgrn_norm, q_norm, k_norm, sinks, w_out, norm_ffn2, ffn2_w_gate, ffn2_w_up, ffn2_w_down):
    depth = ada_w.shape[0]
    mod = _modulation(c, ada_w, ada_b)
    wg1, wu1, wd1, cos_t, sin_t = _cast_layer_and_rope((ffn1_w_gate, ffn1_w_up, ffn1_w_down), 0, positions)
    for l in range(depth):
        x, w_in_l, wg2, wu2, wd2, w_out_l = _ffn(
            x, mod, l, norm_ffn1[l], wg1, wu1, wd1, rows=(0, 1, 2), tm=FFN_TM, sub_rows=FFN_SUB,
            casts=((w_in, l), (ffn2_w_gate, l), (ffn2_w_up, l), (ffn2_w_down, l), (w_out, l)))
        pk, gl = _project(x, mod, l, norm_mix[l], w_in_l, q_norm[l], k_norm[l], cos_t, sin_t, lb_logits,
                          tm=PROJ_TM, sub_rows=PROJ_SUB)
        o_mix = _mixer(sinks[l], hgrn_norm[l], pk, gl)
        nxt = ((ffn1_w_gate, l + 1), (ffn1_w_up, l + 1), (ffn1_w_down, l + 1)) if l + 1 < depth else ()
        x, *nxt_w = _ffn(x, mod, l, norm_ffn2[l], wg2, wu2, wd2, rows=(6, 7, 8), tm=FFN_TM, sub_rows=FFN_MIX_SUB,
                         mix=(o_mix, w_out_l), casts=nxt)
        if nxt_w:
            wg1, wu1, wd1 = nxt_w
    return x
```

```python
import functools

import numpy as np
import jax
import jax.numpy as jnp
from jax import lax
from jax.experimental import pallas as pl
from jax.experimental.pallas import tpu as pltpu

F32 = jnp.float32
BF16 = jnp.bfloat16

HGRN_HEADS = 4
HGRN_DIM = 128
HGRN_WIDTH = HGRN_HEADS * HGRN_DIM
ATTN_HEADS = 8
KV_HEADS = 2
HEAD_DIM = 64
ATTN_WIDTH = ATTN_HEADS * HEAD_DIM
KV_WIDTH = KV_HEADS * HEAD_DIM
MIX_WIDTH = HGRN_WIDTH + ATTN_WIDTH
PK_Q, PK_I, PK_G = 0, HGRN_WIDTH, 2 * HGRN_WIDTH
PK_AQ = 3 * HGRN_WIDTH
PK_AK = PK_AQ + ATTN_WIDTH
PK_AV = PK_AK + KV_WIDTH
PK_WIDTH = PK_AV + KV_WIDTH
GROUP = ATTN_HEADS // KV_HEADS
WINDOW = 128
ROPE_THETA = 500000.0
ROPE_DIM = HEAD_DIM // 4
N_MOD = 9
EPS = 1e-6

LANES = 128
VMEM_LIMIT_BYTES = 56 * 1024 * 1024

MOD_TN = 2304
ROPE_TM = 1024
FFN_TM = 1024
FFN_SUB = 256
FFN_MIX_SUB = 512
PROJ_TM = 1024
PROJ_SUB = 256
FFN_CHUNKS = (1536, 1280)
HGRN_CHUNK = 256
HGRN_DIAG = 8
MIX_TM = 1024
HGRN_HALF = 128
LOG2E = 1.4426950408889634


def _params(sem):
    return pltpu.CompilerParams(dimension_semantics=sem, vmem_limit_bytes=VMEM_LIMIT_BYTES)


def _resident(shape):
    nd = len(shape)
    return pl.BlockSpec(shape, lambda *_: (0,) * nd, pipeline_mode=pl.Buffered(1))


def _cast_specs(arr, l, steps_inner, bands):
    _, r, c = arr.shape
    assert r % bands == 0 and (r // bands) % 16 == 0
    rb = r // bands
    return (pl.BlockSpec((None, rb, c), lambda b, i: (l, b * steps_inner + i, 0)),
            pl.BlockSpec((rb, c), lambda b, i: (b * steps_inner + i, 0)),
            jax.ShapeDtypeStruct((r, c), BF16))


def _mod_spec(mod, l):
    return pl.BlockSpec((None, None) + mod.shape[2:], lambda b, i: (l, b, 0, 0))


def _rms_mod(x, nw, sc, sh):
    ms = jnp.mean(x * x, axis=-1, keepdims=True)
    return (x * lax.rsqrt(ms + EPS)) * (nw * (1.0 + sc)) + sh


def _silu(x):
    return x * jax.nn.sigmoid(x)


def _cast_rope_kernel(pos_ref, invf_ref, ec_ref, es_ref, base_ref, *refs):
    n = (len(refs) - 2) // 2
    for src_ref, dst_ref in zip(refs[:n], refs[n:2 * n]):
        dst_ref[...] = src_ref[...].astype(BF16)
    _rope_rows(pos_ref, invf_ref, ec_ref, es_ref, base_ref, refs[2 * n], refs[2 * n + 1])


def _cast_layer_and_rope(arrs, l, positions):
    bsz, t = positions.shape
    steps = t // ROPE_TM
    bands = bsz * steps
    specs = [_cast_specs(a, l, steps, bands) for a in arrs]
    rope_in, rope_args, rope_out, rope_shape = _rope_specs(positions)
    return pl.pallas_call(
        _cast_rope_kernel,
        grid=(bsz, steps),
        in_specs=rope_in + [s[0] for s in specs],
        out_specs=[s[1] for s in specs] + rope_out,
        out_shape=[s[2] for s in specs] + rope_shape,
        compiler_params=_params(("arbitrary", "arbitrary")),
        name="cast_rope",
    )(*rope_args, *arrs)


def _mod_kernel(ct_ref, w_ref, b_ref, o_ref):
    ca = _silu(ct_ref[...])
    w = w_ref[...]
    rows = [jnp.sum(w * ca[:, b:b + 1], axis=0, keepdims=True) for b in range(ca.shape[1])]
    o_ref[...] = jnp.concatenate(rows, axis=0) + b_ref[...]


def _modulation(c, ada_w, ada_b):
    depth, d, n = ada_w.shape
    bsz = c.shape[0]
    out = pl.pallas_call(
        _mod_kernel,
        grid=(depth, n // MOD_TN),
        in_specs=[
            pl.BlockSpec((d, bsz), lambda l, j: (0, 0)),
            pl.BlockSpec((None, d, MOD_TN), lambda l, j: (l, 0, j)),
            pl.BlockSpec((None, 1, MOD_TN), lambda l, j: (l, 0, j)),
        ],
        out_specs=pl.BlockSpec((None, bsz, MOD_TN), lambda l, j: (l, 0, j)),
        out_shape=jax.ShapeDtypeStruct((depth, bsz, n), F32),
        compiler_params=_params(("arbitrary", "arbitrary")),
        name="adaln_mod",
    )(c.T, ada_w, ada_b.reshape(depth, 1, n))
    return out.reshape(depth, bsz, N_MOD, d)


def _rope_rows(pos_ref, invf_ref, ec_ref, es_ref, base_ref, cos_ref, sin_ref):
    tn_dims = (((0,), (0,)), ((), ()))
    ang = invf_ref[...] * pos_ref[...].astype(F32)

    def expand(vals, e_ref):
        pad = jnp.zeros(vals.shape, BF16)
        stacked = jnp.concatenate(list(_split3(vals)) + [pad], axis=0)
        return lax.dot_general(stacked, e_ref[...], tn_dims, preferred_element_type=F32)

    cos_ref[...] = expand(jnp.cos(ang), ec_ref) + base_ref[...]
    sin_ref[...] = expand(jnp.sin(ang), es_ref)


def _rope_specs(positions):
    bsz, t = positions.shape
    nf = ROPE_DIM // 2
    inv_freq = ROPE_THETA ** (-jnp.arange(0, ROPE_DIM, 2, dtype=F32) / ROPE_DIM)
    lane = np.arange(2 * HEAD_DIM) % HEAD_DIM
    rot = lane < ROPE_DIM
    pick = (np.arange(nf)[:, None] == (lane % nf)[None, :]) & rot[None, :]
    e_cos = np.tile(pick.astype(np.float32), (4, 1))
    e_cos[3 * nf:] = 0.0
    e_sin = e_cos * np.where(lane < nf, -1.0, 1.0)[None, :]
    base = (~rot).astype(np.float32).reshape(1, 2 * HEAD_DIM)
    shp = jax.ShapeDtypeStruct((bsz, t, 2 * HEAD_DIM), F32)
    const = lambda shape: pl.BlockSpec(shape, lambda b, i: (0, 0))
    in_specs = [
        pl.BlockSpec((None, 1, ROPE_TM), lambda b, i: (b, 0, i)),
        const((nf, 1)),
        const((4 * nf, 2 * HEAD_DIM)),
        const((4 * nf, 2 * HEAD_DIM)),
        const((1, 2 * HEAD_DIM)),
    ]
    args = [positions.reshape(bsz, 1, t), inv_freq.reshape(nf, 1), jnp.asarray(e_cos, BF16), jnp.asarray(e_sin, BF16),
            jnp.asarray(base)]
    out_specs = [pl.BlockSpec((None, ROPE_TM, 2 * HEAD_DIM), lambda b, i: (b, i, 0))] * 2
    return in_specs, args, out_specs, [shp, shp]


def _head_norm_rope(t, nw, cos, sin, swap_lo):
    sq = t * t
    first = lax.broadcasted_iota(jnp.int32, t.shape, 1) < HEAD_DIM
    s_lo = jnp.sum(jnp.where(first, sq, 0.0), axis=-1, keepdims=True)
    s_all = jnp.sum(sq, axis=-1, keepdims=True)
    ms = jnp.where(first, s_lo, s_all - s_lo) * (1.0 / HEAD_DIM)
    y = (t * lax.rsqrt(ms + EPS)) * nw
    swapped = jnp.where(swap_lo, pltpu.roll(y, LANES - ROPE_DIM // 2, 1), pltpu.roll(y, ROPE_DIM // 2, 1))
    return y * cos + swapped * sin


def _lower_bound_logs(lbl, layer):
    e = jnp.exp(lbl - jnp.max(lbl, axis=0, keepdims=True))
    sm = e / jnp.sum(e, axis=0, keepdims=True)
    cum0 = sm[0:1]
    cum = cum0
    for j in range(1, layer + 1):
        cum = cum + sm[j:j + 1]
    lb = cum - cum0
    return jnp.log(lb), jnp.log1p(-lb)


def _proj_rows(x, m, nw, w_ref, qn, kn, cos, sin, lbl_ref, outs, rs, layer):
    pk_ref, f32_ref = outs
    h = _rms_mod(x, nw, m[4:5], m[3:4]).astype(BF16)

    proj = jnp.dot(h, w_ref[...], preferred_element_type=F32)

    def seg_dot(lo, width):
        return proj[:, lo:lo + width]

    pk_ref[rs, PK_Q:PK_Q + HGRN_WIDTH] = (seg_dot(0, HGRN_WIDTH) * (HGRN_DIM ** -0.5)).astype(BF16)

    fl2 = seg_dot(HGRN_WIDTH, HGRN_WIDTH) * LOG2E
    sp2 = jnp.log2(1.0 + jnp.exp2(-jnp.abs(fl2)))
    ls2 = jnp.minimum(fl2, 0.0) - sp2
    lsn2 = -(jnp.maximum(fl2, 0.0) + sp2)
    if layer == 0:
        f32_ref[rs, :HGRN_WIDTH] = ls2
        f32_ref[rs, HGRN_WIDTH:] = lsn2
    else:
        log_lb, l1m = _lower_bound_logs(lbl_ref[...], layer)
        log_lb2 = log_lb * LOG2E
        l1m2 = l1m * LOG2E
        cc2 = l1m2 + ls2
        f32_ref[rs, HGRN_WIDTH:] = l1m2 + lsn2
        dlt = log_lb2 - cc2
        f32_ref[rs, :HGRN_WIDTH] = jnp.maximum(log_lb2, cc2) + jnp.log2(1.0 + jnp.exp2(-jnp.abs(dlt)))

    pk_ref[rs, PK_I:PK_I + HGRN_WIDTH] = seg_dot(2 * HGRN_WIDTH, HGRN_WIDTH).astype(BF16)
    pk_ref[rs, PK_G:PK_G + HGRN_WIDTH] = _silu(seg_dot(3 * HGRN_WIDTH, HGRN_WIDTH)).astype(BF16)
    base = 4 * HGRN_WIDTH
    aq = seg_dot(base, ATTN_WIDTH)
    ak = seg_dot(base + ATTN_WIDTH, KV_WIDTH)
    pk_ref[rs, PK_AV:PK_AV + KV_WIDTH] = seg_dot(base + ATTN_WIDTH + KV_WIDTH, KV_WIDTH).astype(BF16)

    lane = lax.broadcasted_iota(jnp.int32, cos.shape, 1)
    swap_lo = (lane & (HEAD_DIM - 1)) < ROPE_DIM // 2
    for j in range(ATTN_WIDTH // LANES):
        sl = slice(j * LANES, (j + 1) * LANES)
        aq_ref_cols = slice(PK_AQ + j * LANES, PK_AQ + (j + 1) * LANES)
        pk_ref[rs, aq_ref_cols] = _head_norm_rope(aq[:, sl], qn, cos, sin, swap_lo).astype(BF16)
    for j in range(KV_WIDTH // LANES):
        sl = slice(j * LANES, (j + 1) * LANES)
        ak_ref_cols = slice(PK_AK + j * LANES, PK_AK + (j + 1) * LANES)
        pk_ref[rs, ak_ref_cols] = _head_norm_rope(ak[:, sl], kn, cos, sin, swap_lo).astype(BF16)


def _proj_kernel(x_ref, mod_ref, pnw_ref, win_ref, qn_ref, kn_ref, cos_ref, sin_ref, lbl_ref, *outs,
                 tm, sub_rows, layer):
    m = mod_ref[...]
    for sub in range(tm // sub_rows):
        rs = slice(sub * sub_rows, (sub + 1) * sub_rows)
        _proj_rows(x_ref[rs, :], m, pnw_ref[...], win_ref, qn_ref[...], kn_ref[...], cos_ref[rs, :], sin_ref[rs, :],
                   lbl_ref, outs, rs, layer)


def _project(x, mod, layer, pnw, w_in, q_norm, k_norm, cos_t, sin_t, lb_logits, tm, sub_rows):
    bsz, t, d = x.shape
    tok = lambda w: pl.BlockSpec((None, tm, w), lambda b, i: (b, i, 0))
    sds = lambda w, dt: jax.ShapeDtypeStruct((bsz, t, w), dt)
    qn = jnp.tile(q_norm * (LOG2E * HEAD_DIM ** -0.5), LANES // HEAD_DIM).reshape(1, LANES)
    kn = jnp.tile(k_norm, LANES // HEAD_DIM).reshape(1, LANES)
    return pl.pallas_call(
        functools.partial(_proj_kernel, tm=tm, sub_rows=sub_rows, layer=layer),
        grid=(bsz, t // tm),
        in_specs=[tok(d), _mod_spec(mod, layer), _resident((1, d)), _resident(w_in.shape),
                  _resident((1, LANES)), _resident((1, LANES)), tok(LANES), tok(LANES), _resident(lb_logits.shape)],
        out_specs=[tok(PK_WIDTH), tok(2 * HGRN_WIDTH)],
        out_shape=[sds(PK_WIDTH, BF16), sds(2 * HGRN_WIDTH, F32)],
        compiler_params=_params(("arbitrary", "arbitrary")),
        name="in_proj",
    )(x, mod, pnw.reshape(1, d), w_in, qn, kn, cos_t, sin_t, lb_logits)


def _hgrn_inner_levels():
    lv = []
    m = HGRN_DIAG
    while m < HGRN_HALF:
        lv.append(m)
        m *= 2
    return tuple(lv)


def _hgrn_code_table():
    n = HGRN_HALF
    t = np.arange(n)[:, None]
    s = np.arange(n)[None, :]
    x = t ^ s
    code = np.full((n, n), -2, np.int32)
    for idx, m in enumerate(_hgrn_inner_levels()):
        code[(x >= m) & (x < 2 * m) & (t > s)] = idx
    code[(t // HGRN_DIAG == s // HGRN_DIAG) & (t >= s)] = -1
    return np.tile(code, (HGRN_CHUNK // HGRN_HALF, 1))


def _split3(g):
    hi = g.astype(BF16)
    r1 = g - hi.astype(F32)
    mid = r1.astype(BF16)
    lo = (r1 - mid.astype(F32)).astype(BF16)
    return hi, mid, lo


def _hgrn_chunk(nw_ref, q_ref, g2_ref, lk_ref, i_ref, gate_ref, tri_ref, code_ref, exp_ref, o_ref, st_ref, rs):
    c = HGRN_CHUNK
    dg = HGRN_DIAG
    hb = HGRN_HALF
    dh = HGRN_DIM
    nt_dims = (((1,), (1,)), ((), ()))
    tn_dims = (((0,), (0,)), ((), ()))

    tri = tri_ref[...]
    b_all = sum(jnp.dot(tri, p, preferred_element_type=F32) for p in _split3(g2_ref[rs, :]))

    code = code_ref[...]
    expand = exp_ref[...]
    nw = nw_ref[...]
    nb = c // dg
    lane3 = lax.broadcasted_iota(jnp.int32, (1, dg, dh), 2)
    row3 = lax.broadcasted_iota(jnp.int32, (1, dg, dh), 1)
    diag_code = jnp.where(lane3 <= row3, lane3, -1)

    for hd in range(HGRN_HEADS):
        sl = slice(hd * dh, (hd + 1) * dh)
        q = q_ref[rs, sl].astype(F32)
        v = i_ref[rs, sl]
        b = b_all[:, sl]
        lk = lk_ref[rs, sl]
        b_last = b[c - 1:c, :]

        st = st_ref[hd]
        qq0 = (q * jnp.exp2(b)).astype(BF16)
        o = lax.dot_general(qq0, st.astype(BF16), nt_dims, preferred_element_type=F32)
        kk_end = jnp.exp2(lk + (b_last - b)).astype(BF16)
        st_ref[hd] = st * jnp.exp2(b_last) + lax.dot_general(v, kk_end, tn_dims, preferred_element_type=F32)

        att_d = None
        for idx, m in reversed(list(enumerate(_hgrn_inner_levels()))):
            shp = (c // (2 * m), 2 * m, dh)
            b3 = b.reshape(shp)
            mid = b3[:, m - 1:m, :]
            zero = jnp.zeros((shp[0], m, dh), F32)
            q_hi = q.reshape(shp)[:, m:, :] * jnp.exp2(b3[:, m:, :] - mid)
            k_lo = jnp.exp2(lk.reshape(shp)[:, :m, :] + (mid - b3[:, :m, :]))
            qm = jnp.concatenate([zero, q_hi], axis=1).reshape(c, dh).astype(BF16)
            km = jnp.concatenate([k_lo, zero], axis=1).reshape(c, dh).astype(BF16)
            a = lax.dot_general(qm, km, nt_dims, preferred_element_type=F32)
            a_d = jnp.concatenate([a[:hb, :hb], a[hb:, hb:]], axis=0)
            att_d = a_d if att_d is None else jnp.where(code == idx, a_d, att_d)

        mid = b[hb - 1:hb, :]
        q_top = (q[hb:] * jnp.exp2(b[hb:] - mid)).astype(BF16)
        k_top = jnp.exp2(lk[:hb] + (mid - b[:hb])).astype(BF16)
        a_top = lax.dot_general(q_top, k_top, nt_dims, preferred_element_type=F32)

        q3 = q.reshape(nb, dg, dh)
        b3 = b.reshape(nb, dg, dh)
        bk3 = (b - lk).reshape(nb, dg, dh)
        compact = jnp.zeros((nb, dg, dh), F32)
        for s_loc in range(dg):
            z = q3 * jnp.exp2(b3 - bk3[:, s_loc:s_loc + 1, :])
            col = jnp.sum(z, axis=-1, keepdims=True)
            compact = jnp.where(diag_code == s_loc, col, compact)
        compact = compact.reshape(c, dh)
        a_diag = jnp.dot(compact.astype(BF16), expand, preferred_element_type=F32)
        att_d = jnp.where(code == -1, a_diag, att_d)

        att = jnp.concatenate([
            jnp.concatenate([att_d[:hb], jnp.zeros((hb, hb), F32)], axis=1),
            jnp.concatenate([a_top, att_d[hb:]], axis=1)], axis=0)
        o = o + jnp.dot(att.astype(BF16), v, preferred_element_type=F32)
        ms = jnp.mean(o * o, axis=-1, keepdims=True)
        y = (o * lax.rsqrt(ms + EPS)) * nw
        o_ref[rs, sl] = (y * gate_ref[rs, sl].astype(F32)).astype(BF16)


def _attn_blocks(sink_ref, q_ref, kc_ref, kp_ref, vc_ref, vp_ref, o_ref, n):
    blk = WINDOW
    nt_dims = (((1,), (1,)), ((), ()))
    k_all = jnp.concatenate([kp_ref[...], kc_ref[...]], axis=0)
    v_all = jnp.concatenate([vp_ref[...], vc_ref[...]], axis=0)
    k_sw = pltpu.roll(k_all.astype(F32), HEAD_DIM, 1).astype(BF16)
    v_sw = pltpu.roll(v_all.astype(F32), HEAD_DIM, 1).astype(BF16)
    lane = lax.broadcasted_iota(jnp.int32, (1, LANES), 1)
    m_lo = jnp.where(lane < HEAD_DIM, 1.0, 0.0).astype(BF16)
    m_hi = jnp.where(lane < HEAD_DIM, 0.0, 1.0).astype(BF16)

    qi = lax.broadcasted_iota(jnp.int32, (2 * blk, blk), 0) & (blk - 1)
    kj = lax.broadcasted_iota(jnp.int32, (2 * blk, blk), 1)
    own = kj <= qi
    top = lax.broadcasted_iota(jnp.int32, (2 * blk, 1), 0) < blk
    first_bias = jnp.where(n > 0, 0.0, -jnp.inf).astype(F32)

    for kh in range(KV_HEADS):
        own_k, other_k = (k_all, k_sw) if kh == 0 else (k_sw, k_all)
        own_v, other_v = (v_all, v_sw) if kh == 0 else (v_sw, v_all)
        kv_par = ((own_k * m_lo, own_v * m_lo), (other_k * m_hi, other_v * m_hi))
        base = kh * GROUP * HEAD_DIM
        ob = HGRN_WIDTH + base
        for j in range(MIX_TM // WINDOW):
            rows = slice(j * blk, (j + 1) * blk)
            keys = slice(j * blk, (j + 2) * blk)
            qs = jnp.concatenate([q_ref[rows, base:base + LANES], q_ref[rows, base + LANES:base + 2 * LANES]], axis=0)
            o = None
            for par, (kk, vv) in enumerate(kv_par):
                s = lax.dot_general(qs, kk[keys], nt_dims, preferred_element_type=F32)
                s_prev = s[:, :blk] + first_bias if j == 0 else s[:, :blk]
                sc = jnp.where(own, s[:, blk:], s_prev)
                sink = jnp.where(top, sink_ref[kh * GROUP + par], sink_ref[kh * GROUP + 2 + par]) * LOG2E
                mx = jnp.maximum(jnp.max(sc, axis=-1, keepdims=True), sink)
                p = jnp.exp2(sc - mx)
                den = jnp.sum(p, axis=-1, keepdims=True) + jnp.exp2(sink - mx)
                p2 = jnp.concatenate([jnp.where(own, 0.0, p), jnp.where(own, p, 0.0)], axis=1).astype(BF16)
                part = jnp.dot(p2, vv[keys], preferred_element_type=F32) / den
                o = part if o is None else o + part
            o_ref[rows, ob:ob + LANES] = o[:blk].astype(BF16)
            o_ref[rows, ob + LANES:ob + 2 * LANES] = o[blk:].astype(BF16)


def _mixer_kernel(sink_ref, nw_ref, pk_ref, gl_ref, tri_ref, code_ref, exp_ref, kp_ref, vp_ref, o_ref, st_ref):
    n = pl.program_id(1)

    @pl.when(n == 0)
    def _():
        st_ref[...] = jnp.zeros(st_ref.shape, F32)

    group = lambda ref, col, width: ref.at[:, col:col + width]
    hq_ref, hi_ref, gate_ref = (group(pk_ref, col, HGRN_WIDTH) for col in (PK_Q, PK_I, PK_G))
    g2_ref, lk_ref = group(gl_ref, 0, HGRN_WIDTH), group(gl_ref, HGRN_WIDTH, HGRN_WIDTH)
    aq_ref = group(pk_ref, PK_AQ, ATTN_WIDTH)
    kc_ref, vc_ref = group(pk_ref, PK_AK, KV_WIDTH), group(pk_ref, PK_AV, KV_WIDTH)

    for ch in range(MIX_TM // HGRN_CHUNK):
        rs = slice(ch * HGRN_CHUNK, (ch + 1) * HGRN_CHUNK)
        _hgrn_chunk(nw_ref, hq_ref, g2_ref, lk_ref, hi_ref, gate_ref, tri_ref, code_ref, exp_ref, o_ref, st_ref, rs)
    _attn_blocks(sink_ref, aq_ref, kc_ref, kp_ref, vc_ref, vp_ref, o_ref, n)


def _mixer(sinks_l, norm_w, pk, gl):
    bsz, t, _ = pk.shape
    w = HGRN_WIDTH
    c = HGRN_CHUNK
    tm = MIX_TM
    blk = WINDOW
    assert c == 2 * HGRN_HALF and HGRN_HALF == LANES and tm % c == 0 and tm % blk == 0
    tri = jnp.asarray(np.tril(np.ones((c, c))), BF16)
    code = jnp.asarray(_hgrn_code_table())
    expand_np = (np.arange(HGRN_DIM)[:, None] == (np.arange(HGRN_HALF)[None, :] % HGRN_DIAG)).astype(np.float32)
    expand = jnp.asarray(expand_np, BF16)
    cur = lambda width: pl.BlockSpec((None, tm, width), lambda b, i: (b, i, 0))
    prev = lambda col: pl.BlockSpec(
        (None, blk, KV_WIDTH), lambda b, i: (b, jnp.maximum(i * (tm // blk) - 1, 0), col // KV_WIDTH))
    return pl.pallas_call(
        _mixer_kernel,
        grid=(bsz, t // tm),
        in_specs=[
            pl.BlockSpec(memory_space=pltpu.SMEM),
            _resident((1, HGRN_DIM)),
            cur(PK_WIDTH), cur(2 * w),
            _resident((c, c)),
            _resident((c, HGRN_HALF)),
            _resident((HGRN_DIM, HGRN_HALF)),
            prev(PK_AK), prev(PK_AV),
        ],
        out_specs=cur(MIX_WIDTH),
        out_shape=jax.ShapeDtypeStruct((bsz, t, MIX_WIDTH), BF16),
        scratch_shapes=[pltpu.VMEM((HGRN_HEADS, HGRN_DIM, HGRN_DIM), F32)],
        compiler_params=_params(("arbitrary", "arbitrary")),
        name="mixer",
    )(sinks_l, norm_w.reshape(1, HGRN_DIM), pk, gl, tri, code, expand, pk, pk)


def _ffn_kernel(*refs, rows, tm, sub_rows, with_mix, n_cast):
    refs = list(refs)
    a_scr = refs.pop()
    cast_dst = [refs.pop() for _ in range(n_cast)][::-1]
    o_ref = refs.pop()
    cast_src = [refs.pop() for _ in range(n_cast)][::-1]
    if with_mix:
        x_ref, mod_ref, nw_ref, wg_ref, wu_ref, wd_ref, om_ref, wo_ref = refs
    else:
        x_ref, mod_ref, nw_ref, wg_ref, wu_ref, wd_ref = refs
    m = mod_ref[...]
    sh, sc, gate = (m[r:r + 1] for r in rows)
    for sub in range(tm // sub_rows):
        rs = slice(sub * sub_rows, (sub + 1) * sub_rows)
        x = x_ref[rs, :]
        if with_mix:
            x = x + m[5:6] * jnp.dot(om_ref[rs, :], wo_ref[...], preferred_element_type=F32)
        h = _rms_mod(x, nw_ref[...], sc, sh).astype(BF16)
        lo = 0
        for width in FFN_CHUNKS:
            gt = jnp.dot(h, wg_ref[:, lo:lo + width], preferred_element_type=F32)
            ut = jnp.dot(h, wu_ref[:, lo:lo + width], preferred_element_type=F32)
            a_scr[rs, lo:lo + width] = (_silu(gt) * ut).astype(BF16)
            lo += width
        y = jnp.dot(a_scr[rs, :], wd_ref[...], preferred_element_type=F32)
        o_ref[rs, :] = x + (0.5 * gate) * y
    for src_ref, dst_ref in zip(cast_src, cast_dst):
        dst_ref[...] = src_ref[...].astype(BF16)


def _ffn(x, mod, layer, nw, wg, wu, wd, rows, tm, sub_rows, mix=None, casts=()):
    bsz, t, d = x.shape
    dff = wg.shape[-1]
    assert sum(FFN_CHUNKS) == dff and tm % sub_rows == 0
    steps = t // tm
    tok = lambda w: pl.BlockSpec((None, tm, w), lambda b, i: (b, i, 0))
    in_specs = [
        tok(d),
        _mod_spec(mod, layer),
        _resident((1, d)),
        _resident(wg.shape),
        _resident(wu.shape),
        _resident(wd.shape),
    ]
    args = [x, mod, nw.reshape(1, d), wg, wu, wd]
    if mix is not None:
        o_mix, w_out = mix
        in_specs += [tok(o_mix.shape[-1]), _resident(w_out.shape)]
        args += [o_mix, w_out]
    cast_specs = [_cast_specs(arr, l, steps, bsz * steps) for arr, l in casts]
    in_specs += [s[0] for s in cast_specs]
    args += [arr for arr, _ in casts]
    return pl.pallas_call(
        functools.partial(_ffn_kernel, rows=rows, tm=tm, sub_rows=sub_rows, with_mix=mix is not None,
                          n_cast=len(casts)),
        grid=(bsz, steps),
        in_specs=in_specs,
        out_specs=[tok(d)] + [s[1] for s in cast_specs],
        out_shape=[jax.ShapeDtypeStruct((bsz, t, d), F32)] + [s[2] for s in cast_specs],
        scratch_shapes=[pltpu.VMEM((tm, dff), BF16)],
        compiler_params=_params(("arbitrary", "arbitrary")),
        name="ffn_mix" if mix is not None else "ffn",
    )(*args)


def kernel(x, c, positions, ada_w, ada_b, norm_ffn1, ffn1_w_gate, ffn1_w_up, ffn1_w_down, norm_mix, w_in,
           lb_logits, hgrn_norm, q_norm, k_norm, sinks, w_out, norm_ffn2, ffn2_w_gate, ffn2_w_up, ffn2_w_down):
    depth = ada_w.shape[0]
    mod = _modulation(c, ada_w, ada_b)
    wg1, wu1, wd1, cos_t, sin_t = _cast_layer_and_rope((ffn1_w_gate, ffn1_w_up, ffn1_w_down), 0, positions)
    for l in range(depth):
        x, w_in_l, wg2, wu2, wd2, w_out_l = _ffn(
            x, mod, l, norm_ffn1[l], wg1, wu1, wd1, rows=(0, 1, 2), tm=FFN_TM, sub_rows=FFN_SUB,
            casts=((w_in, l), (ffn2_w_gate, l), (ffn2_w_up, l), (ffn2_w_down, l), (w_out, l)))
        pk, gl = _project(x, mod, l, norm_mix[l], w_in_l, q_norm[l], k_norm[l], cos_t, sin_t, lb_logits,
                          tm=PROJ_TM, sub_rows=PROJ_SUB)
        o_mix = _mixer(sinks[l], hgrn_norm[l], pk, gl)
        nxt = ((ffn1_w_gate, l + 1), (ffn1_w_up, l + 1), (ffn1_w_down, l + 1)) if l + 1 < depth else ()
        x, *nxt_w = _ffn(x, mod, l, norm_ffn2[l], wg2, wu2, wd2, rows=(6, 7, 8), tm=FFN_TM, sub_rows=FFN_MIX_SUB,
                         mix=(o_mix, w_out_l), casts=nxt)
        if nxt_w:
            wg1, wu1, wd1 = nxt_w
    return x
```

```python
import functools

import numpy as np
import jax
import jax.numpy as jnp
from jax import lax
from jax.experimental import pallas as pl
from jax.experimental.pallas import tpu as pltpu

F32 = jnp.float32
BF16 = jnp.bfloat16

HGRN_HEADS = 4
HGRN_DIM = 128
HGRN_WIDTH = HGRN_HEADS * HGRN_DIM
ATTN_HEADS = 8
KV_HEADS = 2
HEAD_DIM = 64
ATTN_WIDTH = ATTN_HEADS * HEAD_DIM
KV_WIDTH = KV_HEADS * HEAD_DIM
MIX_WIDTH = HGRN_WIDTH + ATTN_WIDTH
PK_Q, PK_I, PK_G = 0, HGRN_WIDTH, 2 * HGRN_WIDTH
PK_AQ = 3 * HGRN_WIDTH
PK_AK = PK_AQ + ATTN_WIDTH
PK_AV = PK_AK + KV_WIDTH
PK_WIDTH = PK_AV + KV_WIDTH
GROUP = ATTN_HEADS // KV_HEADS
WINDOW = 128
ROPE_THETA = 500000.0
ROPE_DIM = HEAD_DIM // 4
N_MOD = 9
EPS = 1e-6

LANES = 128
VMEM_LIMIT_BYTES = 56 * 1024 * 1024

MOD_TK = 256
ROPE_TM = 1024
FFN_TM = 1024
FFN_SUB = 256
FFN_MIX_SUB = 512
PROJ_TM = 1024
PROJ_SUB = 256
FFN_CHUNKS = (1536, 1280)
HGRN_CHUNK = 256
HGRN_DIAG = 8
MIX_TM = 1024
HGRN_HALF = 128
LOG2E = 1.4426950408889634


def _params(sem):
    return pltpu.CompilerParams(dimension_semantics=sem, vmem_limit_bytes=VMEM_LIMIT_BYTES)


def _resident(shape):
    nd = len(shape)
    return pl.BlockSpec(shape, lambda *_: (0,) * nd, pipeline_mode=pl.Buffered(1))


def _cast_specs(arr, l, steps_inner, bands):
    _, r, c = arr.shape
    assert r % bands == 0 and (r // bands) % 16 == 0
    rb = r // bands
    return (pl.BlockSpec((None, rb, c), lambda b, i: (l, b * steps_inner + i, 0)),
            pl.BlockSpec((rb, c), lambda b, i: (b * steps_inner + i, 0)),
            jax.ShapeDtypeStruct((r, c), BF16))


def _mod_spec(mod, l):
    return pl.BlockSpec((None, None) + mod.shape[2:], lambda b, i: (l, b, 0, 0))


def _rms_mod(x, nw, sc, sh):
    ms = jnp.mean(x * x, axis=-1, keepdims=True)
    return (x * lax.rsqrt(ms + EPS)) * (nw * (1.0 + sc)) + sh


def _silu(x):
    return x * jax.nn.sigmoid(x)


def _cast_rope_kernel(pos_ref, invf_ref, ec_ref, es_ref, base_ref, *refs):
    n = (len(refs) - 2) // 2
    for src_ref, dst_ref in zip(refs[:n], refs[n:2 * n]):
        dst_ref[...] = src_ref[...].astype(BF16)
    _rope_rows(pos_ref, invf_ref, ec_ref, es_ref, base_ref, refs[2 * n], refs[2 * n + 1])


def _cast_layer_and_rope(arrs, l, positions):
    bsz, t = positions.shape
    steps = t // ROPE_TM
    bands = bsz * steps
    specs = [_cast_specs(a, l, steps, bands) for a in arrs]
    rope_in, rope_args, rope_out, rope_shape = _rope_specs(positions)
    return pl.pallas_call(
        _cast_rope_kernel,
        grid=(bsz, steps),
        in_specs=rope_in + [s[0] for s in specs],
        out_specs=[s[1] for s in specs] + rope_out,
        out_shape=[s[2] for s in specs] + rope_shape,
        compiler_params=_params(("arbitrary", "arbitrary")),
        name="cast_rope",
    )(*rope_args, *arrs)


def _mod_kernel(ct_ref, w_ref, b_ref, o_ref):
    @pl.when(pl.program_id(1) == 0)
    def _():
        o_ref[...] = jnp.broadcast_to(b_ref[...], o_ref.shape)

    ca = _silu(ct_ref[...])
    w = w_ref[...]
    rows = [jnp.sum(w * ca[:, b:b + 1], axis=0, keepdims=True) for b in range(ca.shape[1])]
    o_ref[...] += jnp.concatenate(rows, axis=0)


def _modulation(c, ada_w, ada_b):
    depth, d, n = ada_w.shape
    bsz = c.shape[0]
    out = pl.pallas_call(
        _mod_kernel,
        grid=(depth, d // MOD_TK),
        in_specs=[
            pl.BlockSpec((MOD_TK, bsz), lambda l, k: (k, 0)),
            pl.BlockSpec((None, MOD_TK, n), lambda l, k: (l, k, 0)),
            pl.BlockSpec((None, 1, n), lambda l, k: (l, 0, 0)),
        ],
        out_specs=pl.BlockSpec((None, bsz, n), lambda l, k: (l, 0, 0)),
        out_shape=jax.ShapeDtypeStruct((depth, bsz, n), F32),
        compiler_params=_params(("arbitrary", "arbitrary")),
        name="adaln_mod",
    )(c.T, ada_w, ada_b.reshape(depth, 1, n))
    return out.reshape(depth, bsz, N_MOD, d)


def _rope_rows(pos_ref, invf_ref, ec_ref, es_ref, base_ref, cos_ref, sin_ref):
    tn_dims = (((0,), (0,)), ((), ()))
    ang = invf_ref[...] * pos_ref[...].astype(F32)

    def expand(vals, e_ref):
        pad = jnp.zeros(vals.shape, BF16)
        stacked = jnp.concatenate(list(_split3(vals)) + [pad], axis=0)
        return lax.dot_general(stacked, e_ref[...], tn_dims, preferred_element_type=F32)

    cos_ref[...] = expand(jnp.cos(ang), ec_ref) + base_ref[...]
    sin_ref[...] = expand(jnp.sin(ang), es_ref)


def _rope_specs(positions):
    bsz, t = positions.shape
    nf = ROPE_DIM // 2
    inv_freq = ROPE_THETA ** (-jnp.arange(0, ROPE_DIM, 2, dtype=F32) / ROPE_DIM)
    lane = np.arange(2 * HEAD_DIM) % HEAD_DIM
    rot = lane < ROPE_DIM
    pick = (np.arange(nf)[:, None] == (lane % nf)[None, :]) & rot[None, :]
    e_cos = np.tile(pick.astype(np.float32), (4, 1))
    e_cos[3 * nf:] = 0.0
    e_sin = e_cos * np.where(lane < nf, -1.0, 1.0)[None, :]
    base = (~rot).astype(np.float32).reshape(1, 2 * HEAD_DIM)
    shp = jax.ShapeDtypeStruct((bsz, t, 2 * HEAD_DIM), F32)
    const = lambda shape: pl.BlockSpec(shape, lambda b, i: (0, 0))
    in_specs = [
        pl.BlockSpec((None, 1, ROPE_TM), lambda b, i: (b, 0, i)),
        const((nf, 1)),
        const((4 * nf, 2 * HEAD_DIM)),
        const((4 * nf, 2 * HEAD_DIM)),
        const((1, 2 * HEAD_DIM)),
    ]
    args = [positions.reshape(bsz, 1, t), inv_freq.reshape(nf, 1), jnp.asarray(e_cos, BF16), jnp.asarray(e_sin, BF16),
            jnp.asarray(base)]
    out_specs = [pl.BlockSpec((None, ROPE_TM, 2 * HEAD_DIM), lambda b, i: (b, i, 0))] * 2
    return in_specs, args, out_specs, [shp, shp]


def _head_norm_rope(t, nw, cos, sin, swap_lo):
    sq = t * t
    first = lax.broadcasted_iota(jnp.int32, t.shape, 1) < HEAD_DIM
    s_lo = jnp.sum(jnp.where(first, sq, 0.0), axis=-1, keepdims=True)
    s_all = jnp.sum(sq, axis=-1, keepdims=True)
    ms = jnp.where(first, s_lo, s_all - s_lo) * (1.0 / HEAD_DIM)
    y = (t * lax.rsqrt(ms + EPS)) * nw
    swapped = jnp.where(swap_lo, pltpu.roll(y, LANES - ROPE_DIM // 2, 1), pltpu.roll(y, ROPE_DIM // 2, 1))
    return y * cos + swapped * sin


def _lower_bound_logs(lbl, layer):
    e = jnp.exp(lbl - jnp.max(lbl, axis=0, keepdims=True))
    sm = e / jnp.sum(e, axis=0, keepdims=True)
    cum0 = sm[0:1]
    cum = cum0
    for j in range(1, layer + 1):
        cum = cum + sm[j:j + 1]
    lb = cum - cum0
    return jnp.log(lb), jnp.log1p(-lb)


def _proj_rows(x, m, nw, w_ref, qn, kn, cos, sin, lbl_ref, outs, rs, layer):
    pk_ref, f32_ref = outs
    h = _rms_mod(x, nw, m[4:5], m[3:4]).astype(BF16)

    proj = jnp.dot(h, w_ref[...], preferred_element_type=F32)

    def seg_dot(lo, width):
        return proj[:, lo:lo + width]

    pk_ref[rs, PK_Q:PK_Q + HGRN_WIDTH] = (seg_dot(0, HGRN_WIDTH) * (HGRN_DIM ** -0.5)).astype(BF16)

    fl2 = seg_dot(HGRN_WIDTH, HGRN_WIDTH) * LOG2E
    sp2 = jnp.log2(1.0 + jnp.exp2(-jnp.abs(fl2)))
    ls2 = jnp.minimum(fl2, 0.0) - sp2
    lsn2 = -(jnp.maximum(fl2, 0.0) + sp2)
    if layer == 0:
        f32_ref[rs, :HGRN_WIDTH] = ls2
        f32_ref[rs, HGRN_WIDTH:] = lsn2
    else:
        log_lb, l1m = _lower_bound_logs(lbl_ref[...], layer)
        log_lb2 = log_lb * LOG2E
        l1m2 = l1m * LOG2E
        cc2 = l1m2 + ls2
        f32_ref[rs, HGRN_WIDTH:] = l1m2 + lsn2
        dlt = log_lb2 - cc2
        f32_ref[rs, :HGRN_WIDTH] = jnp.maximum(log_lb2, cc2) + jnp.log2(1.0 + jnp.exp2(-jnp.abs(dlt)))

    pk_ref[rs, PK_I:PK_I + HGRN_WIDTH] = seg_dot(2 * HGRN_WIDTH, HGRN_WIDTH).astype(BF16)
    pk_ref[rs, PK_G:PK_G + HGRN_WIDTH] = _silu(seg_dot(3 * HGRN_WIDTH, HGRN_WIDTH)).astype(BF16)
    base = 4 * HGRN_WIDTH
    aq = seg_dot(base, ATTN_WIDTH)
    ak = seg_dot(base + ATTN_WIDTH, KV_WIDTH)
    pk_ref[rs, PK_AV:PK_AV + KV_WIDTH] = seg_dot(base + ATTN_WIDTH + KV_WIDTH, KV_WIDTH).astype(BF16)

    lane = lax.broadcasted_iota(jnp.int32, cos.shape, 1)
    swap_lo = (lane & (HEAD_DIM - 1)) < ROPE_DIM // 2
    for j in range(ATTN_WIDTH // LANES):
        sl = slice(j * LANES, (j + 1) * LANES)
        aq_ref_cols = slice(PK_AQ + j * LANES, PK_AQ + (j + 1) * LANES)
        pk_ref[rs, aq_ref_cols] = _head_norm_rope(aq[:, sl], qn, cos, sin, swap_lo).astype(BF16)
    for j in range(KV_WIDTH // LANES):
        sl = slice(j * LANES, (j + 1) * LANES)
        ak_ref_cols = slice(PK_AK + j * LANES, PK_AK + (j + 1) * LANES)
        pk_ref[rs, ak_ref_cols] = _head_norm_rope(ak[:, sl], kn, cos, sin, swap_lo).astype(BF16)


def _proj_kernel(x_ref, mod_ref, pnw_ref, win_ref, qn_ref, kn_ref, cos_ref, sin_ref, lbl_ref, *outs,
                 tm, sub_rows, layer):
    m = mod_ref[...]
    for sub in range(tm // sub_rows):
        rs = slice(sub * sub_rows, (sub + 1) * sub_rows)
        _proj_rows(x_ref[rs, :], m, pnw_ref[...], win_ref, qn_ref[...], kn_ref[...], cos_ref[rs, :], sin_ref[rs, :],
                   lbl_ref, outs, rs, layer)


def _project(x, mod, layer, pnw, w_in, q_norm, k_norm, cos_t, sin_t, lb_logits, tm, sub_rows):
    bsz, t, d = x.shape
    tok = lambda w: pl.BlockSpec((None, tm, w), lambda b, i: (b, i, 0))
    sds = lambda w, dt: jax.ShapeDtypeStruct((bsz, t, w), dt)
    qn = jnp.tile(q_norm * (LOG2E * HEAD_DIM ** -0.5), LANES // HEAD_DIM).reshape(1, LANES)
    kn = jnp.tile(k_norm, LANES // HEAD_DIM).reshape(1, LANES)
    return pl.pallas_call(
        functools.partial(_proj_kernel, tm=tm, sub_rows=sub_rows, layer=layer),
        grid=(bsz, t // tm),
        in_specs=[tok(d), _mod_spec(mod, layer), _resident((1, d)), _resident(w_in.shape),
                  _resident((1, LANES)), _resident((1, LANES)), tok(LANES), tok(LANES), _resident(lb_logits.shape)],
        out_specs=[tok(PK_WIDTH), tok(2 * HGRN_WIDTH)],
        out_shape=[sds(PK_WIDTH, BF16), sds(2 * HGRN_WIDTH, F32)],
        compiler_params=_params(("arbitrary", "arbitrary")),
        name="in_proj",
    )(x, mod, pnw.reshape(1, d), w_in, qn, kn, cos_t, sin_t, lb_logits)


def _hgrn_inner_levels():
    lv = []
    m = HGRN_DIAG
    while m < HGRN_HALF:
        lv.append(m)
        m *= 2
    return tuple(lv)


def _hgrn_code_table():
    n = HGRN_HALF
    t = np.arange(n)[:, None]
    s = np.arange(n)[None, :]
    x = t ^ s
    code = np.full((n, n), -2, np.int32)
    for idx, m in enumerate(_hgrn_inner_levels()):
        code[(x >= m) & (x < 2 * m) & (t > s)] = idx
    code[(t // HGRN_DIAG == s // HGRN_DIAG) & (t >= s)] = -1
    return np.tile(code, (HGRN_CHUNK // HGRN_HALF, 1))


def _split3(g):
    hi = g.astype(BF16)
    r1 = g - hi.astype(F32)
    mid = r1.astype(BF16)
    lo = (r1 - mid.astype(F32)).astype(BF16)
    return hi, mid, lo


def _hgrn_chunk(nw_ref, q_ref, g2_ref, lk_ref, i_ref, gate_ref, tri_ref, code_ref, exp_ref, o_ref, st_ref, rs):
    c = HGRN_CHUNK
    dg = HGRN_DIAG
    hb = HGRN_HALF
    dh = HGRN_DIM
    nt_dims = (((1,), (1,)), ((), ()))
    tn_dims = (((0,), (0,)), ((), ()))

    tri = tri_ref[...]
    b_all = sum(jnp.dot(tri, p, preferred_element_type=F32) for p in _split3(g2_ref[rs, :]))

    code = code_ref[...]
    expand = exp_ref[...]
    nw = nw_ref[...]
    nb = c // dg
    lane3 = lax.broadcasted_iota(jnp.int32, (1, dg, dh), 2)
    row3 = lax.broadcasted_iota(jnp.int32, (1, dg, dh), 1)
    diag_code = jnp.where(lane3 <= row3, lane3, -1)

    for hd in range(HGRN_HEADS):
        sl = slice(hd * dh, (hd + 1) * dh)
        q = q_ref[rs, sl].astype(F32)
        v = i_ref[rs, sl]
        b = b_all[:, sl]
        lk = lk_ref[rs, sl]
        b_last = b[c - 1:c, :]

        st = st_ref[hd]
        qq0 = (q * jnp.exp2(b)).astype(BF16)
        o = lax.dot_general(qq0, st.astype(BF16), nt_dims, preferred_element_type=F32)
        kk_end = jnp.exp2(lk + (b_last - b)).astype(BF16)
        st_ref[hd] = st * jnp.exp2(b_last) + lax.dot_general(v, kk_end, tn_dims, preferred_element_type=F32)

        att_d = None
        for idx, m in reversed(list(enumerate(_hgrn_inner_levels()))):
            shp = (c // (2 * m), 2 * m, dh)
            b3 = b.reshape(shp)
            mid = b3[:, m - 1:m, :]
            zero = jnp.zeros((shp[0], m, dh), F32)
            q_hi = q.reshape(shp)[:, m:, :] * jnp.exp2(b3[:, m:, :] - mid)
            k_lo = jnp.exp2(lk.reshape(shp)[:, :m, :] + (mid - b3[:, :m, :]))
            qm = jnp.concatenate([zero, q_hi], axis=1).reshape(c, dh).astype(BF16)
            km = jnp.concatenate([k_lo, zero], axis=1).reshape(c, dh).astype(BF16)
            a = lax.dot_general(qm, km, nt_dims, preferred_element_type=F32)
            a_d = jnp.concatenate([a[:hb, :hb], a[hb:, hb:]], axis=0)
            att_d = a_d if att_d is None else jnp.where(code == idx, a_d, att_d)

        mid = b[hb - 1:hb, :]
        q_top = (q[hb:] * jnp.exp2(b[hb:] - mid)).astype(BF16)
        k_top = jnp.exp2(lk[:hb] + (mid - b[:hb])).astype(BF16)
        a_top = lax.dot_general(q_top, k_top, nt_dims, preferred_element_type=F32)

        q3 = q.reshape(nb, dg, dh)
        b3 = b.reshape(nb, dg, dh)
        bk3 = (b - lk).reshape(nb, dg, dh)
        compact = jnp.zeros((nb, dg, dh), F32)
        for s_loc in range(dg):
            z = q3 * jnp.exp2(b3 - bk3[:, s_loc:s_loc + 1, :])
            col = jnp.sum(z, axis=-1, keepdims=True)
            compact = jnp.where(diag_code == s_loc, col, compact)
        compact = compact.reshape(c, dh)
        a_diag = jnp.dot(compact.astype(BF16), expand, preferred_element_type=F32)
        att_d = jnp.where(code == -1, a_diag, att_d)

        att = jnp.concatenate([
            jnp.concatenate([att_d[:hb], jnp.zeros((hb, hb), F32)], axis=1),
            jnp.concatenate([a_top, att_d[hb:]], axis=1)], axis=0)
        o = o + jnp.dot(att.astype(BF16), v, preferred_element_type=F32)
        ms = jnp.mean(o * o, axis=-1, keepdims=True)
        y = (o * lax.rsqrt(ms + EPS)) * nw
        o_ref[rs, sl] = (y * gate_ref[rs, sl].astype(F32)).astype(BF16)


def _attn_blocks(sink_ref, q_ref, kc_ref, kp_ref, vc_ref, vp_ref, o_ref, n):
    blk = WINDOW
    nt_dims = (((1,), (1,)), ((), ()))
    k_all = jnp.concatenate([kp_ref[...], kc_ref[...]], axis=0)
    v_all = jnp.concatenate([vp_ref[...], vc_ref[...]], axis=0)
    k_sw = pltpu.roll(k_all.astype(F32), HEAD_DIM, 1).astype(BF16)
    v_sw = pltpu.roll(v_all.astype(F32), HEAD_DIM, 1).astype(BF16)
    lane = lax.broadcasted_iota(jnp.int32, (1, LANES), 1)
    m_lo = jnp.where(lane < HEAD_DIM, 1.0, 0.0).astype(BF16)
    m_hi = jnp.where(lane < HEAD_DIM, 0.0, 1.0).astype(BF16)

    qi = lax.broadcasted_iota(jnp.int32, (2 * blk, blk), 0) & (blk - 1)
    kj = lax.broadcasted_iota(jnp.int32, (2 * blk, blk), 1)
    own = kj <= qi
    top = lax.broadcasted_iota(jnp.int32, (2 * blk, 1), 0) < blk
    first_bias = jnp.where(n > 0, 0.0, -jnp.inf).astype(F32)

    for kh in range(KV_HEADS):
        own_k, other_k = (k_all, k_sw) if kh == 0 else (k_sw, k_all)
        own_v, other_v = (v_all, v_sw) if kh == 0 else (v_sw, v_all)
        kv_par = ((own_k * m_lo, own_v * m_lo), (other_k * m_hi, other_v * m_hi))
        base = kh * GROUP * HEAD_DIM
        ob = HGRN_WIDTH + base
        for j in range(MIX_TM // WINDOW):
            rows = slice(j * blk, (j + 1) * blk)
            keys = slice(j * blk, (j + 2) * blk)
            qs = jnp.concatenate([q_ref[rows, base:base + LANES], q_ref[rows, base + LANES:base + 2 * LANES]], axis=0)
            o = None
            for par, (kk, vv) in enumerate(kv_par):
                s = lax.dot_general(qs, kk[keys], nt_dims, preferred_element_type=F32)
                s_prev = s[:, :blk] + first_bias if j == 0 else s[:, :blk]
                sc = jnp.where(own, s[:, blk:], s_prev)
                sink = jnp.where(top, sink_ref[kh * GROUP + par], sink_ref[kh * GROUP + 2 + par]) * LOG2E
                mx = jnp.maximum(jnp.max(sc, axis=-1, keepdims=True), sink)
                p = jnp.exp2(sc - mx)
                den = jnp.sum(p, axis=-1, keepdims=True) + jnp.exp2(sink - mx)
                p2 = jnp.concatenate([jnp.where(own, 0.0, p), jnp.where(own, p, 0.0)], axis=1).astype(BF16)
                part = jnp.dot(p2, vv[keys], preferred_element_type=F32) / den
                o = part if o is None else o + part
            o_ref[rows, ob:ob + LANES] = o[:blk].astype(BF16)
            o_ref[rows, ob + LANES:ob + 2 * LANES] = o[blk:].astype(BF16)


def _mixer_kernel(sink_ref, nw_ref, hq_ref, g2_ref, lk_ref, hi_ref, gate_ref, tri_ref, code_ref, exp_ref,
                  aq_ref, kc_ref, kp_ref, vc_ref, vp_ref, o_ref, st_ref):
    n = pl.program_id(1)

    @pl.when(n == 0)
    def _():
        st_ref[...] = jnp.zeros(st_ref.shape, F32)

    for ch in range(MIX_TM // HGRN_CHUNK):
        rs = slice(ch * HGRN_CHUNK, (ch + 1) * HGRN_CHUNK)
        _hgrn_chunk(nw_ref, hq_ref, g2_ref, lk_ref, hi_ref, gate_ref, tri_ref, code_ref, exp_ref, o_ref, st_ref, rs)
    _attn_blocks(sink_ref, aq_ref, kc_ref, kp_ref, vc_ref, vp_ref, o_ref, n)


def _mixer(sinks_l, norm_w, pk, gl):
    bsz, t, _ = pk.shape
    w = HGRN_WIDTH
    c = HGRN_CHUNK
    tm = MIX_TM
    blk = WINDOW
    assert c == 2 * HGRN_HALF and HGRN_HALF == LANES and tm % c == 0 and tm % blk == 0
    tri = jnp.asarray(np.tril(np.ones((c, c))), BF16)
    code = jnp.asarray(_hgrn_code_table())
    expand_np = (np.arange(HGRN_DIM)[:, None] == (np.arange(HGRN_HALF)[None, :] % HGRN_DIAG)).astype(np.float32)
    expand = jnp.asarray(expand_np, BF16)
    cur = lambda width, col: pl.BlockSpec((None, tm, width), lambda b, i: (b, i, col // width))
    prev = lambda width, col: pl.BlockSpec(
        (None, blk, width), lambda b, i: (b, jnp.maximum(i * (tm // blk) - 1, 0), col // width))
    out = lambda width: pl.BlockSpec((None, tm, width), lambda b, i: (b, i, 0))
    return pl.pallas_call(
        _mixer_kernel,
        grid=(bsz, t // tm),
        in_specs=[
            pl.BlockSpec(memory_space=pltpu.SMEM),
            _resident((1, HGRN_DIM)),
            cur(w, PK_Q), cur(w, 0), cur(w, w), cur(w, PK_I), cur(w, PK_G),
            _resident((c, c)),
            _resident((c, HGRN_HALF)),
            _resident((HGRN_DIM, HGRN_HALF)),
            cur(ATTN_WIDTH, PK_AQ), cur(KV_WIDTH, PK_AK), prev(KV_WIDTH, PK_AK), cur(KV_WIDTH, PK_AV),
            prev(KV_WIDTH, PK_AV),
        ],
        out_specs=out(MIX_WIDTH),
        out_shape=jax.ShapeDtypeStruct((bsz, t, MIX_WIDTH), BF16),
        scratch_shapes=[pltpu.VMEM((HGRN_HEADS, HGRN_DIM, HGRN_DIM), F32)],
        compiler_params=_params(("arbitrary", "arbitrary")),
        name="mixer",
    )(sinks_l, norm_w.reshape(1, HGRN_DIM), pk, gl, gl, pk, pk, tri, code, expand, pk, pk, pk, pk, pk)


def _ffn_kernel(*refs, rows, tm, sub_rows, with_mix, n_cast):
    refs = list(refs)
    a_scr = refs.pop()
    cast_dst = [refs.pop() for _ in range(n_cast)][::-1]
    o_ref = refs.pop()
    cast_src = [refs.pop() for _ in range(n_cast)][::-1]
    if with_mix:
        x_ref, mod_ref, nw_ref, wg_ref, wu_ref, wd_ref, om_ref, wo_ref = refs
    else:
        x_ref, mod_ref, nw_ref, wg_ref, wu_ref, wd_ref = refs
    m = mod_ref[...]
    sh, sc, gate = (m[r:r + 1] for r in rows)
    for sub in range(tm // sub_rows):
        rs = slice(sub * sub_rows, (sub + 1) * sub_rows)
        x = x_ref[rs, :]
        if with_mix:
            x = x + m[5:6] * jnp.dot(om_ref[rs, :], wo_ref[...], preferred_element_type=F32)
        h = _rms_mod(x, nw_ref[...], sc, sh).astype(BF16)
        lo = 0
        for width in FFN_CHUNKS:
            gt = jnp.dot(h, wg_ref[:, lo:lo + width], preferred_element_type=F32)
            ut = jnp.dot(h, wu_ref[:, lo:lo + width], preferred_element_type=F32)
            a_scr[rs, lo:lo + width] = (_silu(gt) * ut).astype(BF16)
            lo += width
        y = jnp.dot(a_scr[rs, :], wd_ref[...], preferred_element_type=F32)
        o_ref[rs, :] = x + (0.5 * gate) * y
    for src_ref, dst_ref in zip(cast_src, cast_dst):
        dst_ref[...] = src_ref[...].astype(BF16)


def _ffn(x, mod, layer, nw, wg, wu, wd, rows, tm, sub_rows, mix=None, casts=()):
    bsz, t, d = x.shape
    dff = wg.shape[-1]
    assert sum(FFN_CHUNKS) == dff and tm % sub_rows == 0
    steps = t // tm
    tok = lambda w: pl.BlockSpec((None, tm, w), lambda b, i: (b, i, 0))
    in_specs = [
        tok(d),
        _mod_spec(mod, layer),
        _resident((1, d)),
        _resident(wg.shape),
        _resident(wu.shape),
        _resident(wd.shape),
    ]
    args = [x, mod, nw.reshape(1, d), wg, wu, wd]
    if mix is not None:
        o_mix, w_out = mix
        in_specs += [tok(o_mix.shape[-1]), _resident(w_out.shape)]
        args += [o_mix, w_out]
    cast_specs = [_cast_specs(arr, l, steps, bsz * steps) for arr, l in casts]
    in_specs += [s[0] for s in cast_specs]
    args += [arr for arr, _ in casts]
    return pl.pallas_call(
        functools.partial(_ffn_kernel, rows=rows, tm=tm, sub_rows=sub_rows, with_mix=mix is not None,
                          n_cast=len(casts)),
        grid=(bsz, steps),
        in_specs=in_specs,
        out_specs=[tok(d)] + [s[1] for s in cast_specs],
        out_shape=[jax.ShapeDtypeStruct((bsz, t, d), F32)] + [s[2] for s in cast_specs],
        scratch_shapes=[pltpu.VMEM((tm, dff), BF16)],
        compiler_params=_params(("arbitrary", "arbitrary")),
        name="ffn_mix" if mix is not None else "ffn",
    )(*args)


def kernel(x, c, positions, ada_w, ada_b, norm_ffn1, ffn1_w_gate, ffn1_w_up, ffn1_w_down, norm_mix, w_in,
           lb_logits, hgrn_norm, q_norm, k_norm, sinks, w_out, norm_ffn2, ffn2_w_gate, ffn2_w_up, ffn2_w_down):
    depth = ada_w.shape[0]
    mod = _modulation(c, ada_w, ada_b)
    wg1, wu1, wd1, cos_t, sin_t = _cast_layer_and_rope((ffn1_w_gate, ffn1_w_up, ffn1_w_down), 0, positions)
    for l in range(depth):
        x, w_in_l, wg2, wu2, wd2, w_out_l = _ffn(
            x, mod, l, norm_ffn1[l], wg1, wu1, wd1, rows=(0, 1, 2), tm=FFN_TM, sub_rows=FFN_SUB,
            casts=((w_in, l), (ffn2_w_gate, l), (ffn2_w_up, l), (ffn2_w_down, l), (w_out, l)))
        pk, gl = _project(x, mod, l, norm_mix[l], w_in_l, q_norm[l], k_norm[l], cos_t, sin_t, lb_logits,
                          tm=PROJ_TM, sub_rows=PROJ_SUB)
        o_mix = _mixer(sinks[l], hgrn_norm[l], pk, gl)
        nxt = ((ffn1_w_gate, l + 1), (ffn1_w_up, l + 1), (ffn1_w_down, l + 1)) if l + 1 < depth else ()
        x, *nxt_w = _ffn(x, mod, l, norm_ffn2[l], wg2, wu2, wd2, rows=(6, 7, 8), tm=FFN_TM, sub_rows=FFN_MIX_SUB,
                         mix=(o_mix, w_out_l), casts=nxt)
        if nxt_w:
            wg1, wu1, wd1 = nxt_w
    return x
```

```python
import functools

import numpy as np
import jax
import jax.numpy as jnp
from jax import lax
from jax.experimental import pallas as pl
from jax.experimental.pallas import tpu as pltpu

F32 = jnp.float32
BF16 = jnp.bfloat16

HGRN_HEADS = 4
HGRN_DIM = 128
HGRN_WIDTH = HGRN_HEADS * HGRN_DIM
ATTN_HEADS = 8
KV_HEADS = 2
HEAD_DIM = 64
ATTN_WIDTH = ATTN_HEADS * HEAD_DIM
KV_WIDTH = KV_HEADS * HEAD_DIM
MIX_WIDTH = HGRN_WIDTH + ATTN_WIDTH
PK_Q, PK_I, PK_G = 0, HGRN_WIDTH, 2 * HGRN_WIDTH
PK_AQ = 3 * HGRN_WIDTH
PK_AK = PK_AQ + ATTN_WIDTH
PK_AV = PK_AK + KV_WIDTH
PK_WIDTH = PK_AV + KV_WIDTH
GROUP = ATTN_HEADS // KV_HEADS
WINDOW = 128
ROPE_THETA = 500000.0
ROPE_DIM = HEAD_DIM // 4
N_MOD = 9
EPS = 1e-6

LANES = 128
VMEM_LIMIT_BYTES = 56 * 1024 * 1024

MOD_TN = 2304
ROPE_TM = 1024
FFN_TM = 1024
FFN_SUB = 256
FFN_MIX_SUB = 256
PROJ_TM = 1024
PROJ_SUB = 256
FFN_CHUNKS = (1536, 1280)
HGRN_CHUNK = 256
HGRN_DIAG = 8
MIX_TM = 1024
HGRN_HALF = 128
LOG2E = 1.4426950408889634


def _params(sem):
    return pltpu.CompilerParams(dimension_semantics=sem, vmem_limit_bytes=VMEM_LIMIT_BYTES)


def _resident(shape):
    nd = len(shape)
    return pl.BlockSpec(shape, lambda *_: (0,) * nd, pipeline_mode=pl.Buffered(1))


def _cast_specs(arr, l, steps_inner, bands):
    _, r, c = arr.shape
    assert r % bands == 0 and (r // bands) % 16 == 0
    rb = r // bands
    return (pl.BlockSpec((None, rb, c), lambda b, i: (l, b * steps_inner + i, 0)),
            pl.BlockSpec((rb, c), lambda b, i: (b * steps_inner + i, 0)),
            jax.ShapeDtypeStruct((r, c), BF16))


def _mod_spec(mod, l):
    return pl.BlockSpec((None, None) + mod.shape[2:], lambda b, i: (l, b, 0, 0))


def _rms_mod(x, nw, sc, sh):
    ms = jnp.mean(x * x, axis=-1, keepdims=True)
    return (x * lax.rsqrt(ms + EPS)) * (nw * (1.0 + sc)) + sh


def _silu(x):
    return x * jax.nn.sigmoid(x)


def _cast_rope_kernel(pos_ref, invf_ref, ec_ref, es_ref, base_ref, *refs):
    n = (len(refs) - 2) // 2
    for src_ref, dst_ref in zip(refs[:n], refs[n:2 * n]):
        dst_ref[...] = src_ref[...].astype(BF16)
    _rope_rows(pos_ref, invf_ref, ec_ref, es_ref, base_ref, refs[2 * n], refs[2 * n + 1])


def _cast_layer_and_rope(arrs, l, positions):
    bsz, t = positions.shape
    steps = t // ROPE_TM
    bands = bsz * steps
    specs = [_cast_specs(a, l, steps, bands) for a in arrs]
    rope_in, rope_args, rope_out, rope_shape = _rope_specs(positions)
    return pl.pallas_call(
        _cast_rope_kernel,
        grid=(bsz, steps),
        in_specs=rope_in + [s[0] for s in specs],
        out_specs=[s[1] for s in specs] + rope_out,
        out_shape=[s[2] for s in specs] + rope_shape,
        compiler_params=_params(("arbitrary", "arbitrary")),
        name="cast_rope",
    )(*rope_args, *arrs)


def _mod_kernel(ct_ref, w_ref, b_ref, o_ref):
    ca = _silu(ct_ref[...])
    w = w_ref[...]
    rows = [jnp.sum(w * ca[:, b:b + 1], axis=0, keepdims=True) for b in range(ca.shape[1])]
    o_ref[...] = jnp.concatenate(rows, axis=0) + b_ref[...]


def _modulation(c, ada_w, ada_b):
    depth, d, n = ada_w.shape
    bsz = c.shape[0]
    out = pl.pallas_call(
        _mod_kernel,
        grid=(depth, n // MOD_TN),
        in_specs=[
            pl.BlockSpec((d, bsz), lambda l, j: (0, 0)),
            pl.BlockSpec((None, d, MOD_TN), lambda l, j: (l, 0, j)),
            pl.BlockSpec((None, 1, MOD_TN), lambda l, j: (l, 0, j)),
        ],
        out_specs=pl.BlockSpec((None, bsz, MOD_TN), lambda l, j: (l, 0, j)),
        out_shape=jax.ShapeDtypeStruct((depth, bsz, n), F32),
        compiler_params=_params(("arbitrary", "arbitrary")),
        name="adaln_mod",
    )(c.T, ada_w, ada_b.reshape(depth, 1, n))
    return out.reshape(depth, bsz, N_MOD, d)


def _rope_rows(pos_ref, invf_ref, ec_ref, es_ref, base_ref, cos_ref, sin_ref):
    tn_dims = (((0,), (0,)), ((), ()))
    ang = invf_ref[...] * pos_ref[...].astype(F32)

    def expand(vals, e_ref):
        pad = jnp.zeros(vals.shape, BF16)
        stacked = jnp.concatenate(list(_split3(vals)) + [pad], axis=0)
        return lax.dot_general(stacked, e_ref[...], tn_dims, preferred_element_type=F32)

    cos_ref[...] = expand(jnp.cos(ang), ec_ref) + base_ref[...]
    sin_ref[...] = expand(jnp.sin(ang), es_ref)


def _rope_specs(positions):
    bsz, t = positions.shape
    nf = ROPE_DIM // 2
    inv_freq = ROPE_THETA ** (-jnp.arange(0, ROPE_DIM, 2, dtype=F32) / ROPE_DIM)
    lane = np.arange(2 * HEAD_DIM) % HEAD_DIM
    rot = lane < ROPE_DIM
    pick = (np.arange(nf)[:, None] == (lane % nf)[None, :]) & rot[None, :]
    e_cos = np.tile(pick.astype(np.float32), (4, 1))
    e_cos[3 * nf:] = 0.0
    e_sin = e_cos * np.where(lane < nf, -1.0, 1.0)[None, :]
    base = (~rot).astype(np.float32).reshape(1, 2 * HEAD_DIM)
    shp = jax.ShapeDtypeStruct((bsz, t, 2 * HEAD_DIM), F32)
    const = lambda shape: pl.BlockSpec(shape, lambda b, i: (0, 0))
    in_specs = [
        pl.BlockSpec((None, 1, ROPE_TM), lambda b, i: (b, 0, i)),
        const((nf, 1)),
        const((4 * nf, 2 * HEAD_DIM)),
        const((4 * nf, 2 * HEAD_DIM)),
        const((1, 2 * HEAD_DIM)),
    ]
    args = [positions.reshape(bsz, 1, t), inv_freq.reshape(nf, 1), jnp.asarray(e_cos, BF16), jnp.asarray(e_sin, BF16),
            jnp.asarray(base)]
    out_specs = [pl.BlockSpec((None, ROPE_TM, 2 * HEAD_DIM), lambda b, i: (b, i, 0))] * 2
    return in_specs, args, out_specs, [shp, shp]


def _head_norm_rope(t, nw, cos, sin, swap_lo):
    sq = t * t
    first = lax.broadcasted_iota(jnp.int32, t.shape, 1) < HEAD_DIM
    s_lo = jnp.sum(jnp.where(first, sq, 0.0), axis=-1, keepdims=True)
    s_all = jnp.sum(sq, axis=-1, keepdims=True)
    ms = jnp.where(first, s_lo, s_all - s_lo) * (1.0 / HEAD_DIM)
    y = (t * lax.rsqrt(ms + EPS)) * nw
    swapped = jnp.where(swap_lo, pltpu.roll(y, LANES - ROPE_DIM // 2, 1), pltpu.roll(y, ROPE_DIM // 2, 1))
    return y * cos + swapped * sin


def _lower_bound_logs(lbl, layer):
    e = jnp.exp(lbl - jnp.max(lbl, axis=0, keepdims=True))
    sm = e / jnp.sum(e, axis=0, keepdims=True)
    cum0 = sm[0:1]
    cum = cum0
    for j in range(1, layer + 1):
        cum = cum + sm[j:j + 1]
    lb = cum - cum0
    return jnp.log(lb), jnp.log1p(-lb)


def _proj_rows(x, m, nw, w_ref, qn, kn, cos, sin, lbl_ref, outs, rs, layer):
    pk_ref, f32_ref = outs
    h = _rms_mod(x, nw, m[4:5], m[3:4]).astype(BF16)

    proj = jnp.dot(h, w_ref[...], preferred_element_type=F32)

    def seg_dot(lo, width):
        return proj[:, lo:lo + width]

    pk_ref[rs, PK_Q:PK_Q + HGRN_WIDTH] = (seg_dot(0, HGRN_WIDTH) * (HGRN_DIM ** -0.5)).astype(BF16)

    fl2 = seg_dot(HGRN_WIDTH, HGRN_WIDTH) * LOG2E
    sp2 = jnp.log2(1.0 + jnp.exp2(-jnp.abs(fl2)))
    ls2 = jnp.minimum(fl2, 0.0) - sp2
    lsn2 = -(jnp.maximum(fl2, 0.0) + sp2)
    if layer == 0:
        f32_ref[rs, :HGRN_WIDTH] = ls2
        f32_ref[rs, HGRN_WIDTH:] = lsn2
    else:
        log_lb, l1m = _lower_bound_logs(lbl_ref[...], layer)
        log_lb2 = log_lb * LOG2E
        l1m2 = l1m * LOG2E
        cc2 = l1m2 + ls2
        f32_ref[rs, HGRN_WIDTH:] = l1m2 + lsn2
        dlt = log_lb2 - cc2
        f32_ref[rs, :HGRN_WIDTH] = jnp.maximum(log_lb2, cc2) + jnp.log2(1.0 + jnp.exp2(-jnp.abs(dlt)))

    pk_ref[rs, PK_I:PK_I + HGRN_WIDTH] = seg_dot(2 * HGRN_WIDTH, HGRN_WIDTH).astype(BF16)
    pk_ref[rs, PK_G:PK_G + HGRN_WIDTH] = _silu(seg_dot(3 * HGRN_WIDTH, HGRN_WIDTH)).astype(BF16)
    base = 4 * HGRN_WIDTH
    aq = seg_dot(base, ATTN_WIDTH)
    ak = seg_dot(base + ATTN_WIDTH, KV_WIDTH)
    pk_ref[rs, PK_AV:PK_AV + KV_WIDTH] = seg_dot(base + ATTN_WIDTH + KV_WIDTH, KV_WIDTH).astype(BF16)

    lane = lax.broadcasted_iota(jnp.int32, cos.shape, 1)
    swap_lo = (lane & (HEAD_DIM - 1)) < ROPE_DIM // 2
    for j in range(ATTN_WIDTH // LANES):
        sl = slice(j * LANES, (j + 1) * LANES)
        aq_ref_cols = slice(PK_AQ + j * LANES, PK_AQ + (j + 1) * LANES)
        pk_ref[rs, aq_ref_cols] = _head_norm_rope(aq[:, sl], qn, cos, sin, swap_lo).astype(BF16)
    for j in range(KV_WIDTH // LANES):
        sl = slice(j * LANES, (j + 1) * LANES)
        ak_ref_cols = slice(PK_AK + j * LANES, PK_AK + (j + 1) * LANES)
        pk_ref[rs, ak_ref_cols] = _head_norm_rope(ak[:, sl], kn, cos, sin, swap_lo).astype(BF16)


def _proj_kernel(x_ref, mod_ref, pnw_ref, win_ref, qn_ref, kn_ref, cos_ref, sin_ref, lbl_ref, *outs,
                 tm, sub_rows, layer):
    m = mod_ref[...]
    for sub in range(tm // sub_rows):
        rs = slice(sub * sub_rows, (sub + 1) * sub_rows)
        _proj_rows(x_ref[rs, :], m, pnw_ref[...], win_ref, qn_ref[...], kn_ref[...], cos_ref[rs, :], sin_ref[rs, :],
                   lbl_ref, outs, rs, layer)


def _project(x, mod, layer, pnw, w_in, q_norm, k_norm, cos_t, sin_t, lb_logits, tm, sub_rows):
    bsz, t, d = x.shape
    tok = lambda w: pl.BlockSpec((None, tm, w), lambda b, i: (b, i, 0))
    sds = lambda w, dt: jax.ShapeDtypeStruct((bsz, t, w), dt)
    qn = jnp.tile(q_norm * (LOG2E * HEAD_DIM ** -0.5), LANES // HEAD_DIM).reshape(1, LANES)
    kn = jnp.tile(k_norm, LANES // HEAD_DIM).reshape(1, LANES)
    return pl.pallas_call(
        functools.partial(_proj_kernel, tm=tm, sub_rows=sub_rows, layer=layer),
        grid=(bsz, t // tm),
        in_specs=[tok(d), _mod_spec(mod, layer), _resident((1, d)), _resident(w_in.shape),
                  _resident((1, LANES)), _resident((1, LANES)), tok(LANES), tok(LANES), _resident(lb_logits.shape)],
        out_specs=[tok(PK_WIDTH), tok(2 * HGRN_WIDTH)],
        out_shape=[sds(PK_WIDTH, BF16), sds(2 * HGRN_WIDTH, F32)],
        compiler_params=_params(("arbitrary", "arbitrary")),
        name="in_proj",
    )(x, mod, pnw.reshape(1, d), w_in, qn, kn, cos_t, sin_t, lb_logits)


def _hgrn_inner_levels():
    lv = []
    m = HGRN_DIAG
    while m < HGRN_HALF:
        lv.append(m)
        m *= 2
    return tuple(lv)


def _hgrn_code_table():
    n = HGRN_HALF
    t = np.arange(n)[:, None]
    s = np.arange(n)[None, :]
    x = t ^ s
    code = np.full((n, n), -2, np.int32)
    for idx, m in enumerate(_hgrn_inner_levels()):
        code[(x >= m) & (x < 2 * m) & (t > s)] = idx
    code[(t // HGRN_DIAG == s // HGRN_DIAG) & (t >= s)] = -1
    return np.tile(code, (HGRN_CHUNK // HGRN_HALF, 1))


def _split3(g):
    hi = g.astype(BF16)
    r1 = g - hi.astype(F32)
    mid = r1.astype(BF16)
    lo = (r1 - mid.astype(F32)).astype(BF16)
    return hi, mid, lo


def _hgrn_chunk(nw_ref, q_ref, g2_ref, lk_ref, i_ref, gate_ref, tri_ref, code_ref, exp_ref, o_ref, st_ref, rs):
    c = HGRN_CHUNK
    dg = HGRN_DIAG
    hb = HGRN_HALF
    dh = HGRN_DIM
    nt_dims = (((1,), (1,)), ((), ()))
    tn_dims = (((0,), (0,)), ((), ()))

    tri = tri_ref[...]
    b_all = sum(jnp.dot(tri, p, preferred_element_type=F32) for p in _split3(g2_ref[rs, :]))

    code = code_ref[...]
    expand = exp_ref[...]
    nw = nw_ref[...]
    nb = c // dg
    lane3 = lax.broadcasted_iota(jnp.int32, (1, dg, dh), 2)
    row3 = lax.broadcasted_iota(jnp.int32, (1, dg, dh), 1)
    diag_code = jnp.where(lane3 <= row3, lane3, -1)

    for hd in range(HGRN_HEADS):
        sl = slice(hd * dh, (hd + 1) * dh)
        q = q_ref[rs, sl].astype(F32)
        v = i_ref[rs, sl]
        b = b_all[:, sl]
        lk = lk_ref[rs, sl]
        b_last = b[c - 1:c, :]

        st = st_ref[hd]
        qq0 = (q * jnp.exp2(b)).astype(BF16)
        o = lax.dot_general(qq0, st.astype(BF16), nt_dims, preferred_element_type=F32)
        kk_end = jnp.exp2(lk + (b_last - b)).astype(BF16)
        st_ref[hd] = st * jnp.exp2(b_last) + lax.dot_general(v, kk_end, tn_dims, preferred_element_type=F32)

        att_d = None
        for idx, m in reversed(list(enumerate(_hgrn_inner_levels()))):
            shp = (c // (2 * m), 2 * m, dh)
            b3 = b.reshape(shp)
            mid = b3[:, m - 1:m, :]
            zero = jnp.zeros((shp[0], m, dh), F32)
            q_hi = q.reshape(shp)[:, m:, :] * jnp.exp2(b3[:, m:, :] - mid)
            k_lo = jnp.exp2(lk.reshape(shp)[:, :m, :] + (mid - b3[:, :m, :]))
            qm = jnp.concatenate([zero, q_hi], axis=1).reshape(c, dh).astype(BF16)
            km = jnp.concatenate([k_lo, zero], axis=1).reshape(c, dh).astype(BF16)
            a = lax.dot_general(qm, km, nt_dims, preferred_element_type=F32)
            a_d = jnp.concatenate([a[:hb, :hb], a[hb:, hb:]], axis=0)
            att_d = a_d if att_d is None else jnp.where(code == idx, a_d, att_d)

        mid = b[hb - 1:hb, :]
        q_top = (q[hb:] * jnp.exp2(b[hb:] - mid)).astype(BF16)
        k_top = jnp.exp2(lk[:hb] + (mid - b[:hb])).astype(BF16)
        a_top = lax.dot_general(q_top, k_top, nt_dims, preferred_element_type=F32)

        q3 = q.reshape(nb, dg, dh)
        b3 = b.reshape(nb, dg, dh)
        bk3 = (b - lk).reshape(nb, dg, dh)
        compact = jnp.zeros((nb, dg, dh), F32)
        for s_loc in range(dg):
            z = q3 * jnp.exp2(b3 - bk3[:, s_loc:s_loc + 1, :])
            col = jnp.sum(z, axis=-1, keepdims=True)
            compact = jnp.where(diag_code == s_loc, col, compact)
        compact = compact.reshape(c, dh)
        a_diag = jnp.dot(compact.astype(BF16), expand, preferred_element_type=F32)
        att_d = jnp.where(code == -1, a_diag, att_d)

        att = jnp.concatenate([
            jnp.concatenate([att_d[:hb], jnp.zeros((hb, hb), F32)], axis=1),
            jnp.concatenate([a_top, att_d[hb:]], axis=1)], axis=0)
        o = o + jnp.dot(att.astype(BF16), v, preferred_element_type=F32)
        ms = jnp.mean(o * o, axis=-1, keepdims=True)
        y = (o * lax.rsqrt(ms + EPS)) * nw
        o_ref[rs, sl] = (y * gate_ref[rs, sl].astype(F32)).astype(BF16)


def _attn_blocks(sink_ref, q_ref, kc_ref, kp_ref, vc_ref, vp_ref, o_ref, n):
    blk = WINDOW
    nt_dims = (((1,), (1,)), ((), ()))
    k_all = jnp.concatenate([kp_ref[...], kc_ref[...]], axis=0)
    v_all = jnp.concatenate([vp_ref[...], vc_ref[...]], axis=0)
    k_sw = pltpu.roll(k_all.astype(F32), HEAD_DIM, 1).astype(BF16)
    v_sw = pltpu.roll(v_all.astype(F32), HEAD_DIM, 1).astype(BF16)
    lane = lax.broadcasted_iota(jnp.int32, (1, LANES), 1)
    m_lo = jnp.where(lane < HEAD_DIM, 1.0, 0.0).astype(BF16)
    m_hi = jnp.where(lane < HEAD_DIM, 0.0, 1.0).astype(BF16)

    qi = lax.broadcasted_iota(jnp.int32, (2 * blk, blk), 0) & (blk - 1)
    kj = lax.broadcasted_iota(jnp.int32, (2 * blk, blk), 1)
    own = kj <= qi
    top = lax.broadcasted_iota(jnp.int32, (2 * blk, 1), 0) < blk
    first_bias = jnp.where(n > 0, 0.0, -jnp.inf).astype(F32)

    for kh in range(KV_HEADS):
        own_k, other_k = (k_all, k_sw) if kh == 0 else (k_sw, k_all)
        own_v, other_v = (v_all, v_sw) if kh == 0 else (v_sw, v_all)
        kv_par = ((own_k * m_lo, own_v * m_lo), (other_k * m_hi, other_v * m_hi))
        base = kh * GROUP * HEAD_DIM
        ob = HGRN_WIDTH + base
        for j in range(MIX_TM // WINDOW):
            rows = slice(j * blk, (j + 1) * blk)
            keys = slice(j * blk, (j + 2) * blk)
            qs = jnp.concatenate([q_ref[rows, base:base + LANES], q_ref[rows, base + LANES:base + 2 * LANES]], axis=0)
            o = None
            for par, (kk, vv) in enumerate(kv_par):
                s = lax.dot_general(qs, kk[keys], nt_dims, preferred_element_type=F32)
                s_prev = s[:, :blk] + first_bias if j == 0 else s[:, :blk]
                sc = jnp.where(own, s[:, blk:], s_prev)
                sink = jnp.where(top, sink_ref[kh * GROUP + par], sink_ref[kh * GROUP + 2 + par]) * LOG2E
                mx = jnp.maximum(jnp.max(sc, axis=-1, keepdims=True), sink)
                p = jnp.exp2(sc - mx)
                den = jnp.sum(p, axis=-1, keepdims=True) + jnp.exp2(sink - mx)
                p2 = jnp.concatenate([jnp.where(own, 0.0, p), jnp.where(own, p, 0.0)], axis=1).astype(BF16)
                part = jnp.dot(p2, vv[keys], preferred_element_type=F32) / den
                o = part if o is None else o + part
            o_ref[rows, ob:ob + LANES] = o[:blk].astype(BF16)
            o_ref[rows, ob + LANES:ob + 2 * LANES] = o[blk:].astype(BF16)


def _mixer_kernel(sink_ref, nw_ref, hq_ref, g2_ref, lk_ref, hi_ref, gate_ref, tri_ref, code_ref, exp_ref,
                  aq_ref, kc_ref, kp_ref, vc_ref, vp_ref, o_ref, st_ref):
    n = pl.program_id(1)

    @pl.when(n == 0)
    def _():
        st_ref[...] = jnp.zeros(st_ref.shape, F32)

    for ch in range(MIX_TM // HGRN_CHUNK):
        rs = slice(ch * HGRN_CHUNK, (ch + 1) * HGRN_CHUNK)
        _hgrn_chunk(nw_ref, hq_ref, g2_ref, lk_ref, hi_ref, gate_ref, tri_ref, code_ref, exp_ref, o_ref, st_ref, rs)
    _attn_blocks(sink_ref, aq_ref, kc_ref, kp_ref, vc_ref, vp_ref, o_ref, n)


def _mixer(sinks_l, norm_w, pk, gl):
    bsz, t, _ = pk.shape
    w = HGRN_WIDTH
    c = HGRN_CHUNK
    tm = MIX_TM
    blk = WINDOW
    assert c == 2 * HGRN_HALF and HGRN_HALF == LANES and tm % c == 0 and tm % blk == 0
    tri = jnp.asarray(np.tril(np.ones((c, c))), BF16)
    code = jnp.asarray(_hgrn_code_table())
    expand_np = (np.arange(HGRN_DIM)[:, None] == (np.arange(HGRN_HALF)[None, :] % HGRN_DIAG)).astype(np.float32)
    expand = jnp.asarray(expand_np, BF16)
    cur = lambda width, col: pl.BlockSpec((None, tm, width), lambda b, i: (b, i, col // width))
    prev = lambda width, col: pl.BlockSpec(
        (None, blk, width), lambda b, i: (b, jnp.maximum(i * (tm // blk) - 1, 0), col // width))
    out = lambda width: pl.BlockSpec((None, tm, width), lambda b, i: (b, i, 0))
    return pl.pallas_call(
        _mixer_kernel,
        grid=(bsz, t // tm),
        in_specs=[
            pl.BlockSpec(memory_space=pltpu.SMEM),
            _resident((1, HGRN_DIM)),
            cur(w, PK_Q), cur(w, 0), cur(w, w), cur(w, PK_I), cur(w, PK_G),
            _resident((c, c)),
            _resident((c, HGRN_HALF)),
            _resident((HGRN_DIM, HGRN_HALF)),
            cur(ATTN_WIDTH, PK_AQ), cur(KV_WIDTH, PK_AK), prev(KV_WIDTH, PK_AK), cur(KV_WIDTH, PK_AV),
            prev(KV_WIDTH, PK_AV),
        ],
        out_specs=out(MIX_WIDTH),
        out_shape=jax.ShapeDtypeStruct((bsz, t, MIX_WIDTH), BF16),
        scratch_shapes=[pltpu.VMEM((HGRN_HEADS, HGRN_DIM, HGRN_DIM), F32)],
        compiler_params=_params(("arbitrary", "arbitrary")),
        name="mixer",
    )(sinks_l, norm_w.reshape(1, HGRN_DIM), pk, gl, gl, pk, pk, tri, code, expand, pk, pk, pk, pk, pk)


def _ffn_kernel(*refs, rows, tm, sub_rows, with_mix, n_cast):
    refs = list(refs)
    a_scr = refs.pop()
    cast_dst = [refs.pop() for _ in range(n_cast)][::-1]
    o_ref = refs.pop()
    cast_src = [refs.pop() for _ in range(n_cast)][::-1]
    if with_mix:
        x_ref, mod_ref, nw_ref, wg_ref, wu_ref, wd_ref, om_ref, wo_ref = refs
    else:
        x_ref, mod_ref, nw_ref, wg_ref, wu_ref, wd_ref = refs
    m = mod_ref[...]
    sh, sc, gate = (m[r:r + 1] for r in rows)
    if with_mix:
        mix = jnp.dot(om_ref[...], wo_ref[...], preferred_element_type=F32)
    for sub in range(tm // sub_rows):
        rs = slice(sub * sub_rows, (sub + 1) * sub_rows)
        x = x_ref[rs, :]
        if with_mix:
            x = x + m[5:6] * mix[rs, :]
        h = _rms_mod(x, nw_ref[...], sc, sh).astype(BF16)
        lo = 0
        for width in FFN_CHUNKS:
            gt = jnp.dot(h, wg_ref[:, lo:lo + width], preferred_element_type=F32)
            ut = jnp.dot(h, wu_ref[:, lo:lo + width], preferred_element_type=F32)
            a_scr[rs, lo:lo + width] = (_silu(gt) * ut).astype(BF16)
            lo += width
        y = jnp.dot(a_scr[rs, :], wd_ref[...], preferred_element_type=F32)
        o_ref[rs, :] = x + (0.5 * gate) * y
    for src_ref, dst_ref in zip(cast_src, cast_dst):
        dst_ref[...] = src_ref[...].astype(BF16)


def _ffn(x, mod, layer, nw, wg, wu, wd, rows, tm, sub_rows, mix=None, casts=()):
    bsz, t, d = x.shape
    dff = wg.shape[-1]
    assert sum(FFN_CHUNKS) == dff and tm % sub_rows == 0
    steps = t // tm
    tok = lambda w: pl.BlockSpec((None, tm, w), lambda b, i: (b, i, 0))
    in_specs = [
        tok(d),
        _mod_spec(mod, layer),
        _resident((1, d)),
        _resident(wg.shape),
        _resident(wu.shape),
        _resident(wd.shape),
    ]
    args = [x, mod, nw.reshape(1, d), wg, wu, wd]
    if mix is not None:
        o_mix, w_out = mix
        in_specs += [tok(o_mix.shape[-1]), _resident(w_out.shape)]
        args += [o_mix, w_out]
    cast_specs = [_cast_specs(arr, l, steps, bsz * steps) for arr, l in casts]
    in_specs += [s[0] for s in cast_specs]
    args += [arr for arr, _ in casts]
    return pl.pallas_call(
        functools.partial(_ffn_kernel, rows=rows, tm=tm, sub_rows=sub_rows, with_mix=mix is not None,
                          n_cast=len(casts)),
        grid=(bsz, steps),
        in_specs=in_specs,
        out_specs=[tok(d)] + [s[1] for s in cast_specs],
        out_shape=[jax.ShapeDtypeStruct((bsz, t, d), F32)] + [s[2] for s in cast_specs],
        scratch_shapes=[pltpu.VMEM((tm, dff), BF16)],
        compiler_params=_params(("arbitrary", "arbitrary")),
        name="ffn_mix" if mix is not None else "ffn",
    )(*args)


def kernel(x, c, positions, ada_w, ada_b, norm_ffn1, ffn1_w_gate, ffn1_w_up, ffn1_w_down, norm_mix, w_in,
           lb_logits, hgrn_norm, q_norm, k_norm, sinks, w_out, norm_ffn2, ffn2_w_gate, ffn2_w_up, ffn2_w_down):
    depth = ada_w.shape[0]
    mod = _modulation(c, ada_w, ada_b)
    wg1, wu1, wd1, cos_t, sin_t = _cast_layer_and_rope((ffn1_w_gate, ffn1_w_up, ffn1_w_down), 0, positions)
    for l in range(depth):
        x, w_in_l, wg2, wu2, wd2, w_out_l = _ffn(
            x, mod, l, norm_ffn1[l], wg1, wu1, wd1, rows=(0, 1, 2), tm=FFN_TM, sub_rows=FFN_SUB,
            casts=((w_in, l), (ffn2_w_gate, l), (ffn2_w_up, l), (ffn2_w_down, l), (w_out, l)))
        pk, gl = _project(x, mod, l, norm_mix[l], w_in_l, q_norm[l], k_norm[l], cos_t, sin_t, lb_logits,
                          tm=PROJ_TM, sub_rows=PROJ_SUB)
        o_mix = _mixer(sinks[l], hgrn_norm[l], pk, gl)
        nxt = ((ffn1_w_gate, l + 1), (ffn1_w_up, l + 1), (ffn1_w_down, l + 1)) if l + 1 < depth else ()
        x, *nxt_w = _ffn(x, mod, l, norm_ffn2[l], wg2, wu2, wd2, rows=(6, 7, 8), tm=FFN_TM, sub_rows=FFN_MIX_SUB,
                         mix=(o_mix, w_out_l), casts=nxt)
        if nxt_w:
            wg1, wu1, wd1 = nxt_w
    return x
```
